```python
import jax, jax.numpy as jnp
from jax import lax
import numpy as np

D_MODEL = 1024
BATCH = 8
SEQ = 2048
DEPTH = 1
DEC_BATCH = 128
DEC_SEQ = 8
PAST_LEN = 16384
PAGE_SIZE = 128

S5_WIDTH = D_MODEL // 2
S5_GROUP = 16
S5_NGROUPS = S5_WIDTH // S5_GROUP
S5_STATE = 64
RWKV_WIDTH = D_MODEL - S5_WIDTH
RWKV_HEAD = 64
RWKV_HEADS = RWKV_WIDTH // RWKV_HEAD
RWKV_IN = 4 * RWKV_WIDTH
IN_WIDTH = S5_WIDTH + RWKV_IN
MIX_WIDTH = S5_WIDTH + RWKV_WIDTH
LORA_DECAY = 64
LORA_AAA = 64
LORA_GATE = 128
DECAY_SCALE = 0.6065306597
N_MEM = 256
MEM_HEADS = 4
MEM_HEAD_DIM = D_MODEL // MEM_HEADS
MEM_SCALE = MEM_HEAD_DIM ** -0.5
N_EXPERTS = 32
TOP_K = 4
D_FF = D_MODEL
SWIGLU_LIMIT = 7.0
SWIGLU_ALPHA = 1.702
MOE_BLOCK = 128
RMS_EPS = 1e-5
GN_EPS = 64e-5

kernel_name = "hymba_s5_rwkv7_memxattn_moe_step"


def rmsnorm(x, g):
    xf = x.astype(jnp.float32)
    xf = xf * lax.rsqrt(jnp.mean(xf * xf, axis=-1, keepdims=True) + RMS_EPS)
    return xf.astype(x.dtype) * g


def _complex_affine_combine(e1, e2):
    a1r, a1i, b1r, b1i = e1
    a2r, a2i, b2r, b2i = e2
    return (a2r * a1r - a2i * a1i,
            a2r * a1i + a2i * a1r,
            a2r * b1r - a2i * b1i + b2r,
            a2r * b1i + a2i * b1r + b2i)


def s5_mixer(u, h0_re, h0_im, lam_re, lam_im, log_dt, b_re, b_im, c_re, c_im, d, glu_w, glu_b):
    f32 = jnp.float32
    bsz, t, _ = u.shape
    uf = u.astype(f32).reshape(bsz, t, S5_NGROUPS, S5_GROUP)
    dt = jnp.exp(log_dt.astype(f32))[:, None]
    lr = lam_re.astype(f32)
    li = lam_im.astype(f32)
    mag = jnp.exp(lr * dt)
    ar = mag * jnp.cos(li * dt)
    ai = mag * jnp.sin(li * dt)
    den = lr * lr + li * li
    cr = ((ar - 1.0) * lr + ai * li) / den
    ci = (ai * lr - (ar - 1.0) * li) / den
    br = b_re.astype(f32)
    bi = b_im.astype(f32)
    bbar_r = cr[..., None] * br - ci[..., None] * bi
    bbar_i = cr[..., None] * bi + ci[..., None] * br
    bu_r = jnp.einsum('btgc,gpc->btgp', uf, bbar_r)
    bu_i = jnp.einsum('btgc,gpc->btgp', uf, bbar_i)
    a_r = jnp.broadcast_to(ar, bu_r.shape)
    a_i = jnp.broadcast_to(ai, bu_i.shape)
    acc_r, acc_i, x_r, x_i = lax.associative_scan(_complex_affine_combine, (a_r, a_i, bu_r, bu_i), axis=1)
    if h0_re is not None:
        hr = h0_re.astype(f32)[:, None]
        hi = h0_im.astype(f32)[:, None]
        x_r, x_i = x_r + acc_r * hr - acc_i * hi, x_i + acc_r * hi + acc_i * hr
    y = (jnp.einsum('gcp,btgp->btgc', c_re.astype(f32), x_r)
         - jnp.einsum('gcp,btgp->btgc', c_im.astype(f32), x_i)
         + d.astype(f32).reshape(S5_NGROUPS, S5_GROUP) * uf)
    y = y.reshape(bsz, t, S5_WIDTH)
    z = jax.nn.gelu(y)
    out = z * jax.nn.sigmoid(z @ glu_w.astype(f32) + glu_b.astype(f32))
    return out.astype(u.dtype), x_r[:, -1], x_i[:, -1]


def rwkv7_mixer(p, prev, s0, mu, w0, w1, w2, a0, a1, a2, g1, g2, k_k, k_a, r_k, ln_w, ln_b):
    f32 = jnp.float32
    bsz, t, _ = p.shape
    pf = p.astype(f32)
    shifted = jnp.concatenate([prev.astype(f32)[:, None], pf[:, :-1]], axis=1)
    delta = shifted - pf
    r_p, k_p, v_p, z_p = jnp.split(pf, 4, axis=-1)
    r_d, k_d, v_d, z_d = jnp.split(delta, 4, axis=-1)
    mu = mu.astype(f32)
    r = r_p + r_d * mu[0]
    k = k_p + k_d * mu[1]
    v = v_p + v_d * mu[2]
    zw = z_p + z_d * mu[3]
    za = z_p + z_d * mu[4]
    zg = z_p + z_d * mu[5]
    decay = jnp.exp(-DECAY_SCALE * jax.nn.sigmoid(w0 + jnp.tanh(zw @ w1) @ w2))
    a = jax.nn.sigmoid(a0 + (za @ a1) @ a2)
    g = jax.nn.sigmoid(zg @ g1) @ g2
    hd = lambda z: z.astype(f32).reshape(bsz, t, RWKV_HEADS, RWKV_HEAD)
    r, decay, k, v, a = hd(r), hd(decay), hd(k), hd(v), hd(a)
    kk = k * k_k.astype(f32).reshape(RWKV_HEADS, RWKV_HEAD)
    kk = kk * lax.rsqrt(jnp.maximum(jnp.sum(kk * kk, axis=-1, keepdims=True), 1e-24))
    k = k * (1.0 + (a - 1.0) * k_a.astype(f32).reshape(RWKV_HEADS, RWKV_HEAD))

    def step(s, inp):
        r_t, w_t, k_t, v_t, kk_t, a_t = inp
        sa = jnp.einsum('bhvk,bhk->bhv', s, -kk_t)
        s = (s * w_t[:, :, None, :] + sa[..., None] * (kk_t * a_t)[:, :, None, :]
             + v_t[..., None] * k_t[:, :, None, :])
        return s, jnp.einsum('bhvk,bhk->bhv', s, r_t)

    xs = tuple(jnp.moveaxis(z, 1, 0) for z in (r, decay, k, v, kk, a))
    s_final, ys = lax.scan(step, s0.astype(f32), xs)
    y = jnp.moveaxis(ys, 0, 1)
    m = jnp.mean(y, axis=-1, keepdims=True)
    var = jnp.mean(jnp.square(y - m), axis=-1, keepdims=True)
    y = ((y - m) * lax.rsqrt(var + GN_EPS)).reshape(bsz, t, RWKV_WIDTH) * ln_w + ln_b
    bonus = (jnp.sum(r * k * r_k.astype(f32), axis=-1, keepdims=True) * v).reshape(bsz, t, RWKV_WIDTH)
    out = (y + bonus) * g
    return out.astype(p.dtype), p[:, -1], s_final


def memory_kv(mem, g_mem, w_mem_kv):
    m = rmsnorm(mem, g_mem) @ w_mem_kv
    mk, mv = jnp.split(m, 2, axis=-1)
    bsz = mem.shape[0]
    return (mk.reshape(bsz, N_MEM, MEM_HEADS, MEM_HEAD_DIM),
            mv.reshape(bsz, N_MEM, MEM_HEADS, MEM_HEAD_DIM))


def cross_attention(x, mem_k, mem_v, w_cq, w_co):
    bsz, t, _ = x.shape
    q = (x @ w_cq).reshape(bsz, t, MEM_HEADS, MEM_HEAD_DIM)
    s = jnp.einsum('bthd,bmhd->bhtm', q, mem_k.astype(q.dtype)).astype(jnp.float32) * MEM_SCALE
    prob = jax.nn.softmax(s, axis=-1)
    o = jnp.einsum('bhtm,bmhd->bthd', prob.astype(x.dtype), mem_v.astype(x.dtype))
    return o.reshape(bsz, t, D_MODEL) @ w_co


def moe_ffn(x, router_w, router_b, w_gu, b_gu, w_down, b_down):
    f32 = jnp.float32
    n_tok, d = x.shape
    n_assign = n_tok * TOP_K
    logits = x.astype(f32) @ router_w.astype(f32) + router_b.astype(f32)
    top_logit, top_e = lax.top_k(logits, TOP_K)
    gate = jax.nn.softmax(top_logit, axis=-1)
    flat_e = top_e.reshape(-1)
    flat_g = gate.reshape(-1)
    flat_t = jnp.arange(n_assign, dtype=jnp.int32) // TOP_K
    order = jnp.argsort(flat_e)
    sorted_e = flat_e[order]
    counts = jnp.bincount(flat_e, length=N_EXPERTS)
    padded = (counts + MOE_BLOCK - 1) // MOE_BLOCK * MOE_BLOCK
    pad_end = jnp.cumsum(padded)
    pad_start = pad_end - padded
    raw_start = jnp.cumsum(counts) - counts
    dest = pad_start[sorted_e] + jnp.arange(n_assign, dtype=jnp.int32) - raw_start[sorted_e]
    n_rows = -(-n_assign // MOE_BLOCK) * MOE_BLOCK + N_EXPERTS * MOE_BLOCK
    n_blocks = n_rows // MOE_BLOCK
    row_tok = jnp.zeros((n_rows,), jnp.int32).at[dest].set(flat_t[order])
    row_gate = jnp.zeros((n_rows,), f32).at[dest].set(flat_g[order])
    block_start = jnp.arange(n_blocks, dtype=jnp.int32) * MOE_BLOCK
    block_e = jnp.minimum(jnp.sum(pad_end[None, :] <= block_start[:, None], axis=1), N_EXPERTS - 1)
    xb = x[row_tok].reshape(n_blocks, MOE_BLOCK, d)

    def expert_block(args):
        xblk, e = args
        h = xblk @ w_gu[e] + b_gu[e]
        hg, hu = jnp.split(h, 2, axis=-1)
        hg = jnp.minimum(hg, SWIGLU_LIMIT)
        hu = jnp.clip(hu, -SWIGLU_LIMIT, SWIGLU_LIMIT)
        act = (hu + 1.0) * hg * jax.nn.sigmoid(SWIGLU_ALPHA * hg)
        return act @ w_down[e] + b_down[e]

    yb = lax.map(expert_block, (xb, block_e))
    y = jax.ops.segment_sum(yb.reshape(n_rows, d).astype(f32) * row_gate[:, None], row_tok, num_segments=n_tok)
    return y.astype(x.dtype)


def layer(x, mem_k, mem_v, h0_re, h0_im, prev, s0, P, l):
    bsz, t, _ = x.shape
    n1 = rmsnorm(x, P['norm_mix_g'][l])
    proj = n1 @ P['w_in'][l]
    u = proj[..., :S5_WIDTH]
    prk = proj[..., S5_WIDTH:]
    s5_y, h_re, h_im = s5_mixer(u, h0_re, h0_im, P['s5_lambda_re'][l], P['s5_lambda_im'][l], P['s5_log_dt'][l],
                                P['s5_b_re'][l], P['s5_b_im'][l], P['s5_c_re'][l], P['s5_c_im'][l],
                                P['s5_d'][l], P['s5_glu_w'][l], P['s5_glu_b'][l])
    rw_y, shift_last, s_final = rwkv7_mixer(prk, prev, s0, P['rwkv_mu'][l], P['rwkv_w0'][l], P['rwkv_w1'][l],
                                            P['rwkv_w2'][l], P['rwkv_a0'][l], P['rwkv_a1'][l], P['rwkv_a2'][l],
                                            P['rwkv_g1'][l], P['rwkv_g2'][l], P['rwkv_k_k'][l], P['rwkv_k_a'][l],
                                            P['rwkv_r_k'][l], P['rwkv_ln_w'][l], P['rwkv_ln_b'][l])
    x = x + jnp.concatenate([s5_y, rw_y], axis=-1) @ P['w_out'][l]
    x = x + cross_attention(rmsnorm(x, P['norm_ca_g'][l]), mem_k, mem_v, P['w_cq'][l], P['w_co'][l])
    h = rmsnorm(x, P['norm_ffn_g'][l]).reshape(bsz * t, D_MODEL)
    x = x + moe_ffn(h, P['router_w'][l], P['router_b'][l], P['ex_w_gu'][l], P['ex_b_gu'][l],
                    P['ex_w_down'][l], P['ex_b_down'][l]).reshape(bsz, t, D_MODEL)
    return x, h_re, h_im, shift_last, s_final


def setup_inputs(seed: int = 0) -> dict:
    key = jax.random.key(seed)
    ks = jax.random.split(key, 64)
    f32 = jnp.float32
    L = DEPTH
    nrm = lambda k, shape, scale: jax.random.normal(k, shape, f32) * scale
    uni = lambda k, shape, lo, hi: jax.random.uniform(k, shape, f32, lo, hi)
    inp = {}
    inp['x_prompt'] = nrm(ks[0], (BATCH, SEQ, D_MODEL), 1.0)
    inp['x_sample'] = nrm(ks[1], (DEC_BATCH, DEC_SEQ, D_MODEL), 1.0)
    inp['mem_prompt'] = nrm(ks[2], (BATCH, N_MEM, D_MODEL), 1.0)
    inp['state_s5_re'] = nrm(ks[3], (L, DEC_BATCH, S5_NGROUPS, S5_STATE), 0.5)
    inp['state_s5_im'] = nrm(ks[4], (L, DEC_BATCH, S5_NGROUPS, S5_STATE), 0.5)
    inp['state_rwkv_shift'] = nrm(ks[5], (L, DEC_BATCH, RWKV_IN), 1.0)
    inp['state_rwkv_wkv'] = nrm(ks[6], (L, DEC_BATCH, RWKV_HEADS, RWKV_HEAD, RWKV_HEAD), 0.3)
    inp['cache_mem_k'] = nrm(ks[7], (L, DEC_BATCH, N_MEM, MEM_HEADS, MEM_HEAD_DIM), 1.0)
    inp['cache_mem_v'] = nrm(ks[8], (L, DEC_BATCH, N_MEM, MEM_HEADS, MEM_HEAD_DIM), 1.0)
    inp['norm_mix_g'] = 1.0 + nrm(ks[9], (L, D_MODEL), 0.01)
    inp['w_in'] = nrm(ks[10], (L, D_MODEL, IN_WIDTH), D_MODEL ** -0.5)
    inp['s5_lambda_re'] = -0.5 + nrm(ks[11], (L, S5_NGROUPS, S5_STATE), 0.01)
    inp['s5_lambda_im'] = (jnp.pi * jnp.arange(S5_STATE, dtype=f32)) + nrm(ks[12], (L, S5_NGROUPS, S5_STATE), 0.01)
    inp['s5_log_dt'] = uni(ks[13], (L, S5_NGROUPS), float(np.log(0.001)), float(np.log(0.1)))
    inp['s5_b_re'] = nrm(ks[14], (L, S5_NGROUPS, S5_STATE, S5_GROUP), (2 * S5_GROUP) ** -0.5)
    inp['s5_b_im'] = nrm(ks[15], (L, S5_NGROUPS, S5_STATE, S5_GROUP), (2 * S5_GROUP) ** -0.5)
    inp['s5_c_re'] = nrm(ks[16], (L, S5_NGROUPS, S5_GROUP, S5_STATE), (2 * S5_STATE) ** -0.5)
    inp['s5_c_im'] = nrm(ks[17], (L, S5_NGROUPS, S5_GROUP, S5_STATE), (2 * S5_STATE) ** -0.5)
    inp['s5_d'] = nrm(ks[18], (L, S5_WIDTH), 1.0)
    inp['s5_glu_w'] = nrm(ks[19], (L, S5_WIDTH, S5_WIDTH), S5_WIDTH ** -0.5)
    inp['s5_glu_b'] = nrm(ks[20], (L, S5_WIDTH), 0.01)
    inp['rwkv_mu'] = uni(ks[21], (L, 6, RWKV_WIDTH), 0.0, 1.0)
    inp['rwkv_w0'] = nrm(ks[22], (L, RWKV_WIDTH), 1.0)
    inp['rwkv_w1'] = nrm(ks[23], (L, RWKV_WIDTH, LORA_DECAY), RWKV_WIDTH ** -0.5)
    inp['rwkv_w2'] = nrm(ks[24], (L, LORA_DECAY, RWKV_WIDTH), 0.1)
    inp['rwkv_a0'] = nrm(ks[25], (L, RWKV_WIDTH), 0.1)
    inp['rwkv_a1'] = nrm(ks[26], (L, RWKV_WIDTH, LORA_AAA), RWKV_WIDTH ** -0.5)
    inp['rwkv_a2'] = nrm(ks[27], (L, LORA_AAA, RWKV_WIDTH), 0.1)
    inp['rwkv_g1'] = nrm(ks[28], (L, RWKV_WIDTH, LORA_GATE), RWKV_WIDTH ** -0.5)
    inp['rwkv_g2'] = nrm(ks[29], (L, LORA_GATE, RWKV_WIDTH), LORA_GATE ** -0.5)
    inp['rwkv_k_k'] = 0.85 + nrm(ks[30], (L, RWKV_WIDTH), 0.05)
    inp['rwkv_k_a'] = 1.0 + nrm(ks[31], (L, RWKV_WIDTH), 0.05)
    inp['rwkv_r_k'] = nrm(ks[32], (L, RWKV_HEADS, RWKV_HEAD), 0.1)
    inp['rwkv_ln_w'] = 1.0 + nrm(ks[33], (L, RWKV_WIDTH), 0.01)
    inp['rwkv_ln_b'] = nrm(ks[34], (L, RWKV_WIDTH), 0.01)
    inp['w_out'] = nrm(ks[35], (L, MIX_WIDTH, D_MODEL), MIX_WIDTH ** -0.5)
    inp['norm_ca_g'] = 1.0 + nrm(ks[36], (L, D_MODEL), 0.01)
    inp['norm_mem_g'] = 1.0 + nrm(ks[37], (L, D_MODEL), 0.01)
    inp['w_cq'] = nrm(ks[38], (L, D_MODEL, D_MODEL), D_MODEL ** -0.5)
    inp['w_mem_kv'] = nrm(ks[39], (L, D_MODEL, 2 * D_MODEL), D_MODEL ** -0.5)
    inp['w_co'] = nrm(ks[40], (L, D_MODEL, D_MODEL), D_MODEL ** -0.5)
    inp['norm_ffn_g'] = 1.0 + nrm(ks[41], (L, D_MODEL), 0.01)
    inp['router_w'] = nrm(ks[42], (L, D_MODEL, N_EXPERTS), D_MODEL ** -0.5)
    inp['router_b'] = nrm(ks[43], (L, N_EXPERTS), 0.01)
    inp['ex_w_gu'] = nrm(ks[44], (L, N_EXPERTS, D_MODEL, 2 * D_FF), D_MODEL ** -0.5)
    inp['ex_b_gu'] = nrm(ks[45], (L, N_EXPERTS, 2 * D_FF), 0.01)
    inp['ex_w_down'] = nrm(ks[46], (L, N_EXPERTS, D_FF, D_MODEL), D_FF ** -0.5)
    inp['ex_b_down'] = nrm(ks[47], (L, N_EXPERTS, D_MODEL), 0.01)
    inp['norm_final_g'] = 1.0 + nrm(ks[48], (D_MODEL,), 0.01)
    return inp


def reference(x_prompt, x_sample, mem_prompt, state_s5_re, state_s5_im, state_rwkv_shift, state_rwkv_wkv,
              cache_mem_k, cache_mem_v, norm_mix_g, w_in, s5_lambda_re, s5_lambda_im, s5_log_dt,
              s5_b_re, s5_b_im, s5_c_re, s5_c_im, s5_d, s5_glu_w, s5_glu_b, rwkv_mu, rwkv_w0, rwkv_w1,
              rwkv_w2, rwkv_a0, rwkv_a1, rwkv_a2, rwkv_g1, rwkv_g2, rwkv_k_k, rwkv_k_a, rwkv_r_k,
              rwkv_ln_w, rwkv_ln_b, w_out, norm_ca_g, norm_mem_g, w_cq, w_mem_kv, w_co, norm_ffn_g,
              router_w, router_b, ex_w_gu, ex_b_gu, ex_w_down, ex_b_down, norm_final_g):
    P = dict(norm_mix_g=norm_mix_g, w_in=w_in, s5_lambda_re=s5_lambda_re, s5_lambda_im=s5_lambda_im,
             s5_log_dt=s5_log_dt, s5_b_re=s5_b_re, s5_b_im=s5_b_im, s5_c_re=s5_c_re, s5_c_im=s5_c_im,
             s5_d=s5_d, s5_glu_w=s5_glu_w, s5_glu_b=s5_glu_b, rwkv_mu=rwkv_mu, rwkv_w0=rwkv_w0,
             rwkv_w1=rwkv_w1, rwkv_w2=rwkv_w2, rwkv_a0=rwkv_a0, rwkv_a1=rwkv_a1, rwkv_a2=rwkv_a2,
             rwkv_g1=rwkv_g1, rwkv_g2=rwkv_g2, rwkv_k_k=rwkv_k_k, rwkv_k_a=rwkv_k_a, rwkv_r_k=rwkv_r_k,
             rwkv_ln_w=rwkv_ln_w, rwkv_ln_b=rwkv_ln_b, w_out=w_out, norm_ca_g=norm_ca_g, w_cq=w_cq,
             w_co=w_co, norm_ffn_g=norm_ffn_g, router_w=router_w, router_b=router_b, ex_w_gu=ex_w_gu,
             ex_b_gu=ex_b_gu, ex_w_down=ex_w_down, ex_b_down=ex_b_down)
    bp = x_prompt.shape[0]
    hp = x_prompt
    hs = x_sample
    p_re, p_im, p_sh, p_wkv, p_mk, p_mv = [], [], [], [], [], []
    s_re, s_im, s_sh, s_wkv = [], [], [], []
    for l in range(DEPTH):
        mk, mv = memory_kv(mem_prompt, norm_mem_g[l], w_mem_kv[l])
        prev0 = jnp.zeros((bp, RWKV_IN), x_prompt.dtype)
        s00 = jnp.zeros((bp, RWKV_HEADS, RWKV_HEAD, RWKV_HEAD), jnp.float32)
        hp, re_p, im_p, sh_p, wkv_p = layer(hp, mk, mv, None, None, prev0, s00, P, l)
        p_re.append(re_p); p_im.append(im_p); p_sh.append(sh_p); p_wkv.append(wkv_p)
        p_mk.append(mk); p_mv.append(mv)
        hs, re_s, im_s, sh_s, wkv_s = layer(hs, cache_mem_k[l], cache_mem_v[l], state_s5_re[l], state_s5_im[l],
                                            state_rwkv_shift[l], state_rwkv_wkv[l], P, l)
        s_re.append(re_s); s_im.append(im_s); s_sh.append(sh_s); s_wkv.append(wkv_s)
    y_prompt = rmsnorm(hp, norm_final_g)
    y_sample = rmsnorm(hs, norm_final_g)
    prompt_s5_re = jnp.stack(p_re)
    prompt_s5_im = jnp.stack(p_im)
    prompt_rwkv_shift = jnp.stack(p_sh)
    prompt_rwkv_wkv = jnp.stack(p_wkv)
    prompt_mem_k = jnp.stack(p_mk)
    prompt_mem_v = jnp.stack(p_mv)
    sample_s5_re = jnp.stack(s_re)
    sample_s5_im = jnp.stack(s_im)
    sample_rwkv_shift = jnp.stack(s_sh)
    sample_rwkv_wkv = jnp.stack(s_wkv)
    return (y_prompt, y_sample, prompt_s5_re, prompt_s5_im, prompt_rwkv_shift, prompt_rwkv_wkv,
            prompt_mem_k, prompt_mem_v, sample_s5_re, sample_s5_im, sample_rwkv_shift, sample_rwkv_wkv)
```

```python
import functools

import jax
import jax.numpy as jnp
from jax import lax
from jax.experimental import pallas as pl
from jax.experimental.pallas import tpu as pltpu

f32 = jnp.float32
bf16 = jnp.bfloat16

S5_GROUP = 16
S5_STATE = 64
RWKV_HEAD = 64
HEAD_PAIR = 2 * RWKV_HEAD
DECAY_SCALE = 0.6065306597
MEM_HEADS = 4
TOP_K = 4
SWIGLU_LIMIT = 7.0
SWIGLU_ALPHA = 1.702
RMS_EPS = 1e-5
GN_EPS = 64e-5

V7X_VMEM_LIMIT = 56 * 1024 * 1024
ROW_TILE = 512
MOE_ROWS = 256
WKV_CHUNK = 64
S5_CHUNK = 64

_NT = (((1,), (1,)), ((), ()))
_TN = (((0,), (0,)), ((), ()))


def _params(*sem):
    return pltpu.CompilerParams(dimension_semantics=sem, vmem_limit_bytes=V7X_VMEM_LIMIT)


def _dot(a, b):
    return jnp.dot(a.astype(bf16), b.astype(bf16), preferred_element_type=f32)


def _dg(a, b, dims):
    return lax.dot_general(a.astype(bf16), b.astype(bf16), dims, preferred_element_type=f32)


def _rms(x, g):
    return x * lax.rsqrt(jnp.mean(x * x, axis=-1, keepdims=True) + RMS_EPS) * g


def _full(shape):
    n = len(shape)
    return pl.BlockSpec(shape, lambda *_: (0,) * n)


def _tile(rows):
    t = min(ROW_TILE, rows)
    assert rows % t == 0
    return t


def _inproj_kernel(x_ref, g_ref, w_ref, u_ref, p_ref, *, s5w):
    nb = _rms(x_ref[...], g_ref[...]).astype(bf16)
    u_ref[...] = jnp.dot(nb, w_ref[:, :s5w], preferred_element_type=f32)
    p_ref[...] = jnp.dot(nb, w_ref[:, s5w:], preferred_element_type=f32)


def _inproj(x, g, w, s5w):
    rows, d = x.shape
    inw = w.shape[1]
    tm = _tile(rows)
    return pl.pallas_call(
        functools.partial(_inproj_kernel, s5w=s5w),
        grid=(rows // tm,),
        in_specs=[pl.BlockSpec((tm, d), lambda i: (i, 0)), _full((1, d)), _full((d, inw))],
        out_specs=[pl.BlockSpec((tm, s5w), lambda i: (i, 0)), pl.BlockSpec((tm, inw - s5w), lambda i: (i, 0))],
        out_shape=[jax.ShapeDtypeStruct((rows, s5w), f32), jax.ShapeDtypeStruct((rows, inw - s5w), f32)],
        compiler_params=_params("parallel"),
        name="inproj",
    )(x, g.reshape(1, d), w)


def _s5_kernel(u_ref, h0r_ref, h0i_ref, ar_ref, ai_ref, bre_ref, bim_ref, cre_ref, cim_ref, d_ref, gw_ref, gb_ref,
               y_ref, hr_ref, hi_ref, xr_s, xi_s, *, steps, bt, lane_chunk):
    i = pl.program_id(1)
    width = u_ref.shape[-1]
    nstate = ar_ref.shape[-1]

    @pl.when(i == 0)
    def _():
        hr_ref[...] = h0r_ref[...]
        hi_ref[...] = h0i_ref[...]

    u = u_ref[...].reshape(steps * bt, width)
    ub = u.astype(bf16)
    xr_s[...] = jnp.dot(ub, bre_ref[...], preferred_element_type=f32)
    xi_s[...] = jnp.dot(ub, bim_ref[...], preferred_element_type=f32)

    for c in range(nstate // lane_chunk):
        sl = slice(c * lane_chunk, (c + 1) * lane_chunk)
        arb = jnp.broadcast_to(ar_ref[:, sl], (bt, lane_chunk))
        aib = jnp.broadcast_to(ai_ref[:, sl], (bt, lane_chunk))

        def step(t, carry, sl=sl, arb=arb, aib=aib):
            hr, hi = carry
            r0 = pl.multiple_of(t * bt, bt)
            nhr = arb * hr - aib * hi + xr_s[pl.ds(r0, bt), sl]
            nhi = arb * hi + aib * hr + xi_s[pl.ds(r0, bt), sl]
            xr_s[pl.ds(r0, bt), sl] = nhr
            xi_s[pl.ds(r0, bt), sl] = nhi
            return nhr, nhi

        hr, hi = lax.fori_loop(0, steps, step, (hr_ref[:, sl], hi_ref[:, sl]), unroll=2)
        hr_ref[:, sl] = hr
        hi_ref[:, sl] = hi

    y = (jnp.dot(xr_s[...].astype(bf16), cre_ref[...], preferred_element_type=f32)
         + jnp.dot(xi_s[...].astype(bf16), cim_ref[...], preferred_element_type=f32)
         + d_ref[...] * u)
    z = jax.nn.gelu(y)
    out = z * jax.nn.sigmoid(_dot(z, gw_ref[...]) + gb_ref[...])
    y_ref[...] = out.reshape(steps, bt, width)


def _s5(u3, h0r, h0i, ar, ai, bre, bim, cre_m, cim_m, d, gw, gb):
    t, b, width = u3.shape
    nstate = ar.shape[-1]
    bt = 8
    steps = min(S5_CHUNK, t)
    assert b % bt == 0 and t % steps == 0
    kern = functools.partial(_s5_kernel, steps=steps, bt=bt, lane_chunk=512)
    st_spec = pl.BlockSpec((bt, nstate), lambda j, i: (j, 0))
    y, hr, hi = pl.pallas_call(
        kern,
        grid=(b // bt, t // steps),
        in_specs=[pl.BlockSpec((steps, bt, width), lambda j, i: (i, j, 0)), st_spec, st_spec,
                  _full((1, nstate)), _full((1, nstate)), _full((width, nstate)), _full((width, nstate)),
                  _full((nstate, width)), _full((nstate, width)), _full((1, width)), _full((width, width)),
                  _full((1, width))],
        out_specs=[pl.BlockSpec((steps, bt, width), lambda j, i: (i, j, 0)), st_spec, st_spec],
        out_shape=[jax.ShapeDtypeStruct((t, b, width), f32), jax.ShapeDtypeStruct((b, nstate), f32),
                   jax.ShapeDtypeStruct((b, nstate), f32)],
        scratch_shapes=[pltpu.VMEM((steps * bt, nstate), f32), pltpu.VMEM((steps * bt, nstate), f32)],
        compiler_params=_params("arbitrary", "arbitrary"),
        name="s5_mixer",
    )(u3, h0r, h0i, ar, ai, bre, bim, cre_m, cim_m, d, gw, gb)
    return y, hr, hi


def _s5_discretize(lam_re, lam_im, log_dt, b_re, b_im, c_re, c_im):
    g, p = lam_re.shape
    dt = jnp.exp(log_dt.astype(f32))[:, None]
    lr = lam_re.astype(f32)
    li = lam_im.astype(f32)
    mag = jnp.exp(lr * dt)
    ar = mag * jnp.cos(li * dt)
    ai = mag * jnp.sin(li * dt)
    den = lr * lr + li * li
    cr = ((ar - 1.0) * lr + ai * li) / den
    ci = (ai * lr - (ar - 1.0) * li) / den
    br = b_re.astype(f32)
    bi = b_im.astype(f32)
    bbar_r = cr[..., None] * br - ci[..., None] * bi
    bbar_i = cr[..., None] * bi + ci[..., None] * br
    eye = jnp.eye(g, dtype=f32)
    c = br.shape[-1]
    bd_in = lambda m: jnp.einsum('gpc,gh->gchp', m, eye).reshape(g * c, g * p).astype(bf16)
    bd_out = lambda m: jnp.einsum('gcp,gh->gphc', m, eye).reshape(g * p, g * c).astype(bf16)
    return (ar.reshape(1, g * p), ai.reshape(1, g * p), bd_in(bbar_r), bd_in(bbar_i),
            bd_out(c_re.astype(f32)), bd_out(-c_im.astype(f32)))


def _rwkv_kernel(p_ref, prev_ref, s0_ref, mu_ref, w0_ref, w1_ref, w2_ref, a0_ref, a1_ref, a2_ref, g1_ref, g2_ref,
                 kk_ref, ka_ref, rk_ref, lnw_ref, lnb_ref, y_ref, sout_ref, prev_s, st_s, *, steps, nsq):
    i = pl.program_id(1)
    L = steps
    L2 = 2 * L
    w = p_ref.shape[-1] // 4
    npair = w // HEAD_PAIR

    row2 = lax.broadcasted_iota(jnp.int32, (L2, HEAD_PAIR), 0)
    lane2 = lax.broadcasted_iota(jnp.int32, (L2, HEAD_PAIR), 1)
    own = (row2 < L) == (lane2 < RWKV_HEAD)
    ri = lax.broadcasted_iota(jnp.int32, (L2, L2), 0)
    ci = lax.broadcasted_iota(jnp.int32, (L2, L2), 1)
    same = (ri < L) == (ci < L)
    strict = same & (ci < ri)
    incl = same & (ci <= ri)
    eye = (ri == ci).astype(f32)
    sr = lax.broadcasted_iota(jnp.int32, (HEAD_PAIR, HEAD_PAIR), 0)
    sc = lax.broadcasted_iota(jnp.int32, (HEAD_PAIR, HEAD_PAIR), 1)
    sdiag = (sr < RWKV_HEAD) == (sc < RWKV_HEAD)

    @pl.when(i == 0)
    def _():
        prev_s[...] = prev_ref[0]
        for pr in range(npair):
            sp = s0_ref[0, pr * HEAD_PAIR:(pr + 1) * HEAD_PAIR, :]
            st_s[pr] = jnp.where(sdiag, jnp.concatenate([sp, sp], axis=1), 0.0)

    p = p_ref[...]
    trow = lax.broadcasted_iota(jnp.int32, (L, 1), 0)
    shifted = jnp.where(trow == 0, prev_s[...], pltpu.roll(p, 1, axis=0))
    prev_s[...] = p[L - 1:L, :]
    delta = shifted - p
    mu = mu_ref[...]
    pz = p[:, 3 * w:]
    dz = delta[:, 3 * w:]
    r = p[:, :w] + delta[:, :w] * mu[0:1]
    k = p[:, w:2 * w] + delta[:, w:2 * w] * mu[1:2]
    v = p[:, 2 * w:3 * w] + delta[:, 2 * w:3 * w] * mu[2:3]
    zw = pz + dz * mu[3:4]
    za = pz + dz * mu[4:5]
    zg = pz + dz * mu[5:6]
    logw = -DECAY_SCALE * jax.nn.sigmoid(w0_ref[...] + _dot(jnp.tanh(_dot(zw, w1_ref[...])), w2_ref[...]))
    a = jax.nn.sigmoid(a0_ref[...] + _dot(_dot(za, a1_ref[...]), a2_ref[...]))
    g = _dot(jax.nn.sigmoid(_dot(zg, g1_ref[...])), g2_ref[...])
    kkraw = k * kk_ref[...]
    kmod = k * (1.0 + (a - 1.0) * ka_ref[...])
    rk = r * kmod * rk_ref[...]

    tr = lax.broadcasted_iota(jnp.int32, (L, L), 0)
    tc = lax.broadcasted_iota(jnp.int32, (L, L), 1)
    tri = (tc <= tr).astype(bf16)
    hi = logw.astype(bf16)
    rem = logw - hi.astype(f32)
    mid = rem.astype(bf16)
    lo = (rem - mid.astype(f32)).astype(bf16)
    cum = (jnp.dot(tri, hi, preferred_element_type=f32) + jnp.dot(tri, mid, preferred_element_type=f32)
           + jnp.dot(tri, lo, preferred_element_type=f32))

    stack = lambda x: jnp.concatenate([x, x], axis=0)
    for pr in range(npair):
        sl = slice(pr * HEAD_PAIR, (pr + 1) * HEAD_PAIR)
        own_of = lambda x: jnp.where(own, stack(x[:, sl]), 0.0)
        cum_s = stack(cum[:, sl])
        wt = jnp.exp(cum_s)
        winv = jnp.exp(-cum_s)
        wprev = jnp.exp(cum_s - stack(logw[:, sl]))
        kkr = own_of(kkraw)
        kkn = kkr * lax.rsqrt(jnp.maximum(jnp.sum(kkr * kkr, axis=-1, keepdims=True), 1e-24))
        rm = (own_of(r) * wt).astype(bf16)
        km = (own_of(kmod) * winv).astype(bf16)
        bm = (kkn * stack(a[:, sl]) * winv).astype(bf16)
        kap = (kkn * wprev).astype(bf16)
        vmf = own_of(v)
        vm = vmf.astype(bf16)

        gram = _dg(jnp.concatenate([kap, rm], axis=0), jnp.concatenate([bm, km], axis=0), _NT)
        n_mat = jnp.where(strict, gram[:L2, :L2], 0.0)
        p_mat = jnp.where(strict, gram[:L2, L2:], 0.0)
        g_mat = jnp.where(incl, gram[L2:, :L2], 0.0)
        q_mat = jnp.where(incl, gram[L2:, L2:], 0.0)

        inv = eye - n_mat
        pw = n_mat
        for _ in range(nsq):
            pw = _dot(pw, pw)
            inv = inv + _dot(inv, pw)

        s_old = st_s[pr]
        um = -_dot(inv, _dg(kap, s_old, _NT) + _dot(p_mat, vm))
        ym = _dg(rm, s_old, _NT) + _dot(q_mat, vm) + _dot(g_mat, um)
        st_s[pr] = (s_old + _dg(vm, km, _TN) + _dg(um, bm, _TN)) * wt[L - 1:L, :]

        mean = jnp.sum(ym, axis=-1, keepdims=True) * (1.0 / RWKV_HEAD)
        yc = jnp.where(own, ym - mean, 0.0)
        var = jnp.sum(yc * yc, axis=-1, keepdims=True) * (1.0 / RWKV_HEAD)
        yn = yc * lax.rsqrt(var + GN_EPS)
        bonus = jnp.sum(own_of(rk), axis=-1, keepdims=True) * vmf
        mix = (yn[:L] + yn[L:]) * lnw_ref[:, sl] + lnb_ref[:, sl] + (bonus[:L] + bonus[L:])
        y_ref[:, sl] = mix * g[:, sl]

    @pl.when(i == pl.num_programs(1) - 1)
    def _():
        for pr in range(npair):
            s_new = st_s[pr]
            sout_ref[0, pr * HEAD_PAIR:(pr + 1) * HEAD_PAIR, :] = s_new[:, :RWKV_HEAD] + s_new[:, RWKV_HEAD:]


def _rwkv(prk, prev, s0, t, b, mu, w0, w1, w2, a0, a1, a2, g1, g2, k_k, k_a, r_k, ln_w, ln_b):
    w = prk.shape[-1] // 4
    steps = min(WKV_CHUNK, t)
    assert t % steps == 0 and w % HEAD_PAIR == 0
    nsq = max((steps - 1).bit_length() - 1, 0)
    row = lambda x: x.reshape(1, -1).astype(f32)
    wb = lambda x: x.astype(bf16)
    weights = [mu.astype(f32), row(w0), wb(w1), wb(w2), row(a0), wb(a1), wb(a2), wb(g1), wb(g2),
               row(k_k), row(k_a), row(r_k), row(ln_w), row(ln_b)]
    y, s_fin = pl.pallas_call(
        functools.partial(_rwkv_kernel, steps=steps, nsq=nsq),
        grid=(b, t // steps),
        in_specs=[pl.BlockSpec((steps, 4 * w), lambda j, i: (i, j)),
                  pl.BlockSpec((1, 1, 4 * w), lambda j, i: (j, 0, 0)),
                  pl.BlockSpec((1, w, RWKV_HEAD), lambda j, i: (j, 0, 0))]
                 + [_full(x.shape) for x in weights],
        out_specs=[pl.BlockSpec((steps, w), lambda j, i: (i, j)),
                   pl.BlockSpec((1, w, RWKV_HEAD), lambda j, i: (j, 0, 0))],
        out_shape=[jax.ShapeDtypeStruct((t, b * w), f32), jax.ShapeDtypeStruct((b, w, RWKV_HEAD), f32)],
        scratch_shapes=[pltpu.VMEM((1, 4 * w), f32), pltpu.VMEM((w // HEAD_PAIR, HEAD_PAIR, HEAD_PAIR), f32)],
        compiler_params=_params("arbitrary", "arbitrary"),
        name="rwkv7_mixer",
    )(prk.reshape(t, b * 4 * w), prev.reshape(b, 1, 4 * w).astype(f32), s0.reshape(b, w, RWKV_HEAD).astype(f32),
      *weights)
    return y.reshape(t * b, w), s_fin.reshape(b, w // RWKV_HEAD, RWKV_HEAD, RWKV_HEAD)


def _outproj_kernel(x_ref, s5_ref, rw_ref, wo_ref, g_ref, wq_ref, x1_ref, q_ref, *, s5w):
    x1 = x_ref[...] + _dot(s5_ref[...], wo_ref[:s5w, :]) + _dot(rw_ref[...], wo_ref[s5w:, :])
    x1_ref[...] = x1
    q_ref[...] = _dot(_rms(x1, g_ref[...]), wq_ref[...]).astype(bf16)


def _outproj(x, s5y, rwy, w_out, g_ca, w_cq):
    rows, d = x.shape
    s5w = s5y.shape[1]
    rww = rwy.shape[1]
    tm = _tile(rows)
    rt = lambda c: pl.BlockSpec((tm, c), lambda i: (i, 0))
    return pl.pallas_call(
        functools.partial(_outproj_kernel, s5w=s5w),
        grid=(rows // tm,),
        in_specs=[rt(d), rt(s5w), rt(rww), _full(w_out.shape), _full((1, d)), _full(w_cq.shape)],
        out_specs=[rt(d), rt(d)],
        out_shape=[jax.ShapeDtypeStruct((rows, d), f32), jax.ShapeDtypeStruct((rows, d), bf16)],
        compiler_params=_params("parallel"),
        name="outproj_q",
    )(x, s5y, rwy, w_out, g_ca.reshape(1, d), w_cq)


def _memkv_kernel(m_ref, g_ref, w_ref, k_ref, v_ref):
    d = m_ref.shape[-1]
    nb = _rms(m_ref[...], g_ref[...]).astype(bf16)
    k_ref[...] = jnp.dot(nb, w_ref[:, :d], preferred_element_type=f32)
    v_ref[...] = jnp.dot(nb, w_ref[:, d:], preferred_element_type=f32)


def _memkv(mem, g, w):
    rows, d = mem.shape
    tm = _tile(rows)
    rt = pl.BlockSpec((tm, d), lambda i: (i, 0))
    return pl.pallas_call(
        _memkv_kernel,
        grid=(rows // tm,),
        in_specs=[rt, _full((1, d)), _full(w.shape)],
        out_specs=[rt, rt],
        out_shape=[jax.ShapeDtypeStruct((rows, d), f32)] * 2,
        compiler_params=_params("parallel"),
        name="memory_kv",
    )(mem, g.reshape(1, d), w)


def _attn_kernel(q_ref, x1_ref, k_ref, v_ref, wo_ref, x2_ref):
    d = q_ref.shape[-1]
    hd = d // MEM_HEADS
    scale = hd ** -0.5
    q = q_ref[...]
    heads = []
    for h in range(MEM_HEADS):
        sl = slice(h * hd, (h + 1) * hd)
        s = _dg(q[:, sl], k_ref[0, :, sl], _NT) * scale
        e = jnp.exp(s - jnp.max(s, axis=-1, keepdims=True))
        prob = e / jnp.sum(e, axis=-1, keepdims=True)
        heads.append(_dot(prob, v_ref[0, :, sl]))
    o = jnp.concatenate(heads, axis=-1)
    x2_ref[...] = x1_ref[...] + _dot(o, wo_ref[...])


def _attn(q, x1, mk, mv, w_co, t, b):
    d = q.shape[-1]
    n_mem = mk.shape[1]
    tt = min(ROW_TILE, t)
    assert t % tt == 0
    xt = pl.BlockSpec((tt, d), lambda j, i: (i, j))
    kv = pl.BlockSpec((1, n_mem, d), lambda j, i: (j, 0, 0))
    x2 = pl.pallas_call(
        _attn_kernel,
        grid=(b, t // tt),
        in_specs=[xt, xt, kv, kv, _full(w_co.shape)],
        out_specs=xt,
        out_shape=jax.ShapeDtypeStruct((t, b * d), f32),
        compiler_params=_params("parallel", "parallel"),
        name="mem_attention",
    )(q.reshape(t, b * d), x1.reshape(t, b * d), mk, mv, w_co)
    return x2.reshape(t * b, d)


def _split3(x):
    hi = x.astype(bf16)
    rem = x - hi.astype(f32)
    return hi, rem.astype(bf16)


def _router_kernel(x_ref, g_ref, rw_ref, rb_ref, h_ref, e_ref, rank_ref, gate_ref, cnt_ref, *, n_exp):
    i = pl.program_id(0)
    tm = x_ref.shape[0]

    @pl.when(i == 0)
    def _():
        cnt_ref[...] = jnp.zeros_like(cnt_ref)

    h = _rms(x_ref[...], g_ref[...])
    h_ref[...] = h
    hh, hl = _split3(h)
    wh, wl = _split3(rw_ref[...])
    logits = (jnp.dot(hh, wh, preferred_element_type=f32) + jnp.dot(hh, wl, preferred_element_type=f32)
              + jnp.dot(hl, wh, preferred_element_type=f32) + rb_ref[...])

    lane = lax.broadcasted_iota(jnp.int32, (tm, n_exp), 1)
    work = logits
    sels, tops = [], []
    for _ in range(TOP_K):
        m = jnp.max(work, axis=-1, keepdims=True)
        idx = jnp.min(jnp.where(work == m, lane, n_exp), axis=-1, keepdims=True)
        sel = lane == idx
        sels.append(sel)
        tops.append(m)
        work = jnp.where(sel, -jnp.inf, work)
    chosen = sels[0] | sels[1] | sels[2] | sels[3]

    onehot = jnp.where(chosen, 1.0, 0.0)
    rr = lax.broadcasted_iota(jnp.int32, (tm, tm), 0)
    cc = lax.broadcasted_iota(jnp.int32, (tm, tm), 1)
    before = (cc < rr).astype(bf16)
    rank = jnp.dot(before, onehot.astype(bf16), preferred_element_type=f32) + cnt_ref[...]
    cnt_ref[...] = cnt_ref[...] + jnp.sum(onehot, axis=0, keepdims=True)

    den = sum(jnp.exp(t - tops[0]) for t in tops)
    e_cols, r_cols, g_cols = [], [], []
    for sel, top in zip(sels, tops):
        e_cols.append(jnp.sum(jnp.where(sel, lane, 0), axis=-1, keepdims=True))
        r_cols.append(jnp.sum(jnp.where(sel, rank, 0.0), axis=-1, keepdims=True))
        g_cols.append(jnp.exp(top - tops[0]) / den)
    e_ref[...] = jnp.concatenate(e_cols, axis=-1)
    rank_ref[...] = jnp.concatenate(r_cols, axis=-1).astype(jnp.int32)
    gate_ref[...] = jnp.concatenate(g_cols, axis=-1)


def _router(x, g, rw, rb):
    rows, d = x.shape
    n_exp = rw.shape[1]
    tm = _tile(rows)
    rt = lambda c: pl.BlockSpec((tm, c), lambda i: (i, 0))
    return pl.pallas_call(
        functools.partial(_router_kernel, n_exp=n_exp),
        grid=(rows // tm,),
        in_specs=[rt(d), _full((1, d)), _full((d, n_exp)), _full((1, n_exp))],
        out_specs=[rt(d), rt(TOP_K), rt(TOP_K), rt(TOP_K), _full((1, n_exp))],
        out_shape=[jax.ShapeDtypeStruct((rows, d), f32), jax.ShapeDtypeStruct((rows, TOP_K), jnp.int32),
                   jax.ShapeDtypeStruct((rows, TOP_K), jnp.int32), jax.ShapeDtypeStruct((rows, TOP_K), f32),
                   jax.ShapeDtypeStruct((1, n_exp), f32)],
        compiler_params=_params("arbitrary"),
        name="moe_router",
    )(x, g.reshape(1, d), rw.astype(f32), rb.reshape(1, n_exp).astype(f32))


def _expert_kernel(be_ref, nv_ref, tok_ref, h_hbm, wgu_ref, bgu_ref, wd_ref, bd_ref, y_ref,
                   xbuf, sem, wgu_b, wd_b, *, blk):
    i = pl.program_id(0)
    slot = i % 2
    nvalid = nv_ref[0]
    dff = wd_ref.shape[1]

    def gather(block, into):
        def issue(r, c):
            tok = tok_ref[block * blk + r]
            pltpu.make_async_copy(h_hbm.at[pl.ds(tok, 1)], xbuf.at[into, pl.ds(r, 1)], sem.at[into]).start()
            return c
        lax.fori_loop(0, blk, issue, 0, unroll=8)

    @pl.when(i == 0)
    def _():
        gather(0, 0)

    @pl.when(i + 1 < nvalid)
    def _():
        gather(i + 1, 1 - slot)

    @pl.when((i == 0) | (be_ref[i] != be_ref[jnp.maximum(i - 1, 0)]))
    def _():
        wgu_b[...] = wgu_ref[0].astype(bf16)
        wd_b[...] = wd_ref[0].astype(bf16)

    @pl.when((i < nvalid) | (i == 0))
    def _():
        pltpu.make_async_copy(xbuf.at[slot], xbuf.at[slot], sem.at[slot]).wait()
        x = xbuf[slot].astype(bf16)
        hgu = jnp.dot(x, wgu_b[...], preferred_element_type=f32) + bgu_ref[0]
        hg = jnp.minimum(hgu[:, :dff], SWIGLU_LIMIT)
        hu = jnp.clip(hgu[:, dff:], -SWIGLU_LIMIT, SWIGLU_LIMIT)
        act = (hu + 1.0) * hg * jax.nn.sigmoid(SWIGLU_ALPHA * hg)
        y_ref[...] = _dot(act, wd_b[...]) + bd_ref[0]

    @pl.when((i >= nvalid) & (i > 0))
    def _():
        y_ref[...] = jnp.zeros_like(y_ref)


def _experts(h_all, block_e, nvalid, row_tok, w_gu, b_gu, w_down, b_down, blk):
    n_exp, d, dff2 = w_gu.shape
    dff = dff2 // 2
    n_rows = row_tok.shape[0]
    n_blocks = n_rows // blk
    grid_spec = pltpu.PrefetchScalarGridSpec(
        num_scalar_prefetch=3,
        grid=(n_blocks,),
        in_specs=[pl.BlockSpec(memory_space=pl.ANY),
                  pl.BlockSpec((1, d, dff2), lambda i, be, nv, tok: (be[i], 0, 0)),
                  pl.BlockSpec((1, 1, dff2), lambda i, be, nv, tok: (be[i], 0, 0)),
                  pl.BlockSpec((1, dff, d), lambda i, be, nv, tok: (be[i], 0, 0)),
                  pl.BlockSpec((1, 1, d), lambda i, be, nv, tok: (be[i], 0, 0))],
        out_specs=pl.BlockSpec((blk, d), lambda i, be, nv, tok: (i, 0)),
        scratch_shapes=[pltpu.VMEM((2, blk, d), f32), pltpu.SemaphoreType.DMA((2,)),
                        pltpu.VMEM((d, dff2), bf16), pltpu.VMEM((dff, d), bf16)],
    )
    return pl.pallas_call(
        functools.partial(_expert_kernel, blk=blk),
        grid_spec=grid_spec,
        out_shape=jax.ShapeDtypeStruct((n_rows, d), f32),
        compiler_params=_params("arbitrary"),
        name="moe_experts",
    )(block_e, nvalid, row_tok, h_all, w_gu, b_gu.reshape(n_exp, 1, dff2), w_down, b_down.reshape(n_exp, 1, d))


def _combine_kernel(dest_ref, x_ref, gate_ref, yb_hbm, g_ref, x3_ref, y_ref, ybuf, sem, *, tm):
    i = pl.program_id(0)
    slot = i % 2

    def gather(tile, into):
        def issue(n, c):
            row = dest_ref[tile * tm * TOP_K + n]
            tok = n // TOP_K
            kk = n % TOP_K
            pltpu.make_async_copy(yb_hbm.at[pl.ds(row, 1)], ybuf.at[into, kk, pl.ds(tok, 1)], sem.at[into]).start()
            return c
        lax.fori_loop(0, tm * TOP_K, issue, 0, unroll=8)

    @pl.when(i == 0)
    def _():
        gather(0, 0)

    @pl.when(i + 1 < pl.num_programs(0))
    def _():
        gather(i + 1, 1 - slot)

    pltpu.make_async_copy(ybuf.at[slot], ybuf.at[slot], sem.at[slot]).wait()
    gate = gate_ref[...]
    acc = x_ref[...]
    moe = gate[:, 0:1] * ybuf[slot, 0]
    for kk in range(1, TOP_K):
        moe = moe + gate[:, kk:kk + 1] * ybuf[slot, kk]
    x3 = acc + moe
    x3_ref[...] = x3
    y_ref[...] = _rms(x3, g_ref[...])


def _combine(dest, x2, gate, yb, g_final):
    rows, d = x2.shape
    tm = min(256, rows)
    assert rows % tm == 0
    grid_spec = pltpu.PrefetchScalarGridSpec(
        num_scalar_prefetch=1,
        grid=(rows // tm,),
        in_specs=[pl.BlockSpec((tm, d), lambda i, dst: (i, 0)),
                  pl.BlockSpec((tm, TOP_K), lambda i, dst: (i, 0)),
                  pl.BlockSpec(memory_space=pl.ANY),
                  pl.BlockSpec((1, d), lambda i, dst: (0, 0))],
        out_specs=[pl.BlockSpec((tm, d), lambda i, dst: (i, 0)), pl.BlockSpec((tm, d), lambda i, dst: (i, 0))],
        scratch_shapes=[pltpu.VMEM((2, TOP_K, tm, d), f32), pltpu.SemaphoreType.DMA((2,))],
    )
    return pl.pallas_call(
        functools.partial(_combine_kernel, tm=tm),
        grid_spec=grid_spec,
        out_shape=[jax.ShapeDtypeStruct((rows, d), f32)] * 2,
        compiler_params=_params("arbitrary"),
        name="moe_combine",
    )(dest, x2, gate, yb, g_final.reshape(1, d))


def _moe(x2, g_ffn, router_w, router_b, w_gu, b_gu, w_down, b_down, g_final):
    n, d = x2.shape
    n_exp = router_w.shape[1]
    blk = MOE_ROWS
    h, top_e, top_rank, gate, counts = _router(x2, g_ffn, router_w, router_b)

    counts = counts.reshape(n_exp).astype(jnp.int32)
    padded = (counts + blk - 1) // blk * blk
    pad_end = jnp.cumsum(padded)
    pad_start = pad_end - padded
    dest = (pad_start[top_e] + top_rank).reshape(-1)
    n_rows = -(-(n * TOP_K) // blk) * blk + n_exp * blk
    n_blocks = n_rows // blk
    row_tok = jnp.zeros((n_rows,), jnp.int32).at[dest].set(
        jnp.arange(n * TOP_K, dtype=jnp.int32) // TOP_K, unique_indices=True)
    block_start = jnp.arange(n_blocks, dtype=jnp.int32) * blk
    block_e = jnp.minimum(jnp.sum(pad_end[None, :] <= block_start[:, None], axis=1), n_exp - 1).astype(jnp.int32)
    nvalid = (pad_end[-1:] // blk).astype(jnp.int32)

    yb = _experts(h, block_e, nvalid, row_tok, w_gu, b_gu, w_down, b_down, blk)
    return _combine(dest, x2, gate, yb, g_final)


def _mixers(x, t, b, h0r, h0i, prev, s0, P, l):
    s5w = P['s5_d'].shape[-1]
    u, prk = _inproj(x, P['norm_mix_g'][l], P['w_in'][l].astype(bf16), s5w)
    ar, ai, bre, bim, cre_m, cim_m = _s5_discretize(P['s5_lambda_re'][l], P['s5_lambda_im'][l], P['s5_log_dt'][l],
                                                    P['s5_b_re'][l], P['s5_b_im'][l], P['s5_c_re'][l], P['s5_c_im'][l])
    s5y, hr, hi = _s5(u.reshape(t, b, s5w), h0r, h0i, ar, ai, bre, bim, cre_m, cim_m,
                      P['s5_d'][l].reshape(1, s5w).astype(f32), P['s5_glu_w'][l].astype(bf16),
                      P['s5_glu_b'][l].reshape(1, s5w).astype(f32))
    rwy, s_fin = _rwkv(prk, prev, s0, t, b, P['rwkv_mu'][l], P['rwkv_w0'][l], P['rwkv_w1'][l], P['rwkv_w2'][l],
                       P['rwkv_a0'][l], P['rwkv_a1'][l], P['rwkv_a2'][l], P['rwkv_g1'][l], P['rwkv_g2'][l],
                       P['rwkv_k_k'][l], P['rwkv_k_a'][l], P['rwkv_r_k'][l], P['rwkv_ln_w'][l], P['rwkv_ln_b'][l])
    x1, q = _outproj(x, s5y.reshape(t * b, s5w), rwy, P['w_out'][l].astype(bf16), P['norm_ca_g'][l],
                     P['w_cq'][l].astype(bf16))
    shift_last = prk.reshape(t, b, -1)[t - 1]
    return x1, q, hr, hi, shift_last, s_fin


def kernel(x_prompt, x_sample, mem_prompt, state_s5_re, state_s5_im, state_rwkv_shift, state_rwkv_wkv, cache_mem_k, cache_mem_v, norm_mix_g, w_in, s5_lambda_re, s5_lambda_im, s5_log_dt, s5_b_re, s5_b_im, s5_c_re, s5_c_im, s5_d, s5_glu_w, s5_glu_b, rwkv_mu, rwkv_w0, rwkv_w1, rwkv_w2, rwkv_a0, rwkv_a1, rwkv_a2, rwkv_g1, rwkv_g2, rwkv_k_k, rwkv_k_a, rwkv_r_k, rwkv_ln_w, rwkv_ln_b, w_out, norm_ca_g, norm_mem_g, w_cq, w_mem_kv, w_co, norm_ffn_g, router_w, router_b, ex_w_gu, ex_b_gu, ex_w_down, ex_b_down, norm_final_g):
    P = dict(norm_mix_g=norm_mix_g, w_in=w_in, s5_lambda_re=s5_lambda_re, s5_lambda_im=s5_lambda_im,
             s5_log_dt=s5_log_dt, s5_b_re=s5_b_re, s5_b_im=s5_b_im, s5_c_re=s5_c_re, s5_c_im=s5_c_im,
             s5_d=s5_d, s5_glu_w=s5_glu_w, s5_glu_b=s5_glu_b, rwkv_mu=rwkv_mu, rwkv_w0=rwkv_w0,
             rwkv_w1=rwkv_w1, rwkv_w2=rwkv_w2, rwkv_a0=rwkv_a0, rwkv_a1=rwkv_a1, rwkv_a2=rwkv_a2,
             rwkv_g1=rwkv_g1, rwkv_g2=rwkv_g2, rwkv_k_k=rwkv_k_k, rwkv_k_a=rwkv_k_a, rwkv_r_k=rwkv_r_k,
             rwkv_ln_w=rwkv_ln_w, rwkv_ln_b=rwkv_ln_b, w_out=w_out, norm_ca_g=norm_ca_g, w_cq=w_cq)
    depth = norm_mix_g.shape[0]
    bp, tp, d = x_prompt.shape
    bs, ts, _ = x_sample.shape
    n_mem = mem_prompt.shape[1]
    s5_groups, s5_state = s5_lambda_re.shape[1:]
    nstate = s5_groups * s5_state
    rw_in = state_rwkv_shift.shape[-1]
    heads = state_rwkv_wkv.shape[2]
    np_rows = tp * bp

    xp = jnp.transpose(x_prompt, (1, 0, 2)).reshape(tp * bp, d)
    xs = jnp.transpose(x_sample, (1, 0, 2)).reshape(ts * bs, d)
    outs = {k: [] for k in ('p_re', 'p_im', 'p_sh', 'p_wkv', 'p_mk', 'p_mv', 's_re', 's_im', 's_sh', 's_wkv')}
    y_all = None
    for l in range(depth):
        mk, mv = _memkv(mem_prompt.reshape(bp * n_mem, d), norm_mem_g[l], w_mem_kv[l].astype(bf16))
        mk = mk.reshape(bp, n_mem, d)
        mv = mv.reshape(bp, n_mem, d)
        zeros_state = jnp.zeros((bp, nstate), f32)
        x1p, qp, re_p, im_p, sh_p, wkv_p = _mixers(
            xp, tp, bp, zeros_state, zeros_state, jnp.zeros((bp, rw_in), f32),
            jnp.zeros((bp, heads, RWKV_HEAD, RWKV_HEAD), f32), P, l)
        x1s, qs, re_s, im_s, sh_s, wkv_s = _mixers(
            xs, ts, bs, state_s5_re[l].reshape(bs, nstate), state_s5_im[l].reshape(bs, nstate),
            state_rwkv_shift[l], state_rwkv_wkv[l], P, l)
        wco = w_co[l].astype(bf16)
        x2p = _attn(qp, x1p, mk, mv, wco, tp, bp)
        x2s = _attn(qs, x1s, cache_mem_k[l].reshape(bs, n_mem, d), cache_mem_v[l].reshape(bs, n_mem, d), wco, ts, bs)
        x3, y_all = _moe(jnp.concatenate([x2p, x2s], axis=0), norm_ffn_g[l], router_w[l], router_b[l],
                         ex_w_gu[l], ex_b_gu[l], ex_w_down[l], ex_b_down[l], norm_final_g)
        xp, xs = x3[:np_rows], x3[np_rows:]
        mh = cache_mem_k.shape[3]
        outs['p_re'].append(re_p.reshape(bp, s5_groups, s5_state))
        outs['p_im'].append(im_p.reshape(bp, s5_groups, s5_state))
        outs['p_sh'].append(sh_p)
        outs['p_wkv'].append(wkv_p)
        outs['p_mk'].append(mk.reshape(bp, n_mem, mh, d // mh))
        outs['p_mv'].append(mv.reshape(bp, n_mem, mh, d // mh))
        outs['s_re'].append(re_s.reshape(bs, s5_groups, s5_state))
        outs['s_im'].append(im_s.reshape(bs, s5_groups, s5_state))
        outs['s_sh'].append(sh_s)
        outs['s_wkv'].append(wkv_s)
    y_prompt = jnp.transpose(y_all[:np_rows].reshape(tp, bp, d), (1, 0, 2))
    y_sample = jnp.transpose(y_all[np_rows:].reshape(ts, bs, d), (1, 0, 2))
    st = lambda k: jnp.stack(outs[k])
    return (y_prompt, y_sample, st('p_re'), st('p_im'), st('p_sh'), st('p_wkv'), st('p_mk'), st('p_mv'),
            st('s_re'), st('s_im'), st('s_sh'), st('s_wkv'))
```

```python
import functools

import jax
import jax.numpy as jnp
from jax import lax
from jax.experimental import pallas as pl
from jax.experimental.pallas import tpu as pltpu

f32 = jnp.float32
bf16 = jnp.bfloat16

RWKV_HEAD = 64
HEAD_PAIR = 2 * RWKV_HEAD
DECAY_SCALE = 0.6065306597
MEM_HEADS = 4
TOP_K = 4
SWIGLU_LIMIT = 7.0
SWIGLU_ALPHA = 1.702
RMS_EPS = 1e-5
GN_EPS = 64e-5

V7X_VMEM_LIMIT = 56 * 1024 * 1024
ROW_TILE = 512
MOE_ROWS = 256
COMBINE_TILE = 256
WKV_CHUNK = 64
WKV_BATCH = 2
S5_CHUNK = 64
S5_BLOCK_CHANNELS = 128
LANES = 128

_NT = (((1,), (1,)), ((), ()))
_TN = (((0,), (0,)), ((), ()))


def _params(*sem):
    return pltpu.CompilerParams(dimension_semantics=sem, vmem_limit_bytes=V7X_VMEM_LIMIT)


def _dot(a, b):
    return jnp.dot(a.astype(bf16), b.astype(bf16), preferred_element_type=f32)


def _dg(a, b, dims):
    return lax.dot_general(a.astype(bf16), b.astype(bf16), dims, preferred_element_type=f32)


def _rms(x, g):
    return x * lax.rsqrt(jnp.mean(x * x, axis=-1, keepdims=True) + RMS_EPS) * g


def _full(shape):
    n = len(shape)
    return pl.BlockSpec(shape, lambda *_: (0,) * n)


def _tile(rows, tile=ROW_TILE):
    t = min(tile, rows)
    assert rows % t == 0
    return t


def _inproj_kernel(x_ref, g_ref, w_ref, u_ref, p_ref, *, s5w):
    nb = _rms(x_ref[...], g_ref[...]).astype(bf16)
    u_ref[...] = jnp.dot(nb, w_ref[:, :s5w], preferred_element_type=f32)
    p_ref[...] = jnp.dot(nb, w_ref[:, s5w:], preferred_element_type=f32)


def _inproj(x, g, w, s5w):
    rows, d = x.shape
    inw = w.shape[1]
    tm = _tile(rows)
    return pl.pallas_call(
        functools.partial(_inproj_kernel, s5w=s5w),
        grid=(rows // tm,),
        in_specs=[pl.BlockSpec((tm, d), lambda i: (i, 0)), _full((1, d)), _full((d, inw))],
        out_specs=[pl.BlockSpec((tm, s5w), lambda i: (i, 0)), pl.BlockSpec((tm, inw - s5w), lambda i: (i, 0))],
        out_shape=[jax.ShapeDtypeStruct((rows, s5w), f32), jax.ShapeDtypeStruct((rows, inw - s5w), f32)],
        compiler_params=_params("parallel"),
        name="inproj",
    )(x, g.reshape(1, d), w)


def _s5_kernel(u_ref, h0r_ref, h0i_ref, ar_ref, ai_ref, bre_ref, bim_ref, cre_ref, cim_ref, d_ref, gw_ref, gb_ref,
               y_ref, hr_ref, hi_ref, xr_s, xi_s, *, steps, bt, lane_chunk):
    i = pl.program_id(1)
    width = u_ref.shape[-1]
    nstate = ar_ref.shape[-1]

    @pl.when(i == 0)
    def _():
        hr_ref[...] = h0r_ref[...]
        hi_ref[...] = h0i_ref[...]

    u = u_ref[...].reshape(steps * bt, width)
    ub = u.astype(bf16)
    nblk, cb, sb = bre_ref.shape
    for j in range(nblk):
        uj = ub[:, j * cb:(j + 1) * cb]
        xr_s[:, j * sb:(j + 1) * sb] = jnp.dot(uj, bre_ref[j], preferred_element_type=f32)
        xi_s[:, j * sb:(j + 1) * sb] = jnp.dot(uj, bim_ref[j], preferred_element_type=f32)

    for c in range(nstate // lane_chunk):
        sl = slice(c * lane_chunk, (c + 1) * lane_chunk)
        arb = jnp.broadcast_to(ar_ref[:, sl], (bt, lane_chunk))
        aib = jnp.broadcast_to(ai_ref[:, sl], (bt, lane_chunk))

        def step(t, carry, sl=sl, arb=arb, aib=aib):
            hr, hi = carry
            r0 = pl.multiple_of(t * bt, bt)
            nhr = arb * hr - aib * hi + xr_s[pl.ds(r0, bt), sl]
            nhi = arb * hi + aib * hr + xi_s[pl.ds(r0, bt), sl]
            xr_s[pl.ds(r0, bt), sl] = nhr
            xi_s[pl.ds(r0, bt), sl] = nhi
            return nhr, nhi

        hr, hi = lax.fori_loop(0, steps, step, (hr_ref[:, sl], hi_ref[:, sl]), unroll=2)
        hr_ref[:, sl] = hr
        hi_ref[:, sl] = hi

    y = jnp.concatenate(
        [jnp.dot(xr_s[:, j * sb:(j + 1) * sb].astype(bf16), cre_ref[j], preferred_element_type=f32)
         + jnp.dot(xi_s[:, j * sb:(j + 1) * sb].astype(bf16), cim_ref[j], preferred_element_type=f32)
         for j in range(nblk)], axis=-1) + d_ref[...] * u
    z = jax.nn.gelu(y)
    out = z * jax.nn.sigmoid(_dot(z, gw_ref[...]) + gb_ref[...])
    y_ref[...] = out.reshape(steps, bt, width)


def _s5(u3, h0r, h0i, ar, ai, bre, bim, cre_m, cim_m, d, gw, gb):
    t, b, width = u3.shape
    nstate = ar.shape[-1]
    bt = 8
    steps = min(S5_CHUNK, t)
    assert b % bt == 0 and t % steps == 0
    kern = functools.partial(_s5_kernel, steps=steps, bt=bt, lane_chunk=512)
    st_spec = pl.BlockSpec((bt, nstate), lambda j, i: (j, 0))
    y, hr, hi = pl.pallas_call(
        kern,
        grid=(b // bt, t // steps),
        in_specs=[pl.BlockSpec((steps, bt, width), lambda j, i: (i, j, 0)), st_spec, st_spec,
                  _full((1, nstate)), _full((1, nstate)), _full(bre.shape), _full(bim.shape),
                  _full(cre_m.shape), _full(cim_m.shape), _full((1, width)), _full((width, width)),
                  _full((1, width))],
        out_specs=[pl.BlockSpec((steps, bt, width), lambda j, i: (i, j, 0)), st_spec, st_spec],
        out_shape=[jax.ShapeDtypeStruct((t, b, width), f32), jax.ShapeDtypeStruct((b, nstate), f32),
                   jax.ShapeDtypeStruct((b, nstate), f32)],
        scratch_shapes=[pltpu.VMEM((steps * bt, nstate), f32), pltpu.VMEM((steps * bt, nstate), f32)],
        compiler_params=_params("arbitrary", "arbitrary"),
        name="s5_mixer",
    )(u3, h0r, h0i, ar, ai, bre, bim, cre_m, cim_m, d, gw, gb)
    return y, hr, hi


def _s5_discretize(lam_re, lam_im, log_dt, b_re, b_im, c_re, c_im):
    g, p = lam_re.shape
    dt = jnp.exp(log_dt.astype(f32))[:, None]
    lr = lam_re.astype(f32)
    li = lam_im.astype(f32)
    mag = jnp.exp(lr * dt)
    ar = mag * jnp.cos(li * dt)
    ai = mag * jnp.sin(li * dt)
    den = lr * lr + li * li
    cr = ((ar - 1.0) * lr + ai * li) / den
    ci = (ai * lr - (ar - 1.0) * li) / den
    br = b_re.astype(f32)
    bi = b_im.astype(f32)
    bbar_r = cr[..., None] * br - ci[..., None] * bi
    bbar_i = cr[..., None] * bi + ci[..., None] * br
    c = br.shape[-1]
    gb = S5_BLOCK_CHANNELS // c
    assert g % gb == 0
    eye = jnp.eye(gb, dtype=f32)
    nblk = g // gb
    bd_in = lambda m: jnp.einsum('ngpc,gh->ngchp', m.reshape(nblk, gb, p, c), eye).reshape(
        nblk, gb * c, gb * p).astype(bf16)
    bd_out = lambda m: jnp.einsum('ngcp,gh->ngphc', m.reshape(nblk, gb, c, p), eye).reshape(
        nblk, gb * p, gb * c).astype(bf16)
    return (ar.reshape(1, g * p), ai.reshape(1, g * p), bd_in(bbar_r), bd_in(bbar_i),
            bd_out(c_re.astype(f32)), bd_out(-c_im.astype(f32)))


def _rwkv_kernel(p_ref, prev_ref, s0_ref, mu_ref, w0_ref, w1_ref, w2_ref, a0_ref, a1_ref, a2_ref, g1_ref, g2_ref,
                 kk_ref, ka_ref, rk_ref, lnw_ref, lnb_ref, y_ref, sout_ref, prev_s, st_s, *, steps, nsq):
    i = pl.program_id(1)
    L = steps
    L2 = 2 * L
    nb = p_ref.shape[0]
    w = p_ref.shape[-1] // 4
    npair = w // HEAD_PAIR

    row2 = lax.broadcasted_iota(jnp.int32, (L2, HEAD_PAIR), 0)
    lane2 = lax.broadcasted_iota(jnp.int32, (L2, HEAD_PAIR), 1)
    own = (row2 < L) == (lane2 < RWKV_HEAD)
    ri = lax.broadcasted_iota(jnp.int32, (L2, L2), 0)
    ci = lax.broadcasted_iota(jnp.int32, (L2, L2), 1)
    same = (ri < L) == (ci < L)
    strict = same & (ci < ri)
    incl = same & (ci <= ri)
    eye = (ri == ci).astype(f32)
    sr = lax.broadcasted_iota(jnp.int32, (HEAD_PAIR, HEAD_PAIR), 0)
    sc = lax.broadcasted_iota(jnp.int32, (HEAD_PAIR, HEAD_PAIR), 1)
    sdiag = (sr < RWKV_HEAD) == (sc < RWKV_HEAD)

    @pl.when(i == 0)
    def _():
        prev_s[...] = prev_ref[...]
        for bb in range(nb):
            for pr in range(npair):
                sp = s0_ref[bb, pr * HEAD_PAIR:(pr + 1) * HEAD_PAIR, :]
                st_s[bb * npair + pr] = jnp.where(sdiag, jnp.concatenate([sp, sp], axis=1), 0.0)

    trow = lax.broadcasted_iota(jnp.int32, (L, 1), 0)
    ps, deltas = [], []
    for bb in range(nb):
        pb = p_ref[bb]
        shifted = jnp.where(trow == 0, prev_s[bb], pltpu.roll(pb, 1, axis=0))
        prev_s[bb] = pb[L - 1:L, :]
        ps.append(pb)
        deltas.append(shifted - pb)
    p = jnp.concatenate(ps, axis=0) if nb > 1 else ps[0]
    delta = jnp.concatenate(deltas, axis=0) if nb > 1 else deltas[0]
    mu = mu_ref[...]
    pz = p[:, 3 * w:]
    dz = delta[:, 3 * w:]
    r = p[:, :w] + delta[:, :w] * mu[0:1]
    k = p[:, w:2 * w] + delta[:, w:2 * w] * mu[1:2]
    v = p[:, 2 * w:3 * w] + delta[:, 2 * w:3 * w] * mu[2:3]
    zw = pz + dz * mu[3:4]
    za = pz + dz * mu[4:5]
    zg = pz + dz * mu[5:6]
    logw = -DECAY_SCALE * jax.nn.sigmoid(w0_ref[...] + _dot(jnp.tanh(_dot(zw, w1_ref[...])), w2_ref[...]))
    a = jax.nn.sigmoid(a0_ref[...] + _dot(_dot(za, a1_ref[...]), a2_ref[...]))
    g = _dot(jax.nn.sigmoid(_dot(zg, g1_ref[...])), g2_ref[...])
    kkraw = k * kk_ref[...]
    kmod = k * (1.0 + (a - 1.0) * ka_ref[...])
    rk = r * kmod * rk_ref[...]

    tr = lax.broadcasted_iota(jnp.int32, (nb * L, nb * L), 0)
    tc = lax.broadcasted_iota(jnp.int32, (nb * L, nb * L), 1)
    tri = ((tc <= tr) & (tc >= (tr // L) * L)).astype(bf16)
    hi = logw.astype(bf16)
    rem = logw - hi.astype(f32)
    mid = rem.astype(bf16)
    lo = (rem - mid.astype(f32)).astype(bf16)
    cum = (jnp.dot(tri, hi, preferred_element_type=f32) + jnp.dot(tri, mid, preferred_element_type=f32)
           + jnp.dot(tri, lo, preferred_element_type=f32))

    stack = lambda x: jnp.concatenate([x, x], axis=0)
    chains = [(bb, pr) for bb in range(nb) for pr in range(npair)]
    nch = range(len(chains))
    rs = [slice(bb * L, (bb + 1) * L) for bb, _ in chains]
    sls = [slice(pr * HEAD_PAIR, (pr + 1) * HEAD_PAIR) for _, pr in chains]
    cut = lambda x, c: stack(x[rs[c], sls[c]])
    own_of = lambda x, c: jnp.where(own, cut(x, c), 0.0)

    cum_s = [cut(cum, c) for c in nch]
    wt = [jnp.exp(x) for x in cum_s]
    winv = [jnp.exp(-x) for x in cum_s]
    wprev = [jnp.exp(cum_s[c] - cut(logw, c)) for c in nch]
    kkr = [own_of(kkraw, c) for c in nch]
    kkn = [x * lax.rsqrt(jnp.maximum(jnp.sum(x * x, axis=-1, keepdims=True), 1e-24)) for x in kkr]
    rm = [(own_of(r, c) * wt[c]).astype(bf16) for c in nch]
    km = [(own_of(kmod, c) * winv[c]).astype(bf16) for c in nch]
    bm = [(kkn[c] * cut(a, c) * winv[c]).astype(bf16) for c in nch]
    kap = [(kkn[c] * wprev[c]).astype(bf16) for c in nch]
    vmf = [own_of(v, c) for c in nch]
    vm = [x.astype(bf16) for x in vmf]

    gram = [_dg(jnp.concatenate([kap[c], rm[c]], axis=0), jnp.concatenate([bm[c], km[c]], axis=0), _NT)
            for c in nch]
    n_mat = [jnp.where(strict, x[:L2, :L2], 0.0) for x in gram]
    p_mat = [jnp.where(strict, x[:L2, L2:], 0.0) for x in gram]
    g_mat = [jnp.where(incl, x[L2:, :L2], 0.0) for x in gram]
    q_mat = [jnp.where(incl, x[L2:, L2:], 0.0) for x in gram]

    inv = [eye - x for x in n_mat]
    pw = n_mat
    for _ in range(nsq):
        pw = [_dot(x, x) for x in pw]
        inv = [x + _dot(x, y) for x, y in zip(inv, pw)]

    s_old = [st_s[c] for c in nch]
    rhs = [_dg(kap[c], s_old[c], _NT) + _dot(p_mat[c], vm[c]) for c in nch]
    ym0 = [_dg(rm[c], s_old[c], _NT) + _dot(q_mat[c], vm[c]) for c in nch]
    s_mid = [s_old[c] + _dg(vm[c], km[c], _TN) for c in nch]
    um = [-_dot(inv[c], rhs[c]) for c in nch]
    ym = [ym0[c] + _dot(g_mat[c], um[c]) for c in nch]
    for c in nch:
        st_s[c] = (s_mid[c] + _dg(um[c], bm[c], _TN)) * wt[c][L - 1:L, :]

    for c, (bb, pr) in enumerate(chains):
        mean = jnp.sum(ym[c], axis=-1, keepdims=True) * (1.0 / RWKV_HEAD)
        yc = jnp.where(own, ym[c] - mean, 0.0)
        var = jnp.sum(yc * yc, axis=-1, keepdims=True) * (1.0 / RWKV_HEAD)
        yn = yc * lax.rsqrt(var + GN_EPS)
        bonus = jnp.sum(own_of(rk, c), axis=-1, keepdims=True) * vmf[c]
        mix = (yn[:L] + yn[L:]) * lnw_ref[:, sls[c]] + lnb_ref[:, sls[c]] + (bonus[:L] + bonus[L:])
        y_ref[bb, :, sls[c]] = mix * g[rs[c], sls[c]]

    @pl.when(i == pl.num_programs(1) - 1)
    def _():
        for c, (bb, pr) in enumerate(chains):
            s_new = st_s[c]
            sout_ref[bb, pr * HEAD_PAIR:(pr + 1) * HEAD_PAIR, :] = s_new[:, :RWKV_HEAD] + s_new[:, RWKV_HEAD:]


def _rwkv(prk, prev, s0, t, b, mu, w0, w1, w2, a0, a1, a2, g1, g2, k_k, k_a, r_k, ln_w, ln_b):
    w = prk.shape[-1] // 4
    steps = min(WKV_CHUNK, t)
    nb = min(WKV_BATCH, b)
    assert t % steps == 0 and w % HEAD_PAIR == 0 and b % nb == 0
    nsq = max((steps - 1).bit_length() - 1, 0)
    row = lambda x: x.reshape(1, -1).astype(f32)
    wb = lambda x: x.astype(bf16)
    weights = [mu.astype(f32), row(w0), wb(w1), wb(w2), row(a0), wb(a1), wb(a2), wb(g1), wb(g2),
               row(k_k), row(k_a), row(r_k), row(ln_w), row(ln_b)]
    state_spec = pl.BlockSpec((nb, w, RWKV_HEAD), lambda j, i: (j, 0, 0))
    y, s_fin = pl.pallas_call(
        functools.partial(_rwkv_kernel, steps=steps, nsq=nsq),
        grid=(b // nb, t // steps),
        in_specs=[pl.BlockSpec((nb, steps, 4 * w), lambda j, i: (j, i, 0)),
                  pl.BlockSpec((nb, 1, 4 * w), lambda j, i: (j, 0, 0)), state_spec]
                 + [_full(x.shape) for x in weights],
        out_specs=[pl.BlockSpec((nb, steps, w), lambda j, i: (j, i, 0)), state_spec],
        out_shape=[jax.ShapeDtypeStruct((b, t, w), f32), jax.ShapeDtypeStruct((b, w, RWKV_HEAD), f32)],
        scratch_shapes=[pltpu.VMEM((nb, 1, 4 * w), f32),
                        pltpu.VMEM((nb * (w // HEAD_PAIR), HEAD_PAIR, HEAD_PAIR), f32)],
        compiler_params=_params("arbitrary", "arbitrary"),
        name="rwkv7_mixer",
    )(prk.reshape(b, t, 4 * w), prev.reshape(b, 1, 4 * w).astype(f32), s0.reshape(b, w, RWKV_HEAD).astype(f32),
      *weights)
    return y.reshape(b * t, w), s_fin.reshape(b, w // RWKV_HEAD, RWKV_HEAD, RWKV_HEAD)


def _outproj_kernel(x_ref, s5_ref, rw_ref, wo_ref, g_ref, wq_ref, x1_ref, q_ref, *, s5w):
    x1 = x_ref[...] + _dot(s5_ref[...], wo_ref[:s5w, :]) + _dot(rw_ref[...], wo_ref[s5w:, :])
    x1_ref[...] = x1
    q_ref[...] = _dot(_rms(x1, g_ref[...]), wq_ref[...]).astype(bf16)


def _outproj(x, s5y, rwy, w_out, g_ca, w_cq):
    rows, d = x.shape
    s5w = s5y.shape[1]
    rww = rwy.shape[1]
    tm = _tile(rows)
    rt = lambda c: pl.BlockSpec((tm, c), lambda i: (i, 0))
    return pl.pallas_call(
        functools.partial(_outproj_kernel, s5w=s5w),
        grid=(rows // tm,),
        in_specs=[rt(d), rt(s5w), rt(rww), _full(w_out.shape), _full((1, d)), _full(w_cq.shape)],
        out_specs=[rt(d), rt(d)],
        out_shape=[jax.ShapeDtypeStruct((rows, d), f32), jax.ShapeDtypeStruct((rows, d), bf16)],
        compiler_params=_params("parallel"),
        name="outproj_q",
    )(x, s5y, rwy, w_out, g_ca.reshape(1, d), w_cq)


def _memkv_kernel(m_ref, g_ref, w_ref, k_ref, v_ref):
    d = m_ref.shape[-1]
    nb = _rms(m_ref[...], g_ref[...]).astype(bf16)
    k_ref[...] = jnp.dot(nb, w_ref[:, :d], preferred_element_type=f32)
    v_ref[...] = jnp.dot(nb, w_ref[:, d:], preferred_element_type=f32)


def _memkv(mem, g, w):
    rows, d = mem.shape
    tm = _tile(rows)
    rt = pl.BlockSpec((tm, d), lambda i: (i, 0))
    return pl.pallas_call(
        _memkv_kernel,
        grid=(rows // tm,),
        in_specs=[rt, _full((1, d)), _full(w.shape)],
        out_specs=[rt, rt],
        out_shape=[jax.ShapeDtypeStruct((rows, d), f32)] * 2,
        compiler_params=_params("parallel"),
        name="memory_kv",
    )(mem, g.reshape(1, d), w)


def _attn_kernel(q_ref, x1_ref, k_ref, v_ref, wo_ref, x2_ref):
    d = q_ref.shape[-1]
    hd = d // MEM_HEADS
    scale = hd ** -0.5
    q = q_ref[...]
    heads = []
    for h in range(MEM_HEADS):
        sl = slice(h * hd, (h + 1) * hd)
        s = _dg(q[:, sl], k_ref[0, :, sl], _NT) * scale
        e = jnp.exp(s - jnp.max(s, axis=-1, keepdims=True))
        prob = e / jnp.sum(e, axis=-1, keepdims=True)
        heads.append(_dot(prob, v_ref[0, :, sl]))
    o = jnp.concatenate(heads, axis=-1)
    x2_ref[...] = x1_ref[...] + _dot(o, wo_ref[...])


def _attn(q, x1, mk, mv, w_co, t, b):
    d = q.shape[-1]
    n_mem = mk.shape[1]
    tt = min(ROW_TILE, t)
    assert t % tt == 0
    nt = t // tt
    xt = pl.BlockSpec((tt, d), lambda j, i: (j * nt + i, 0))
    kv = pl.BlockSpec((1, n_mem, d), lambda j, i: (j, 0, 0))
    return pl.pallas_call(
        _attn_kernel,
        grid=(b, nt),
        in_specs=[xt, xt, kv, kv, _full(w_co.shape)],
        out_specs=xt,
        out_shape=jax.ShapeDtypeStruct((b * t, d), f32),
        compiler_params=_params("parallel", "parallel"),
        name="mem_attention",
    )(q, x1, mk, mv, w_co)


def _two_group_specs(tm, cols, na_tiles):
    a = pl.BlockSpec((tm, cols), lambda i, *_: (jnp.minimum(i, na_tiles - 1), 0))
    b = pl.BlockSpec((tm, cols), lambda i, *_: (jnp.maximum(i - na_tiles, 0), 0))
    return a, b


def _split2(x):
    hi = x.astype(bf16)
    return hi, (x - hi.astype(f32)).astype(bf16)


def _token_chunk(j, rows, nchunk):
    return pl.ds(j, rows, stride=nchunk)


def _to_token_tiles(ref, x):
    rows, d = x.shape
    nchunk = d // LANES
    for j in range(nchunk):
        ref[_token_chunk(j, rows, nchunk), :] = x[:, j * LANES:(j + 1) * LANES]


def _token_rows(t, nchunk):
    return pl.ds(pl.multiple_of(t * nchunk, nchunk), nchunk)


def _router_kernel(xa_ref, xb_ref, g_ref, rw_ref, rb_ref, ht_ref, e_ref, rank_ref, gate_ref, cnt_ref,
                   *, n_exp, na_tiles):
    i = pl.program_id(0)
    tm, d = xa_ref.shape

    @pl.when(i == 0)
    def _():
        cnt_ref[...] = jnp.zeros_like(cnt_ref)

    h = _rms(jnp.where(i < na_tiles, xa_ref[...], xb_ref[...]), g_ref[...])
    _to_token_tiles(ht_ref, h)

    hh, hl = _split2(h)
    wh, wl = _split2(rw_ref[...])
    logits = (jnp.dot(hh, wh, preferred_element_type=f32) + jnp.dot(hh, wl, preferred_element_type=f32)
              + jnp.dot(hl, wh, preferred_element_type=f32) + rb_ref[...])

    lane = lax.broadcasted_iota(jnp.int32, (tm, n_exp), 1)
    work = logits
    sels, tops = [], []
    for _ in range(TOP_K):
        m = jnp.max(work, axis=-1, keepdims=True)
        idx = jnp.min(jnp.where(work == m, lane, n_exp), axis=-1, keepdims=True)
        sel = lane == idx
        sels.append(sel)
        tops.append(m)
        work = jnp.where(sel, -jnp.inf, work)
    chosen = sels[0] | sels[1] | sels[2] | sels[3]

    onehot = jnp.where(chosen, 1.0, 0.0)
    rr = lax.broadcasted_iota(jnp.int32, (tm, tm), 0)
    cc = lax.broadcasted_iota(jnp.int32, (tm, tm), 1)
    before = (cc < rr).astype(bf16)
    rank = jnp.dot(before, onehot.astype(bf16), preferred_element_type=f32) + cnt_ref[...]
    cnt_ref[...] = cnt_ref[...] + jnp.sum(onehot, axis=0, keepdims=True)

    den = sum(jnp.exp(t - tops[0]) for t in tops)
    e_cols, r_cols, g_cols = [], [], []
    for sel, top in zip(sels, tops):
        e_cols.append(jnp.sum(jnp.where(sel, lane, 0), axis=-1, keepdims=True))
        r_cols.append(jnp.sum(jnp.where(sel, rank, 0.0), axis=-1, keepdims=True))
        g_cols.append(jnp.exp(top - tops[0]) / den)
    e_ref[...] = jnp.concatenate(e_cols, axis=-1)
    rank_ref[...] = jnp.concatenate(r_cols, axis=-1).astype(jnp.int32)
    gate_ref[...] = jnp.concatenate(g_cols, axis=-1)


def _router(xa, xb, g, rw, rb):
    d = xa.shape[1]
    n = xa.shape[0] + xb.shape[0]
    n_exp = rw.shape[1]
    tm = ROW_TILE
    assert xa.shape[0] % tm == 0 and xb.shape[0] % tm == 0
    na_tiles = xa.shape[0] // tm
    sa, sb = _two_group_specs(tm, d, na_tiles)
    rt = lambda c: pl.BlockSpec((tm, c), lambda i: (i, 0))
    return pl.pallas_call(
        functools.partial(_router_kernel, n_exp=n_exp, na_tiles=na_tiles),
        grid=(n // tm,),
        in_specs=[sa, sb, _full((1, d)), _full((d, n_exp)), _full((1, n_exp))],
        out_specs=[pl.BlockSpec((tm * (d // LANES), LANES), lambda i: (i, 0)), rt(TOP_K), rt(TOP_K), rt(TOP_K),
                   _full((1, n_exp))],
        out_shape=[jax.ShapeDtypeStruct((n * (d // LANES), LANES), f32), jax.ShapeDtypeStruct((n, TOP_K), jnp.int32),
                   jax.ShapeDtypeStruct((n, TOP_K), jnp.int32), jax.ShapeDtypeStruct((n, TOP_K), f32),
                   jax.ShapeDtypeStruct((1, n_exp), f32)],
        compiler_params=_params("arbitrary"),
        name="moe_router",
    )(xa, xb, g.reshape(1, d), rw.astype(f32), rb.reshape(1, n_exp).astype(f32))


def _scatter_kernel(dest_ref, zs_ref, ze_ref, ht_ref, xs_ref, zero_s, sem, zsem, *, tm, n):
    i = pl.program_id(0)
    base = i * tm
    nchunk = zero_s.shape[0]

    def for_each_unmapped_row(fn):
        def segment(seg, c):
            return lax.fori_loop(zs_ref[seg], ze_ref[seg], fn, c)
        lax.fori_loop(0, zs_ref.shape[0], segment, 0)

    @pl.when(i == 0)
    def _():
        zero_s[...] = jnp.zeros_like(zero_s)

        def fill(r, c):
            pltpu.make_async_copy(zero_s, xs_ref.at[_token_rows(r, nchunk)], zsem).start()
            return c
        for_each_unmapped_row(fill)

    for kk in range(TOP_K):
        def issue(t, c, kk=kk):
            row = dest_ref[kk * n + base + t]
            pltpu.make_async_copy(ht_ref.at[_token_rows(t, nchunk)], xs_ref.at[_token_rows(row, nchunk)], sem).start()
            return c
        lax.fori_loop(0, tm, issue, 0, unroll=8)
    for kk in range(TOP_K):
        pltpu.make_async_copy(ht_ref, xs_ref.at[pl.ds(0, tm * nchunk)], sem).wait()

    @pl.when(i == 0)
    def _():
        def drain(r, c):
            pltpu.make_async_copy(zero_s, xs_ref.at[_token_rows(r, nchunk)], zsem).wait()
            return c
        for_each_unmapped_row(drain)


def _scatter(dest_km, zero_start, zero_end, ht, n, n_rows):
    nchunk = ht.shape[0] // n
    tm = ROW_TILE
    grid_spec = pltpu.PrefetchScalarGridSpec(
        num_scalar_prefetch=3,
        grid=(n // tm,),
        in_specs=[pl.BlockSpec((tm * nchunk, LANES), lambda i, *_: (i, 0))],
        out_specs=pl.BlockSpec(memory_space=pl.ANY),
        scratch_shapes=[pltpu.VMEM((nchunk, LANES), f32), pltpu.SemaphoreType.DMA(()), pltpu.SemaphoreType.DMA(())],
    )
    return pl.pallas_call(
        functools.partial(_scatter_kernel, tm=tm, n=n),
        grid_spec=grid_spec,
        out_shape=jax.ShapeDtypeStruct((n_rows * nchunk, LANES), f32),
        compiler_params=_params("arbitrary"),
        name="moe_scatter",
    )(dest_km, zero_start, zero_end, ht)


def _expert_kernel(be_ref, nv_ref, xs_ref, wgu_ref, bgu_ref, wd_ref, bd_ref, y_ref, wgu_b, wd_b):
    i = pl.program_id(0)
    nvalid = nv_ref[0]
    dff = wd_ref.shape[2]

    @pl.when((i == 0) | (be_ref[i] != be_ref[jnp.maximum(i - 1, 0)]))
    def _():
        wgu_b[...] = wgu_ref[0, 0].astype(bf16)
        wd_b[...] = wd_ref[0, 0].astype(bf16)

    @pl.when(i < nvalid)
    def _():
        nchunk = wgu_b.shape[0] // LANES
        blk = xs_ref.shape[0] // nchunk
        x = jnp.concatenate([xs_ref[_token_chunk(j, blk, nchunk), :].astype(bf16) for j in range(nchunk)], axis=-1)
        hgu = jnp.dot(x, wgu_b[...], preferred_element_type=f32) + bgu_ref[0, 0]
        hg = jnp.minimum(hgu[:, :dff], SWIGLU_LIMIT)
        hu = jnp.clip(hgu[:, dff:], -SWIGLU_LIMIT, SWIGLU_LIMIT)
        act = (hu + 1.0) * hg * jax.nn.sigmoid(SWIGLU_ALPHA * hg)
        _to_token_tiles(y_ref, _dot(act, wd_b[...]) + bd_ref[0, 0])

    @pl.when(i >= nvalid)
    def _():
        y_ref[...] = jnp.zeros_like(y_ref)


def _experts(xs, block_e, nvalid, w_gu, b_gu, w_down, b_down, l, blk):
    _, n_exp, d, dff2 = w_gu.shape
    dff = dff2 // 2
    nchunk = d // LANES
    n_rows = xs.shape[0] // nchunk
    wspec = lambda shape: pl.BlockSpec((1, 1) + shape, lambda i, be, nv: (l, be[i], 0, 0))
    grid_spec = pltpu.PrefetchScalarGridSpec(
        num_scalar_prefetch=2,
        grid=(n_rows // blk,),
        in_specs=[pl.BlockSpec((blk * nchunk, LANES), lambda i, be, nv: (jnp.minimum(i, nv[0] - 1), 0)),
                  wspec((d, dff2)), wspec((1, dff2)), wspec((dff, d)), wspec((1, d))],
        out_specs=pl.BlockSpec((blk * nchunk, LANES), lambda i, be, nv: (i, 0)),
        scratch_shapes=[pltpu.VMEM((d, dff2), bf16), pltpu.VMEM((dff, d), bf16)],
    )
    depth = w_gu.shape[0]
    return pl.pallas_call(
        _expert_kernel,
        grid_spec=grid_spec,
        out_shape=jax.ShapeDtypeStruct(xs.shape, f32),
        compiler_params=_params("arbitrary"),
        name="moe_experts",
    )(block_e, nvalid, xs, w_gu, b_gu.reshape(depth, n_exp, 1, dff2), w_down, b_down.reshape(depth, n_exp, 1, d))


def _combine_kernel(dest_ref, xa_ref, xb_ref, gate_ref, yb_hbm, g_ref, oa_ref, ob_ref, ybuf, sem,
                    *, tm, n, na_tiles, final):
    i = pl.program_id(0)
    slot = i % 2
    nchunk = ybuf.shape[2] // tm

    def gather(tile, into):
        for kk in range(TOP_K):
            def issue(t, c, kk=kk):
                row = dest_ref[kk * n + tile * tm + t]
                pltpu.make_async_copy(yb_hbm.at[_token_rows(row, nchunk)], ybuf.at[into, kk, _token_rows(t, nchunk)],
                                      sem.at[into]).start()
                return c
            lax.fori_loop(0, tm, issue, 0, unroll=8)

    @pl.when(i == 0)
    def _():
        gather(0, 0)

    @pl.when(i + 1 < pl.num_programs(0))
    def _():
        gather(i + 1, 1 - slot)

    pltpu.make_async_copy(ybuf.at[slot], ybuf.at[slot], sem.at[slot]).wait()
    gate = gate_ref[...]
    chunks = []
    for j in range(nchunk):
        acc = gate[:, 0:1] * ybuf[slot, 0, _token_chunk(j, tm, nchunk), :]
        for kk in range(1, TOP_K):
            acc = acc + gate[:, kk:kk + 1] * ybuf[slot, kk, _token_chunk(j, tm, nchunk), :]
        chunks.append(acc)
    x3 = jnp.where(i < na_tiles, xa_ref[...], xb_ref[...]) + jnp.concatenate(chunks, axis=-1)
    out = _rms(x3, g_ref[...]) if final else x3

    @pl.when(i < na_tiles)
    def _():
        oa_ref[...] = out

    @pl.when(i >= na_tiles)
    def _():
        ob_ref[...] = out


def _combine(dest_km, xa, xb, gate, yb, g_final, final):
    d = xa.shape[1]
    n = xa.shape[0] + xb.shape[0]
    tm = COMBINE_TILE
    assert xa.shape[0] % tm == 0 and xb.shape[0] % tm == 0
    na_tiles = xa.shape[0] // tm
    sa, sb = _two_group_specs(tm, d, na_tiles)
    grid_spec = pltpu.PrefetchScalarGridSpec(
        num_scalar_prefetch=1,
        grid=(n // tm,),
        in_specs=[sa, sb, pl.BlockSpec((tm, TOP_K), lambda i, dst: (i, 0)), pl.BlockSpec(memory_space=pl.ANY),
                  pl.BlockSpec((1, d), lambda i, dst: (0, 0))],
        out_specs=[sa, sb],
        scratch_shapes=[pltpu.VMEM((2, TOP_K, tm * (d // LANES), LANES), f32), pltpu.SemaphoreType.DMA((2,))],
    )
    return pl.pallas_call(
        functools.partial(_combine_kernel, tm=tm, n=n, na_tiles=na_tiles, final=final),
        grid_spec=grid_spec,
        out_shape=[jax.ShapeDtypeStruct(xa.shape, f32), jax.ShapeDtypeStruct(xb.shape, f32)],
        compiler_params=_params("arbitrary"),
        name="moe_combine",
    )(dest_km, xa, xb, gate, yb, g_final.reshape(1, d))


def _moe(xa, xb, g_ffn, router_w, router_b, w_gu, b_gu, w_down, b_down, l, g_out, final):
    n = xa.shape[0] + xb.shape[0]
    n_exp = router_w.shape[1]
    blk = MOE_ROWS
    ht, top_e, top_rank, gate, counts = _router(xa, xb, g_ffn, router_w, router_b)

    counts = counts.reshape(n_exp).astype(jnp.int32)
    padded = (counts + blk - 1) // blk * blk
    pad_end = jnp.cumsum(padded)
    pad_start = pad_end - padded
    dest_km = (pad_start[top_e] + top_rank).T.reshape(-1)
    n_rows = -(-(n * TOP_K) // blk) * blk + n_exp * blk
    block_start = jnp.arange(n_rows // blk, dtype=jnp.int32) * blk
    block_e = jnp.minimum(jnp.sum(pad_end[None, :] <= block_start[:, None], axis=1), n_exp - 1).astype(jnp.int32)
    nvalid = (pad_end[-1:] // blk).astype(jnp.int32)
    zero_start = jnp.concatenate([pad_start + counts, pad_end[-1:]]).astype(jnp.int32)
    zero_end = jnp.concatenate([pad_end, jnp.full((1,), n_rows, jnp.int32)]).astype(jnp.int32)

    xs = _scatter(dest_km, zero_start, zero_end, ht, n, n_rows)
    yb = _experts(xs, block_e, nvalid, w_gu, b_gu, w_down, b_down, l, blk)
    return _combine(dest_km, xa, xb, gate, yb, g_out, final)


def _mixers(x, t, b, h0r, h0i, prev, s0, P, l):
    s5w = P['s5_d'].shape[-1]
    u, prk = _inproj(x, P['norm_mix_g'][l], P['w_in'][l].astype(bf16), s5w)
    ar, ai, bre, bim, cre_m, cim_m = _s5_discretize(P['s5_lambda_re'][l], P['s5_lambda_im'][l], P['s5_log_dt'][l],
                                                    P['s5_b_re'][l], P['s5_b_im'][l], P['s5_c_re'][l], P['s5_c_im'][l])
    s5y, hr, hi = _s5(jnp.transpose(u.reshape(b, t, s5w), (1, 0, 2)), h0r, h0i, ar, ai, bre, bim, cre_m, cim_m,
                      P['s5_d'][l].reshape(1, s5w).astype(f32), P['s5_glu_w'][l].astype(bf16),
                      P['s5_glu_b'][l].reshape(1, s5w).astype(f32))
    s5y = jnp.transpose(s5y, (1, 0, 2)).reshape(b * t, s5w)
    rwy, s_fin = _rwkv(prk, prev, s0, t, b, P['rwkv_mu'][l], P['rwkv_w0'][l], P['rwkv_w1'][l], P['rwkv_w2'][l],
                       P['rwkv_a0'][l], P['rwkv_a1'][l], P['rwkv_a2'][l], P['rwkv_g1'][l], P['rwkv_g2'][l],
                       P['rwkv_k_k'][l], P['rwkv_k_a'][l], P['rwkv_r_k'][l], P['rwkv_ln_w'][l], P['rwkv_ln_b'][l])
    x1, q = _outproj(x, s5y, rwy, P['w_out'][l].astype(bf16), P['norm_ca_g'][l], P['w_cq'][l].astype(bf16))
    shift_last = prk.reshape(b, t, -1)[:, t - 1]
    return x1, q, hr, hi, shift_last, s_fin


def kernel(x_prompt, x_sample, mem_prompt, state_s5_re, state_s5_im, state_rwkv_shift, state_rwkv_wkv, cache_mem_k, cache_mem_v, norm_mix_g, w_in, s5_lambda_re, s5_lambda_im, s5_log_dt, s5_b_re, s5_b_im, s5_c_re, s5_c_im, s5_d, s5_glu_w, s5_glu_b, rwkv_mu, rwkv_w0, rwkv_w1, rwkv_w2, rwkv_a0, rwkv_a1, rwkv_a2, rwkv_g1, rwkv_g2, rwkv_k_k, rwkv_k_a, rwkv_r_k, rwkv_ln_w, rwkv_ln_b, w_out, norm_ca_g, norm_mem_g, w_cq, w_mem_kv, w_co, norm_ffn_g, router_w, router_b, ex_w_gu, ex_b_gu, ex_w_down, ex_b_down, norm_final_g):
    P = dict(norm_mix_g=norm_mix_g, w_in=w_in, s5_lambda_re=s5_lambda_re, s5_lambda_im=s5_lambda_im,
             s5_log_dt=s5_log_dt, s5_b_re=s5_b_re, s5_b_im=s5_b_im, s5_c_re=s5_c_re, s5_c_im=s5_c_im,
             s5_d=s5_d, s5_glu_w=s5_glu_w, s5_glu_b=s5_glu_b, rwkv_mu=rwkv_mu, rwkv_w0=rwkv_w0,
             rwkv_w1=rwkv_w1, rwkv_w2=rwkv_w2, rwkv_a0=rwkv_a0, rwkv_a1=rwkv_a1, rwkv_a2=rwkv_a2,
             rwkv_g1=rwkv_g1, rwkv_g2=rwkv_g2, rwkv_k_k=rwkv_k_k, rwkv_k_a=rwkv_k_a, rwkv_r_k=rwkv_r_k,
             rwkv_ln_w=rwkv_ln_w, rwkv_ln_b=rwkv_ln_b, w_out=w_out, norm_ca_g=norm_ca_g, w_cq=w_cq)
    depth = norm_mix_g.shape[0]
    bp, tp, d = x_prompt.shape
    bs, ts, _ = x_sample.shape
    n_mem = mem_prompt.shape[1]
    s5_groups, s5_state = s5_lambda_re.shape[1:]
    nstate = s5_groups * s5_state
    rw_in = state_rwkv_shift.shape[-1]
    heads = state_rwkv_wkv.shape[2]
    mh = cache_mem_k.shape[3]

    xp = x_prompt.reshape(bp * tp, d)
    xs = x_sample.reshape(bs * ts, d)
    outs = {k: [] for k in ('p_re', 'p_im', 'p_sh', 'p_wkv', 'p_mk', 'p_mv', 's_re', 's_im', 's_sh', 's_wkv')}
    for l in range(depth):
        mk, mv = _memkv(mem_prompt.reshape(bp * n_mem, d), norm_mem_g[l], w_mem_kv[l].astype(bf16))
        mk = mk.reshape(bp, n_mem, d)
        mv = mv.reshape(bp, n_mem, d)
        zeros_state = jnp.zeros((bp, nstate), f32)
        x1p, qp, re_p, im_p, sh_p, wkv_p = _mixers(
            xp, tp, bp, zeros_state, zeros_state, jnp.zeros((bp, rw_in), f32),
            jnp.zeros((bp, heads, RWKV_HEAD, RWKV_HEAD), f32), P, l)
        x1s, qs, re_s, im_s, sh_s, wkv_s = _mixers(
            xs, ts, bs, state_s5_re[l].reshape(bs, nstate), state_s5_im[l].reshape(bs, nstate),
            state_rwkv_shift[l], state_rwkv_wkv[l], P, l)
        wco = w_co[l].astype(bf16)
        x2p = _attn(qp, x1p, mk, mv, wco, tp, bp)
        x2s = _attn(qs, x1s, cache_mem_k[l].reshape(bs, n_mem, d), cache_mem_v[l].reshape(bs, n_mem, d), wco, ts, bs)
        last = l == depth - 1
        xp, xs = _moe(x2p, x2s, norm_ffn_g[l], router_w[l], router_b[l], ex_w_gu, ex_b_gu, ex_w_down, ex_b_down,
                      l, norm_final_g, last)
        outs['p_re'].append(re_p.reshape(bp, s5_groups, s5_state))
        outs['p_im'].append(im_p.reshape(bp, s5_groups, s5_state))
        outs['p_sh'].append(sh_p)
        outs['p_wkv'].append(wkv_p)
        outs['p_mk'].append(mk.reshape(bp, n_mem, mh, d // mh))
        outs['p_mv'].append(mv.reshape(bp, n_mem, mh, d // mh))
        outs['s_re'].append(re_s.reshape(bs, s5_groups, s5_state))
        outs['s_im'].append(im_s.reshape(bs, s5_groups, s5_state))
        outs['s_sh'].append(sh_s)
        outs['s_wkv'].append(wkv_s)
    st = lambda k: jnp.stack(outs[k])
    return (xp.reshape(bp, tp, d), xs.reshape(bs, ts, d), st('p_re'), st('p_im'), st('p_sh'), st('p_wkv'),
            st('p_mk'), st('p_mv'), st('s_re'), st('s_im'), st('s_sh'), st('s_wkv'))
```

```python
import functools

import jax
import jax.numpy as jnp
from jax import lax
from jax.experimental import pallas as pl
from jax.experimental.pallas import tpu as pltpu

f32 = jnp.float32
bf16 = jnp.bfloat16

RWKV_HEAD = 64
HEAD_PAIR = 2 * RWKV_HEAD
DECAY_SCALE = 0.6065306597
MEM_HEADS = 4
TOP_K = 4
SWIGLU_LIMIT = 7.0
SWIGLU_ALPHA = 1.702
RMS_EPS = 1e-5
GN_EPS = 64e-5

V7X_VMEM_LIMIT = 56 * 1024 * 1024
ROW_TILE = 512
MOE_ROWS = 256
EXPERT_ROW_GROUPS = 2
COMBINE_TILE = 256
WKV_CHUNK = 64
WKV_BATCH = 2
S5_CHUNK = 64
S5_BLOCK_CHANNELS = 128
LANES = 128
DMA_PRIORITIES = 2

_NT = (((1,), (1,)), ((), ()))
_TN = (((0,), (0,)), ((), ()))


def _params(*sem):
    return pltpu.CompilerParams(dimension_semantics=sem, vmem_limit_bytes=V7X_VMEM_LIMIT)


def _dot(a, b):
    return jnp.dot(a.astype(bf16), b.astype(bf16), preferred_element_type=f32)


def _dg(a, b, dims):
    return lax.dot_general(a.astype(bf16), b.astype(bf16), dims, preferred_element_type=f32)


def _rms(x, g):
    return x * lax.rsqrt(jnp.mean(x * x, axis=-1, keepdims=True) + RMS_EPS) * g


def _full(shape):
    n = len(shape)
    return pl.BlockSpec(shape, lambda *_: (0,) * n)


def _tile(rows, tile=ROW_TILE):
    t = min(tile, rows)
    assert rows % t == 0
    return t


def _inproj_kernel(x_ref, g_ref, w_ref, u_ref, p_ref, *, s5w):
    nb = _rms(x_ref[...], g_ref[...]).astype(bf16)
    u_ref[...] = jnp.dot(nb, w_ref[:, :s5w], preferred_element_type=f32)
    p_ref[...] = jnp.dot(nb, w_ref[:, s5w:], preferred_element_type=f32)


def _inproj(x, g, w, s5w):
    rows, d = x.shape
    inw = w.shape[1]
    tm = _tile(rows)
    return pl.pallas_call(
        functools.partial(_inproj_kernel, s5w=s5w),
        grid=(rows // tm,),
        in_specs=[pl.BlockSpec((tm, d), lambda i: (i, 0)), _full((1, d)), _full((d, inw))],
        out_specs=[pl.BlockSpec((tm, s5w), lambda i: (i, 0)), pl.BlockSpec((tm, inw - s5w), lambda i: (i, 0))],
        out_shape=[jax.ShapeDtypeStruct((rows, s5w), f32), jax.ShapeDtypeStruct((rows, inw - s5w), f32)],
        compiler_params=_params("parallel"),
        name="inproj",
    )(x, g.reshape(1, d), w)


def _s5_kernel(u_ref, h0r_ref, h0i_ref, ar_ref, ai_ref, bre_ref, bim_ref, cre_ref, cim_ref, d_ref, gw_ref, gb_ref,
               y_ref, hr_ref, hi_ref, xr_s, xi_s, *, steps, bt, lane_chunk):
    i = pl.program_id(1)
    width = u_ref.shape[-1]
    nstate = ar_ref.shape[-1]

    @pl.when(i == 0)
    def _():
        hr_ref[...] = h0r_ref[...]
        hi_ref[...] = h0i_ref[...]

    u = u_ref[...].reshape(steps * bt, width)
    ub = u.astype(bf16)
    nblk, cb, sb = bre_ref.shape
    for j in range(nblk):
        uj = ub[:, j * cb:(j + 1) * cb]
        xr_s[:, j * sb:(j + 1) * sb] = jnp.dot(uj, bre_ref[j], preferred_element_type=f32)
        xi_s[:, j * sb:(j + 1) * sb] = jnp.dot(uj, bim_ref[j], preferred_element_type=f32)

    for c in range(nstate // lane_chunk):
        sl = slice(c * lane_chunk, (c + 1) * lane_chunk)
        arb = jnp.broadcast_to(ar_ref[:, sl], (bt, lane_chunk))
        aib = jnp.broadcast_to(ai_ref[:, sl], (bt, lane_chunk))

        def step(t, carry, sl=sl, arb=arb, aib=aib):
            hr, hi = carry
            r0 = pl.multiple_of(t * bt, bt)
            nhr = arb * hr - aib * hi + xr_s[pl.ds(r0, bt), sl]
            nhi = arb * hi + aib * hr + xi_s[pl.ds(r0, bt), sl]
            xr_s[pl.ds(r0, bt), sl] = nhr
            xi_s[pl.ds(r0, bt), sl] = nhi
            return nhr, nhi

        hr, hi = lax.fori_loop(0, steps, step, (hr_ref[:, sl], hi_ref[:, sl]), unroll=2)
        hr_ref[:, sl] = hr
        hi_ref[:, sl] = hi

    y = jnp.concatenate(
        [jnp.dot(xr_s[:, j * sb:(j + 1) * sb].astype(bf16), cre_ref[j], preferred_element_type=f32)
         + jnp.dot(xi_s[:, j * sb:(j + 1) * sb].astype(bf16), cim_ref[j], preferred_element_type=f32)
         for j in range(nblk)], axis=-1) + d_ref[...] * u
    z = jax.nn.gelu(y)
    out = z * jax.nn.sigmoid(_dot(z, gw_ref[...]) + gb_ref[...])
    y_ref[...] = out.reshape(steps, bt, width)


def _s5(u3, h0r, h0i, ar, ai, bre, bim, cre_m, cim_m, d, gw, gb):
    t, b, width = u3.shape
    nstate = ar.shape[-1]
    bt = 8
    steps = min(S5_CHUNK, t)
    assert b % bt == 0 and t % steps == 0
    kern = functools.partial(_s5_kernel, steps=steps, bt=bt, lane_chunk=512)
    st_spec = pl.BlockSpec((bt, nstate), lambda j, i: (j, 0))
    y, hr, hi = pl.pallas_call(
        kern,
        grid=(b // bt, t // steps),
        in_specs=[pl.BlockSpec((steps, bt, width), lambda j, i: (i, j, 0)), st_spec, st_spec,
                  _full((1, nstate)), _full((1, nstate)), _full(bre.shape), _full(bim.shape),
                  _full(cre_m.shape), _full(cim_m.shape), _full((1, width)), _full((width, width)),
                  _full((1, width))],
        out_specs=[pl.BlockSpec((steps, bt, width), lambda j, i: (i, j, 0)), st_spec, st_spec],
        out_shape=[jax.ShapeDtypeStruct((t, b, width), f32), jax.ShapeDtypeStruct((b, nstate), f32),
                   jax.ShapeDtypeStruct((b, nstate), f32)],
        scratch_shapes=[pltpu.VMEM((steps * bt, nstate), f32), pltpu.VMEM((steps * bt, nstate), f32)],
        compiler_params=_params("arbitrary", "arbitrary"),
        name="s5_mixer",
    )(u3, h0r, h0i, ar, ai, bre, bim, cre_m, cim_m, d, gw, gb)
    return y, hr, hi


def _s5_discretize(lam_re, lam_im, log_dt, b_re, b_im, c_re, c_im):
    g, p = lam_re.shape
    dt = jnp.exp(log_dt.astype(f32))[:, None]
    lr = lam_re.astype(f32)
    li = lam_im.astype(f32)
    mag = jnp.exp(lr * dt)
    ar = mag * jnp.cos(li * dt)
    ai = mag * jnp.sin(li * dt)
    den = lr * lr + li * li
    cr = ((ar - 1.0) * lr + ai * li) / den
    ci = (ai * lr - (ar - 1.0) * li) / den
    br = b_re.astype(f32)
    bi = b_im.astype(f32)
    bbar_r = cr[..., None] * br - ci[..., None] * bi
    bbar_i = cr[..., None] * bi + ci[..., None] * br
    c = br.shape[-1]
    gb = S5_BLOCK_CHANNELS // c
    assert g % gb == 0
    eye = jnp.eye(gb, dtype=f32)
    nblk = g // gb
    bd_in = lambda m: jnp.einsum('ngpc,gh->ngchp', m.reshape(nblk, gb, p, c), eye).reshape(
        nblk, gb * c, gb * p).astype(bf16)
    bd_out = lambda m: jnp.einsum('ngcp,gh->ngphc', m.reshape(nblk, gb, c, p), eye).reshape(
        nblk, gb * p, gb * c).astype(bf16)
    return (ar.reshape(1, g * p), ai.reshape(1, g * p), bd_in(bbar_r), bd_in(bbar_i),
            bd_out(c_re.astype(f32)), bd_out(-c_im.astype(f32)))


def _rwkv_kernel(p_ref, prev_ref, s0_ref, mu_ref, w0_ref, w1_ref, w2_ref, a0_ref, a1_ref, a2_ref, g1_ref, g2_ref,
                 kk_ref, ka_ref, rk_ref, lnw_ref, lnb_ref, y_ref, sout_ref, prev_s, st_s, *, steps, nsq):
    i = pl.program_id(1)
    L = steps
    L2 = 2 * L
    nb = p_ref.shape[0]
    w = p_ref.shape[-1] // 4
    npair = w // HEAD_PAIR

    row2 = lax.broadcasted_iota(jnp.int32, (L2, HEAD_PAIR), 0)
    lane2 = lax.broadcasted_iota(jnp.int32, (L2, HEAD_PAIR), 1)
    own = (row2 < L) == (lane2 < RWKV_HEAD)
    ri = lax.broadcasted_iota(jnp.int32, (L2, L2), 0)
    ci = lax.broadcasted_iota(jnp.int32, (L2, L2), 1)
    same = (ri < L) == (ci < L)
    strict = same & (ci < ri)
    incl = same & (ci <= ri)
    eye = (ri == ci).astype(f32)
    sr = lax.broadcasted_iota(jnp.int32, (HEAD_PAIR, HEAD_PAIR), 0)
    sc = lax.broadcasted_iota(jnp.int32, (HEAD_PAIR, HEAD_PAIR), 1)
    sdiag = (sr < RWKV_HEAD) == (sc < RWKV_HEAD)

    @pl.when(i == 0)
    def _():
        prev_s[...] = prev_ref[...]
        for bb in range(nb):
            for pr in range(npair):
                sp = s0_ref[bb, pr * HEAD_PAIR:(pr + 1) * HEAD_PAIR, :]
                st_s[bb * npair + pr] = jnp.where(sdiag, jnp.concatenate([sp, sp], axis=1), 0.0)

    trow = lax.broadcasted_iota(jnp.int32, (L, 1), 0)
    ps, deltas = [], []
    for bb in range(nb):
        pb = p_ref[bb]
        shifted = jnp.where(trow == 0, prev_s[bb], pltpu.roll(pb, 1, axis=0))
        prev_s[bb] = pb[L - 1:L, :]
        ps.append(pb)
        deltas.append(shifted - pb)
    p = jnp.concatenate(ps, axis=0) if nb > 1 else ps[0]
    delta = jnp.concatenate(deltas, axis=0) if nb > 1 else deltas[0]
    mu = mu_ref[...]
    pz = p[:, 3 * w:]
    dz = delta[:, 3 * w:]
    r = p[:, :w] + delta[:, :w] * mu[0:1]
    k = p[:, w:2 * w] + delta[:, w:2 * w] * mu[1:2]
    v = p[:, 2 * w:3 * w] + delta[:, 2 * w:3 * w] * mu[2:3]
    zw = pz + dz * mu[3:4]
    za = pz + dz * mu[4:5]
    zg = pz + dz * mu[5:6]
    logw = -DECAY_SCALE * jax.nn.sigmoid(w0_ref[...] + _dot(jnp.tanh(_dot(zw, w1_ref[...])), w2_ref[...]))
    a = jax.nn.sigmoid(a0_ref[...] + _dot(_dot(za, a1_ref[...]), a2_ref[...]))
    g = _dot(jax.nn.sigmoid(_dot(zg, g1_ref[...])), g2_ref[...])
    kkraw = k * kk_ref[...]
    kmod = k * (1.0 + (a - 1.0) * ka_ref[...])
    rk = r * kmod * rk_ref[...]

    tr = lax.broadcasted_iota(jnp.int32, (nb * L, nb * L), 0)
    tc = lax.broadcasted_iota(jnp.int32, (nb * L, nb * L), 1)
    tri = ((tc <= tr) & (tc >= (tr // L) * L)).astype(bf16)
    hi = logw.astype(bf16)
    rem = logw - hi.astype(f32)
    mid = rem.astype(bf16)
    lo = (rem - mid.astype(f32)).astype(bf16)
    cum = (jnp.dot(tri, hi, preferred_element_type=f32) + jnp.dot(tri, mid, preferred_element_type=f32)
           + jnp.dot(tri, lo, preferred_element_type=f32))

    stack = lambda x: jnp.concatenate([x, x], axis=0)
    chains = [(bb, pr) for bb in range(nb) for pr in range(npair)]
    nch = range(len(chains))
    rs = [slice(bb * L, (bb + 1) * L) for bb, _ in chains]
    sls = [slice(pr * HEAD_PAIR, (pr + 1) * HEAD_PAIR) for _, pr in chains]
    cut = lambda x, c: stack(x[rs[c], sls[c]])
    own_of = lambda x, c: jnp.where(own, cut(x, c), 0.0)

    wt, rm, km, bm, kap, vmf, vm, gram = [], [], [], [], [], [], [], []
    for c in nch:
        cum_c = cut(cum, c)
        wt_c = jnp.exp(cum_c)
        winv = jnp.exp(-cum_c)
        wprev = jnp.exp(cum_c - cut(logw, c))
        kkr = own_of(kkraw, c)
        kkn = kkr * lax.rsqrt(jnp.maximum(jnp.sum(kkr * kkr, axis=-1, keepdims=True), 1e-24))
        wt.append(wt_c)
        rm.append((own_of(r, c) * wt_c).astype(bf16))
        km.append((own_of(kmod, c) * winv).astype(bf16))
        bm.append((kkn * cut(a, c) * winv).astype(bf16))
        kap.append((kkn * wprev).astype(bf16))
        vmf.append(own_of(v, c))
        vm.append(vmf[c].astype(bf16))
        gram.append(_dg(jnp.concatenate([kap[c], rm[c]], axis=0), jnp.concatenate([bm[c], km[c]], axis=0), _NT))
    n_mat = [jnp.where(strict, x[:L2, :L2], 0.0) for x in gram]
    p_mat = [jnp.where(strict, x[:L2, L2:], 0.0) for x in gram]
    g_mat = [jnp.where(incl, x[L2:, :L2], 0.0) for x in gram]
    q_mat = [jnp.where(incl, x[L2:, L2:], 0.0) for x in gram]

    neg = [-x for x in n_mat]
    inv = [eye + x for x in neg]
    if nsq > 0:
        pw = [_dot(x, x) for x in neg]
        for _ in range(nsq - 1):
            both = [_dot(pw[c], jnp.concatenate([inv[c], pw[c]], axis=1)) for c in nch]
            inv = [inv[c] + both[c][:, :L2] for c in nch]
            pw = [x[:, L2:] for x in both]
        inv = [inv[c] + _dot(pw[c], inv[c]) for c in nch]

    s_old = [st_s[c] for c in nch]
    rhs = [_dg(kap[c], s_old[c], _NT) + _dot(p_mat[c], vm[c]) for c in nch]
    ym0 = [_dg(rm[c], s_old[c], _NT) for c in nch]
    um = [(-_dot(inv[c], rhs[c])).astype(bf16) for c in nch]
    vu = [jnp.concatenate([vm[c], um[c]], axis=0) for c in nch]
    ym = [ym0[c] + _dot(jnp.concatenate([q_mat[c], g_mat[c]], axis=1), vu[c]) for c in nch]
    for c in nch:
        st_s[c] = (s_old[c] + _dg(vu[c], jnp.concatenate([km[c], bm[c]], axis=0), _TN)) * wt[c][L - 1:L, :]

    for c, (bb, pr) in enumerate(chains):
        mean = jnp.sum(ym[c], axis=-1, keepdims=True) * (1.0 / RWKV_HEAD)
        yc = jnp.where(own, ym[c] - mean, 0.0)
        var = jnp.sum(yc * yc, axis=-1, keepdims=True) * (1.0 / RWKV_HEAD)
        yn = yc * lax.rsqrt(var + GN_EPS)
        bonus = jnp.sum(own_of(rk, c), axis=-1, keepdims=True) * vmf[c]
        mix = (yn[:L] + yn[L:]) * lnw_ref[:, sls[c]] + lnb_ref[:, sls[c]] + (bonus[:L] + bonus[L:])
        y_ref[bb, :, sls[c]] = mix * g[rs[c], sls[c]]

    @pl.when(i == pl.num_programs(1) - 1)
    def _():
        for c, (bb, pr) in enumerate(chains):
            s_new = st_s[c]
            sout_ref[bb, pr * HEAD_PAIR:(pr + 1) * HEAD_PAIR, :] = s_new[:, :RWKV_HEAD] + s_new[:, RWKV_HEAD:]


def _rwkv(prk, prev, s0, t, b, mu, w0, w1, w2, a0, a1, a2, g1, g2, k_k, k_a, r_k, ln_w, ln_b):
    w = prk.shape[-1] // 4
    steps = min(WKV_CHUNK, t)
    nb = min(WKV_BATCH * max(1, WKV_CHUNK // (4 * steps)), b)
    assert t % steps == 0 and w % HEAD_PAIR == 0 and b % nb == 0
    nsq = max((steps - 1).bit_length() - 1, 0)
    row = lambda x: x.reshape(1, -1).astype(f32)
    wb = lambda x: x.astype(bf16)
    weights = [mu.astype(f32), row(w0), wb(w1), wb(w2), row(a0), wb(a1), wb(a2), wb(g1), wb(g2),
               row(k_k), row(k_a), row(r_k), row(ln_w), row(ln_b)]
    state_spec = pl.BlockSpec((nb, w, RWKV_HEAD), lambda j, i: (j, 0, 0))
    y, s_fin = pl.pallas_call(
        functools.partial(_rwkv_kernel, steps=steps, nsq=nsq),
        grid=(b // nb, t // steps),
        in_specs=[pl.BlockSpec((nb, steps, 4 * w), lambda j, i: (j, i, 0)),
                  pl.BlockSpec((nb, 1, 4 * w), lambda j, i: (j, 0, 0)), state_spec]
                 + [_full(x.shape) for x in weights],
        out_specs=[pl.BlockSpec((nb, steps, w), lambda j, i: (j, i, 0)), state_spec],
        out_shape=[jax.ShapeDtypeStruct((b, t, w), f32), jax.ShapeDtypeStruct((b, w, RWKV_HEAD), f32)],
        scratch_shapes=[pltpu.VMEM((nb, 1, 4 * w), f32),
                        pltpu.VMEM((nb * (w // HEAD_PAIR), HEAD_PAIR, HEAD_PAIR), f32)],
        compiler_params=_params("arbitrary", "arbitrary"),
        name="rwkv7_mixer",
    )(prk.reshape(b, t, 4 * w), prev.reshape(b, 1, 4 * w).astype(f32), s0.reshape(b, w, RWKV_HEAD).astype(f32),
      *weights)
    return y.reshape(b * t, w), s_fin.reshape(b, w // RWKV_HEAD, RWKV_HEAD, RWKV_HEAD)


def _outproj_kernel(x_ref, s5_ref, rw_ref, wo_ref, g_ref, wq_ref, x1_ref, q_ref, *, s5w):
    x1 = x_ref[...] + _dot(s5_ref[...], wo_ref[:s5w, :]) + _dot(rw_ref[...], wo_ref[s5w:, :])
    x1_ref[...] = x1
    q_ref[...] = _dot(_rms(x1, g_ref[...]), wq_ref[...]).astype(bf16)


def _outproj(x, s5y, rwy, w_out, g_ca, w_cq):
    rows, d = x.shape
    s5w = s5y.shape[1]
    rww = rwy.shape[1]
    tm = _tile(rows)
    rt = lambda c: pl.BlockSpec((tm, c), lambda i: (i, 0))
    return pl.pallas_call(
        functools.partial(_outproj_kernel, s5w=s5w),
        grid=(rows // tm,),
        in_specs=[rt(d), rt(s5w), rt(rww), _full(w_out.shape), _full((1, d)), _full(w_cq.shape)],
        out_specs=[rt(d), rt(d)],
        out_shape=[jax.ShapeDtypeStruct((rows, d), f32), jax.ShapeDtypeStruct((rows, d), bf16)],
        compiler_params=_params("parallel"),
        name="outproj_q",
    )(x, s5y, rwy, w_out, g_ca.reshape(1, d), w_cq)


def _kv_to_head_tiles(x):
    b, m, h, hd = x.shape
    return x.reshape(b, m, h, hd // LANES, LANES).transpose(0, 1, 3, 2, 4).reshape(b * m * (hd // LANES) * h, LANES)


def _kv_from_head_tiles(x, b, m, h, hd):
    return x.reshape(b, m, hd // LANES, h, LANES).transpose(0, 1, 3, 2, 4).reshape(b, m, h, hd)


def _memkv_kernel(m_ref, g_ref, w_ref, k_ref, v_ref):
    tm, d = m_ref.shape
    hd = d // MEM_HEADS
    parts = hd // LANES
    per_token = parts * MEM_HEADS
    nb = _rms(m_ref[...], g_ref[...]).astype(bf16)
    for ref, cols in ((k_ref, slice(0, d)), (v_ref, slice(d, 2 * d))):
        kv = jnp.dot(nb, w_ref[:, cols], preferred_element_type=f32)
        for h in range(MEM_HEADS):
            for c in range(parts):
                lo = h * hd + c * LANES
                ref[pl.ds(c * MEM_HEADS + h, tm, stride=per_token), :] = kv[:, lo:lo + LANES]


def _memkv(mem, g, w):
    rows, d = mem.shape
    tm = _tile(rows)
    per_token = d // LANES
    out = pl.BlockSpec((tm * per_token, LANES), lambda i: (i, 0))
    return pl.pallas_call(
        _memkv_kernel,
        grid=(rows // tm,),
        in_specs=[pl.BlockSpec((tm, d), lambda i: (i, 0)), _full((1, d)), _full(w.shape)],
        out_specs=[out, out],
        out_shape=[jax.ShapeDtypeStruct((rows * per_token, LANES), f32)] * 2,
        compiler_params=_params("parallel"),
        name="memory_kv",
    )(mem, g.reshape(1, d), w)


def _attn_kernel(q_ref, x1_ref, k_ref, v_ref, wo_ref, x2_ref):
    d = q_ref.shape[-1]
    hd = d // MEM_HEADS
    parts = hd // LANES
    per_token = parts * MEM_HEADS
    n_mem = k_ref.shape[0] // per_token
    scale = hd ** -0.5
    q = q_ref[...]

    def head(ref, h):
        return jnp.concatenate([ref[pl.ds(c * MEM_HEADS + h, n_mem, stride=per_token), :].astype(bf16)
                                for c in range(parts)], axis=-1)

    heads = []
    for h in range(MEM_HEADS):
        s = _dg(q[:, h * hd:(h + 1) * hd], head(k_ref, h), _NT) * scale
        e = jnp.exp(s - jnp.max(s, axis=-1, keepdims=True))
        prob = e / jnp.sum(e, axis=-1, keepdims=True)
        heads.append(_dot(prob, head(v_ref, h)))
    o = jnp.concatenate(heads, axis=-1)
    x2_ref[...] = x1_ref[...] + _dot(o, wo_ref[...])


def _attn(q, x1, mk, mv, w_co, t, b):
    d = q.shape[-1]
    rows_per_b = mk.shape[0] // b
    tt = min(ROW_TILE, t)
    assert t % tt == 0
    nt = t // tt
    xt = pl.BlockSpec((tt, d), lambda j, i: (j * nt + i, 0))
    kv = pl.BlockSpec((rows_per_b, LANES), lambda j, i: (j, 0))
    return pl.pallas_call(
        _attn_kernel,
        grid=(b, nt),
        in_specs=[xt, xt, kv, kv, _full(w_co.shape)],
        out_specs=xt,
        out_shape=jax.ShapeDtypeStruct((b * t, d), f32),
        compiler_params=_params("parallel", "parallel"),
        name="mem_attention",
    )(q, x1, mk, mv, w_co)


def _two_group_specs(tm, cols, na_tiles):
    a = pl.BlockSpec((tm, cols), lambda i, *_: (jnp.minimum(i, na_tiles - 1), 0))
    b = pl.BlockSpec((tm, cols), lambda i, *_: (jnp.maximum(i - na_tiles, 0), 0))
    return a, b


def _split2(x):
    hi = x.astype(bf16)
    return hi, (x - hi.astype(f32)).astype(bf16)


def _token_chunk(j, rows, nchunk, first=0):
    return pl.ds(first * nchunk + j, rows, stride=nchunk)


def _to_token_tiles(ref, x, first=0):
    rows, d = x.shape
    nchunk = d // LANES
    for j in range(nchunk):
        ref[_token_chunk(j, rows, nchunk, first), :] = x[:, j * LANES:(j + 1) * LANES]


def _token_rows(t, nchunk):
    return pl.ds(pl.multiple_of(t * nchunk, nchunk), nchunk)


def _router_kernel(xa_ref, xb_ref, g_ref, rw_ref, rb_ref, ht_ref, e_ref, rank_ref, gate_ref, cnt_ref,
                   *, n_exp, na_tiles):
    i = pl.program_id(0)
    tm, d = xa_ref.shape

    @pl.when(i == 0)
    def _():
        cnt_ref[...] = jnp.zeros_like(cnt_ref)

    h = _rms(jnp.where(i < na_tiles, xa_ref[...], xb_ref[...]), g_ref[...])
    _to_token_tiles(ht_ref, h)

    hh, hl = _split2(h)
    wh, wl = _split2(rw_ref[...])
    logits = (jnp.dot(hh, wh, preferred_element_type=f32) + jnp.dot(hh, wl, preferred_element_type=f32)
              + jnp.dot(hl, wh, preferred_element_type=f32) + rb_ref[...])

    lane = lax.broadcasted_iota(jnp.int32, (tm, n_exp), 1)
    work = logits
    sels, tops = [], []
    for _ in range(TOP_K):
        m = jnp.max(work, axis=-1, keepdims=True)
        idx = jnp.min(jnp.where(work == m, lane, n_exp), axis=-1, keepdims=True)
        sel = lane == idx
        sels.append(sel)
        tops.append(m)
        work = jnp.where(sel, -jnp.inf, work)
    chosen = sels[0] | sels[1] | sels[2] | sels[3]

    onehot = jnp.where(chosen, 1.0, 0.0)
    rr = lax.broadcasted_iota(jnp.int32, (tm, tm), 0)
    cc = lax.broadcasted_iota(jnp.int32, (tm, tm), 1)
    before = (cc < rr).astype(bf16)
    rank = jnp.dot(before, onehot.astype(bf16), preferred_element_type=f32) + cnt_ref[...]
    cnt_ref[...] = cnt_ref[...] + jnp.sum(onehot, axis=0, keepdims=True)

    den = sum(jnp.exp(t - tops[0]) for t in tops)
    e_cols, r_cols, g_cols = [], [], []
    for sel, top in zip(sels, tops):
        e_cols.append(jnp.sum(jnp.where(sel, lane, 0), axis=-1, keepdims=True))
        r_cols.append(jnp.sum(jnp.where(sel, rank, 0.0), axis=-1, keepdims=True))
        g_cols.append(jnp.exp(top - tops[0]) / den)
    e_ref[...] = jnp.concatenate(e_cols, axis=-1)
    rank_ref[...] = jnp.concatenate(r_cols, axis=-1).astype(jnp.int32)
    gate_ref[...] = jnp.concatenate(g_cols, axis=-1)


def _router(xa, xb, g, rw, rb):
    d = xa.shape[1]
    n = xa.shape[0] + xb.shape[0]
    n_exp = rw.shape[1]
    tm = ROW_TILE
    assert xa.shape[0] % tm == 0 and xb.shape[0] % tm == 0
    na_tiles = xa.shape[0] // tm
    sa, sb = _two_group_specs(tm, d, na_tiles)
    rt = lambda c: pl.BlockSpec((tm, c), lambda i: (i, 0))
    return pl.pallas_call(
        functools.partial(_router_kernel, n_exp=n_exp, na_tiles=na_tiles),
        grid=(n // tm,),
        in_specs=[sa, sb, _full((1, d)), _full((d, n_exp)), _full((1, n_exp))],
        out_specs=[pl.BlockSpec((tm * (d // LANES), LANES), lambda i: (i, 0)), rt(TOP_K), rt(TOP_K), rt(TOP_K),
                   _full((1, n_exp))],
        out_shape=[jax.ShapeDtypeStruct((n * (d // LANES), LANES), f32), jax.ShapeDtypeStruct((n, TOP_K), jnp.int32),
                   jax.ShapeDtypeStruct((n, TOP_K), jnp.int32), jax.ShapeDtypeStruct((n, TOP_K), f32),
                   jax.ShapeDtypeStruct((1, n_exp), f32)],
        compiler_params=_params("arbitrary"),
        name="moe_router",
    )(xa, xb, g.reshape(1, d), rw.astype(f32), rb.reshape(1, n_exp).astype(f32))


def _scatter_kernel(dest_ref, zs_ref, ze_ref, ht_ref, xs_ref, zero_s, sem, zsem, *, tm, n):
    i = pl.program_id(0)
    base = i * tm
    nchunk = zero_s.shape[0]

    def for_each_unmapped_row(fn):
        def segment(seg, c):
            return lax.fori_loop(zs_ref[seg], ze_ref[seg], fn, c)
        lax.fori_loop(0, zs_ref.shape[0], segment, 0)

    @pl.when(i == 0)
    def _():
        zero_s[...] = jnp.zeros_like(zero_s)

        def fill(r, c):
            pltpu.make_async_copy(zero_s, xs_ref.at[_token_rows(r, nchunk)], zsem).start()
            return c
        for_each_unmapped_row(fill)

    for kk in range(TOP_K):
        def issue(pair, c, kk=kk):
            for prio in range(DMA_PRIORITIES):
                t = pair * DMA_PRIORITIES + prio
                row = dest_ref[kk * n + base + t]
                pltpu.make_async_copy(ht_ref.at[_token_rows(t, nchunk)], xs_ref.at[_token_rows(row, nchunk)],
                                      sem).start(priority=prio)
            return c
        lax.fori_loop(0, tm // DMA_PRIORITIES, issue, 0, unroll=4)
    for kk in range(TOP_K):
        pltpu.make_async_copy(ht_ref, xs_ref.at[pl.ds(0, tm * nchunk)], sem).wait()

    @pl.when(i == 0)
    def _():
        def drain(r, c):
            pltpu.make_async_copy(zero_s, xs_ref.at[_token_rows(r, nchunk)], zsem).wait()
            return c
        for_each_unmapped_row(drain)


def _scatter(dest_km, zero_start, zero_end, ht, n, n_rows):
    nchunk = ht.shape[0] // n
    tm = ROW_TILE
    grid_spec = pltpu.PrefetchScalarGridSpec(
        num_scalar_prefetch=3,
        grid=(n // tm,),
        in_specs=[pl.BlockSpec((tm * nchunk, LANES), lambda i, *_: (i, 0))],
        out_specs=pl.BlockSpec(memory_space=pl.ANY),
        scratch_shapes=[pltpu.VMEM((nchunk, LANES), f32), pltpu.SemaphoreType.DMA(()), pltpu.SemaphoreType.DMA(())],
    )
    return pl.pallas_call(
        functools.partial(_scatter_kernel, tm=tm, n=n),
        grid_spec=grid_spec,
        out_shape=jax.ShapeDtypeStruct((n_rows * nchunk, LANES), f32),
        compiler_params=_params("arbitrary"),
        name="moe_scatter",
    )(dest_km, zero_start, zero_end, ht)


def _expert_kernel(be_ref, nv_ref, xs_ref, wgu_ref, bgu_ref, wd_ref, bd_ref, y_ref, wgu_b, wd_b):
    i = pl.program_id(0)
    nvalid = nv_ref[0]
    dff = wd_ref.shape[2]

    @pl.when((i == 0) | (be_ref[i] != be_ref[jnp.maximum(i - 1, 0)]))
    def _():
        wgu_b[...] = wgu_ref[0, 0].astype(bf16)
        wd_b[...] = wd_ref[0, 0].astype(bf16)

    @pl.when(i < nvalid)
    def _():
        nchunk = wgu_b.shape[0] // LANES
        blk = xs_ref.shape[0] // nchunk
        ng = EXPERT_ROW_GROUPS
        rows = blk // ng
        x = [jnp.concatenate([xs_ref[_token_chunk(j, rows, nchunk, s * rows), :].astype(bf16)
                              for j in range(nchunk)], axis=-1) for s in range(ng)]
        hgu = [jnp.dot(v, wgu_b[...], preferred_element_type=f32) + bgu_ref[0, 0] for v in x]
        hg = [jnp.minimum(v[:, :dff], SWIGLU_LIMIT) for v in hgu]
        hu = [jnp.clip(v[:, dff:], -SWIGLU_LIMIT, SWIGLU_LIMIT) for v in hgu]
        act = [((u + 1.0) * g * jax.nn.sigmoid(SWIGLU_ALPHA * g)).astype(bf16) for u, g in zip(hu, hg)]
        y = [jnp.dot(v, wd_b[...], preferred_element_type=f32) + bd_ref[0, 0] for v in act]
        for s in range(ng):
            _to_token_tiles(y_ref, y[s], s * rows)

    @pl.when(i >= nvalid)
    def _():
        y_ref[...] = jnp.zeros_like(y_ref)


def _experts(xs, block_e, nvalid, w_gu, b_gu, w_down, b_down, l, blk):
    _, n_exp, d, dff2 = w_gu.shape
    dff = dff2 // 2
    nchunk = d // LANES
    n_rows = xs.shape[0] // nchunk
    wspec = lambda shape: pl.BlockSpec((1, 1) + shape, lambda i, be, nv: (l, be[i], 0, 0))
    grid_spec = pltpu.PrefetchScalarGridSpec(
        num_scalar_prefetch=2,
        grid=(n_rows // blk,),
        in_specs=[pl.BlockSpec((blk * nchunk, LANES), lambda i, be, nv: (jnp.minimum(i, nv[0] - 1), 0)),
                  wspec((d, dff2)), wspec((1, dff2)), wspec((dff, d)), wspec((1, d))],
        out_specs=pl.BlockSpec((blk * nchunk, LANES), lambda i, be, nv: (i, 0)),
        scratch_shapes=[pltpu.VMEM((d, dff2), bf16), pltpu.VMEM((dff, d), bf16)],
    )
    depth = w_gu.shape[0]
    return pl.pallas_call(
        _expert_kernel,
        grid_spec=grid_spec,
        out_shape=jax.ShapeDtypeStruct(xs.shape, f32),
        compiler_params=_params("arbitrary"),
        name="moe_experts",
    )(block_e, nvalid, xs, w_gu, b_gu.reshape(depth, n_exp, 1, dff2), w_down, b_down.reshape(depth, n_exp, 1, d))


def _combine_kernel(dest_ref, xa_ref, xb_ref, gate_ref, yb_hbm, g_ref, oa_ref, ob_ref, ybuf, sem,
                    *, tm, n, na_tiles, final):
    i = pl.program_id(0)
    slot = i % 2
    nchunk = ybuf.shape[2] // tm

    def gather(tile, into):
        for kk in range(TOP_K):
            def issue(pair, c, kk=kk):
                for prio in range(DMA_PRIORITIES):
                    t = pair * DMA_PRIORITIES + prio
                    row = dest_ref[kk * n + tile * tm + t]
                    pltpu.make_async_copy(yb_hbm.at[_token_rows(row, nchunk)],
                                          ybuf.at[into, kk, _token_rows(t, nchunk)], sem.at[into]).start(priority=prio)
                return c
            lax.fori_loop(0, tm // DMA_PRIORITIES, issue, 0, unroll=4)

    @pl.when(i == 0)
    def _():
        gather(0, 0)

    @pl.when(i + 1 < pl.num_programs(0))
    def _():
        gather(i + 1, 1 - slot)

    pltpu.make_async_copy(ybuf.at[slot], ybuf.at[slot], sem.at[slot]).wait()
    gate = gate_ref[...]
    chunks = []
    for j in range(nchunk):
        acc = gate[:, 0:1] * ybuf[slot, 0, _token_chunk(j, tm, nchunk), :]
        for kk in range(1, TOP_K):
            acc = acc + gate[:, kk:kk + 1] * ybuf[slot, kk, _token_chunk(j, tm, nchunk), :]
        chunks.append(acc)
    x3 = jnp.where(i < na_tiles, xa_ref[...], xb_ref[...]) + jnp.concatenate(chunks, axis=-1)
    out = _rms(x3, g_ref[...]) if final else x3

    @pl.when(i < na_tiles)
    def _():
        oa_ref[...] = out

    @pl.when(i >= na_tiles)
    def _():
        ob_ref[...] = out


def _combine(dest_km, xa, xb, gate, yb, g_final, final):
    d = xa.shape[1]
    n = xa.shape[0] + xb.shape[0]
    tm = COMBINE_TILE
    assert xa.shape[0] % tm == 0 and xb.shape[0] % tm == 0
    na_tiles = xa.shape[0] // tm
    sa, sb = _two_group_specs(tm, d, na_tiles)
    grid_spec = pltpu.PrefetchScalarGridSpec(
        num_scalar_prefetch=1,
        grid=(n // tm,),
        in_specs=[sa, sb, pl.BlockSpec((tm, TOP_K), lambda i, dst: (i, 0)), pl.BlockSpec(memory_space=pl.ANY),
                  pl.BlockSpec((1, d), lambda i, dst: (0, 0))],
        out_specs=[sa, sb],
        scratch_shapes=[pltpu.VMEM((2, TOP_K, tm * (d // LANES), LANES), f32), pltpu.SemaphoreType.DMA((2,))],
    )
    return pl.pallas_call(
        functools.partial(_combine_kernel, tm=tm, n=n, na_tiles=na_tiles, final=final),
        grid_spec=grid_spec,
        out_shape=[jax.ShapeDtypeStruct(xa.shape, f32), jax.ShapeDtypeStruct(xb.shape, f32)],
        compiler_params=_params("arbitrary"),
        name="moe_combine",
    )(dest_km, xa, xb, gate, yb, g_final.reshape(1, d))


def _moe(xa, xb, g_ffn, router_w, router_b, w_gu, b_gu, w_down, b_down, l, g_out, final):
    n = xa.shape[0] + xb.shape[0]
    n_exp = router_w.shape[1]
    blk = MOE_ROWS
    ht, top_e, top_rank, gate, counts = _router(xa, xb, g_ffn, router_w, router_b)

    counts = counts.reshape(n_exp).astype(jnp.int32)
    padded = (counts + blk - 1) // blk * blk
    pad_end = jnp.cumsum(padded)
    pad_start = pad_end - padded
    dest_km = (pad_start[top_e] + top_rank).T.reshape(-1)
    n_rows = -(-(n * TOP_K) // blk) * blk + n_exp * blk
    block_start = jnp.arange(n_rows // blk, dtype=jnp.int32) * blk
    block_e = jnp.minimum(jnp.sum(pad_end[None, :] <= block_start[:, None], axis=1), n_exp - 1).astype(jnp.int32)
    nvalid = (pad_end[-1:] // blk).astype(jnp.int32)
    zero_start = jnp.concatenate([pad_start + counts, pad_end[-1:]]).astype(jnp.int32)
    zero_end = jnp.concatenate([pad_end, jnp.full((1,), n_rows, jnp.int32)]).astype(jnp.int32)

    xs = _scatter(dest_km, zero_start, zero_end, ht, n, n_rows)
    yb = _experts(xs, block_e, nvalid, w_gu, b_gu, w_down, b_down, l, blk)
    return _combine(dest_km, xa, xb, gate, yb, g_out, final)


def _mixers(x, t, b, h0r, h0i, prev, s0, P, l):
    s5w = P['s5_d'].shape[-1]
    u, prk = _inproj(x, P['norm_mix_g'][l], P['w_in'][l].astype(bf16), s5w)
    ar, ai, bre, bim, cre_m, cim_m = _s5_discretize(P['s5_lambda_re'][l], P['s5_lambda_im'][l], P['s5_log_dt'][l],
                                                    P['s5_b_re'][l], P['s5_b_im'][l], P['s5_c_re'][l], P['s5_c_im'][l])
    s5y, hr, hi = _s5(jnp.transpose(u.reshape(b, t, s5w), (1, 0, 2)), h0r, h0i, ar, ai, bre, bim, cre_m, cim_m,
                      P['s5_d'][l].reshape(1, s5w).astype(f32), P['s5_glu_w'][l].astype(bf16),
                      P['s5_glu_b'][l].reshape(1, s5w).astype(f32))
    s5y = jnp.transpose(s5y, (1, 0, 2)).reshape(b * t, s5w)
    rwy, s_fin = _rwkv(prk, prev, s0, t, b, P['rwkv_mu'][l], P['rwkv_w0'][l], P['rwkv_w1'][l], P['rwkv_w2'][l],
                       P['rwkv_a0'][l], P['rwkv_a1'][l], P['rwkv_a2'][l], P['rwkv_g1'][l], P['rwkv_g2'][l],
                       P['rwkv_k_k'][l], P['rwkv_k_a'][l], P['rwkv_r_k'][l], P['rwkv_ln_w'][l], P['rwkv_ln_b'][l])
    x1, q = _outproj(x, s5y, rwy, P['w_out'][l].astype(bf16), P['norm_ca_g'][l], P['w_cq'][l].astype(bf16))
    shift_last = prk.reshape(b, t, -1)[:, t - 1]
    return x1, q, hr, hi, shift_last, s_fin


def kernel(x_prompt, x_sample, mem_prompt, state_s5_re, state_s5_im, state_rwkv_shift, state_rwkv_wkv, cache_mem_k, cache_mem_v, norm_mix_g, w_in, s5_lambda_re, s5_lambda_im, s5_log_dt, s5_b_re, s5_b_im, s5_c_re, s5_c_im, s5_d, s5_glu_w, s5_glu_b, rwkv_mu, rwkv_w0, rwkv_w1, rwkv_w2, rwkv_a0, rwkv_a1, rwkv_a2, rwkv_g1, rwkv_g2, rwkv_k_k, rwkv_k_a, rwkv_r_k, rwkv_ln_w, rwkv_ln_b, w_out, norm_ca_g, norm_mem_g, w_cq, w_mem_kv, w_co, norm_ffn_g, router_w, router_b, ex_w_gu, ex_b_gu, ex_w_down, ex_b_down, norm_final_g):
    P = dict(norm_mix_g=norm_mix_g, w_in=w_in, s5_lambda_re=s5_lambda_re, s5_lambda_im=s5_lambda_im,
             s5_log_dt=s5_log_dt, s5_b_re=s5_b_re, s5_b_im=s5_b_im, s5_c_re=s5_c_re, s5_c_im=s5_c_im,
             s5_d=s5_d, s5_glu_w=s5_glu_w, s5_glu_b=s5_glu_b, rwkv_mu=rwkv_mu, rwkv_w0=rwkv_w0,
             rwkv_w1=rwkv_w1, rwkv_w2=rwkv_w2, rwkv_a0=rwkv_a0, rwkv_a1=rwkv_a1, rwkv_a2=rwkv_a2,
             rwkv_g1=rwkv_g1, rwkv_g2=rwkv_g2, rwkv_k_k=rwkv_k_k, rwkv_k_a=rwkv_k_a, rwkv_r_k=rwkv_r_k,
             rwkv_ln_w=rwkv_ln_w, rwkv_ln_b=rwkv_ln_b, w_out=w_out, norm_ca_g=norm_ca_g, w_cq=w_cq)
    depth = norm_mix_g.shape[0]
    bp, tp, d = x_prompt.shape
    bs, ts, _ = x_sample.shape
    n_mem = mem_prompt.shape[1]
    s5_groups, s5_state = s5_lambda_re.shape[1:]
    nstate = s5_groups * s5_state
    rw_in = state_rwkv_shift.shape[-1]
    heads = state_rwkv_wkv.shape[2]
    mh = cache_mem_k.shape[3]
    assert mh == MEM_HEADS

    xp = x_prompt.reshape(bp * tp, d)
    xs = x_sample.reshape(bs * ts, d)
    outs = {k: [] for k in ('p_re', 'p_im', 'p_sh', 'p_wkv', 'p_mk', 'p_mv', 's_re', 's_im', 's_sh', 's_wkv')}
    for l in range(depth):
        mk, mv = _memkv(mem_prompt.reshape(bp * n_mem, d), norm_mem_g[l], w_mem_kv[l].astype(bf16))
        zeros_state = jnp.zeros((bp, nstate), f32)
        x1p, qp, re_p, im_p, sh_p, wkv_p = _mixers(
            xp, tp, bp, zeros_state, zeros_state, jnp.zeros((bp, rw_in), f32),
            jnp.zeros((bp, heads, RWKV_HEAD, RWKV_HEAD), f32), P, l)
        x1s, qs, re_s, im_s, sh_s, wkv_s = _mixers(
            xs, ts, bs, state_s5_re[l].reshape(bs, nstate), state_s5_im[l].reshape(bs, nstate),
            state_rwkv_shift[l], state_rwkv_wkv[l], P, l)
        wco = w_co[l].astype(bf16)
        x2p = _attn(qp, x1p, mk, mv, wco, tp, bp)
        x2s = _attn(qs, x1s, _kv_to_head_tiles(cache_mem_k[l]), _kv_to_head_tiles(cache_mem_v[l]), wco, ts, bs)
        last = l == depth - 1
        xp, xs = _moe(x2p, x2s, norm_ffn_g[l], router_w[l], router_b[l], ex_w_gu, ex_b_gu, ex_w_down, ex_b_down,
                      l, norm_final_g, last)
        outs['p_re'].append(re_p.reshape(bp, s5_groups, s5_state))
        outs['p_im'].append(im_p.reshape(bp, s5_groups, s5_state))
        outs['p_sh'].append(sh_p)
        outs['p_wkv'].append(wkv_p)
        outs['p_mk'].append(_kv_from_head_tiles(mk, bp, n_mem, mh, d // mh))
        outs['p_mv'].append(_kv_from_head_tiles(mv, bp, n_mem, mh, d // mh))
        outs['s_re'].append(re_s.reshape(bs, s5_groups, s5_state))
        outs['s_im'].append(im_s.reshape(bs, s5_groups, s5_state))
        outs['s_sh'].append(sh_s)
        outs['s_wkv'].append(wkv_s)
    st = lambda k: jnp.stack(outs[k])
    return (xp.reshape(bp, tp, d), xs.reshape(bs, ts, d), st('p_re'), st('p_im'), st('p_sh'), st('p_wkv'),
            st('p_mk'), st('p_mv'), st('s_re'), st('s_im'), st('s_sh'), st('s_wkv'))
```

```python
import functools

import jax
import jax.numpy as jnp
from jax import lax
from jax.experimental import pallas as pl
from jax.experimental.pallas import tpu as pltpu

f32 = jnp.float32
bf16 = jnp.bfloat16

RWKV_HEAD = 64
HEAD_PAIR = 2 * RWKV_HEAD
DECAY_SCALE = 0.6065306597
MEM_HEADS = 4
TOP_K = 4
SWIGLU_LIMIT = 7.0
SWIGLU_ALPHA = 1.702
RMS_EPS = 1e-5
GN_EPS = 64e-5

V7X_VMEM_LIMIT = 56 * 1024 * 1024
ROW_TILE = 512
MOE_ROWS = 256
COMBINE_TILE = 256
WKV_CHUNK = 64
WKV_BATCH = 2
S5_CHUNK = 64
S5_BLOCK_CHANNELS = 128
LANES = 128
DMA_PRIORITIES = 2

_NT = (((1,), (1,)), ((), ()))
_TN = (((0,), (0,)), ((), ()))


def _params(*sem):
    return pltpu.CompilerParams(dimension_semantics=sem, vmem_limit_bytes=V7X_VMEM_LIMIT)


def _dot(a, b):
    return jnp.dot(a.astype(bf16), b.astype(bf16), preferred_element_type=f32)


def _dg(a, b, dims):
    return lax.dot_general(a.astype(bf16), b.astype(bf16), dims, preferred_element_type=f32)


def _rms(x, g):
    return x * lax.rsqrt(jnp.mean(x * x, axis=-1, keepdims=True) + RMS_EPS) * g


def _full(shape):
    n = len(shape)
    return pl.BlockSpec(shape, lambda *_: (0,) * n)


def _tile(rows, tile=ROW_TILE):
    t = min(tile, rows)
    assert rows % t == 0
    return t


def _inproj_kernel(x_ref, g_ref, w_ref, u_ref, p_ref, *, s5w):
    nb = _rms(x_ref[...], g_ref[...]).astype(bf16)
    u_ref[...] = jnp.dot(nb, w_ref[:, :s5w], preferred_element_type=f32)
    p_ref[...] = jnp.dot(nb, w_ref[:, s5w:], preferred_element_type=f32)


def _inproj(x, g, w, s5w):
    rows, d = x.shape
    inw = w.shape[1]
    tm = _tile(rows)
    return pl.pallas_call(
        functools.partial(_inproj_kernel, s5w=s5w),
        grid=(rows // tm,),
        in_specs=[pl.BlockSpec((tm, d), lambda i: (i, 0)), _full((1, d)), _full((d, inw))],
        out_specs=[pl.BlockSpec((tm, s5w), lambda i: (i, 0)), pl.BlockSpec((tm, inw - s5w), lambda i: (i, 0))],
        out_shape=[jax.ShapeDtypeStruct((rows, s5w), f32), jax.ShapeDtypeStruct((rows, inw - s5w), f32)],
        compiler_params=_params("parallel"),
        name="inproj",
    )(x, g.reshape(1, d), w)


def _s5_kernel(u_ref, h0r_ref, h0i_ref, ar_ref, ai_ref, bre_ref, bim_ref, cre_ref, cim_ref, d_ref, gw_ref, gb_ref,
               y_ref, hr_ref, hi_ref, xr_s, xi_s, *, steps, bt, lane_chunk):
    i = pl.program_id(1)
    width = u_ref.shape[-1]
    nstate = ar_ref.shape[-1]

    @pl.when(i == 0)
    def _():
        hr_ref[...] = h0r_ref[...]
        hi_ref[...] = h0i_ref[...]

    u = u_ref[...].reshape(steps * bt, width)
    ub = u.astype(bf16)
    nblk, cb, sb = bre_ref.shape
    for j in range(nblk):
        uj = ub[:, j * cb:(j + 1) * cb]
        xr_s[:, j * sb:(j + 1) * sb] = jnp.dot(uj, bre_ref[j], preferred_element_type=f32)
        xi_s[:, j * sb:(j + 1) * sb] = jnp.dot(uj, bim_ref[j], preferred_element_type=f32)

    for c in range(nstate // lane_chunk):
        sl = slice(c * lane_chunk, (c + 1) * lane_chunk)
        arb = jnp.broadcast_to(ar_ref[:, sl], (bt, lane_chunk))
        aib = jnp.broadcast_to(ai_ref[:, sl], (bt, lane_chunk))

        def step(t, carry, sl=sl, arb=arb, aib=aib):
            hr, hi = carry
            r0 = pl.multiple_of(t * bt, bt)
            nhr = arb * hr - aib * hi + xr_s[pl.ds(r0, bt), sl]
            nhi = arb * hi + aib * hr + xi_s[pl.ds(r0, bt), sl]
            xr_s[pl.ds(r0, bt), sl] = nhr
            xi_s[pl.ds(r0, bt), sl] = nhi
            return nhr, nhi

        hr, hi = lax.fori_loop(0, steps, step, (hr_ref[:, sl], hi_ref[:, sl]), unroll=2)
        hr_ref[:, sl] = hr
        hi_ref[:, sl] = hi

    y = jnp.concatenate(
        [jnp.dot(xr_s[:, j * sb:(j + 1) * sb].astype(bf16), cre_ref[j], preferred_element_type=f32)
         + jnp.dot(xi_s[:, j * sb:(j + 1) * sb].astype(bf16), cim_ref[j], preferred_element_type=f32)
         for j in range(nblk)], axis=-1) + d_ref[...] * u
    z = jax.nn.gelu(y)
    out = z * jax.nn.sigmoid(_dot(z, gw_ref[...]) + gb_ref[...])
    y_ref[...] = out.reshape(steps, bt, width)


def _s5(u3, h0r, h0i, ar, ai, bre, bim, cre_m, cim_m, d, gw, gb):
    t, b, width = u3.shape
    nstate = ar.shape[-1]
    bt = 8
    steps = min(S5_CHUNK, t)
    assert b % bt == 0 and t % steps == 0
    kern = functools.partial(_s5_kernel, steps=steps, bt=bt, lane_chunk=512)
    st_spec = pl.BlockSpec((bt, nstate), lambda j, i: (j, 0))
    y, hr, hi = pl.pallas_call(
        kern,
        grid=(b // bt, t // steps),
        in_specs=[pl.BlockSpec((steps, bt, width), lambda j, i: (i, j, 0)), st_spec, st_spec,
                  _full((1, nstate)), _full((1, nstate)), _full(bre.shape), _full(bim.shape),
                  _full(cre_m.shape), _full(cim_m.shape), _full((1, width)), _full((width, width)),
                  _full((1, width))],
        out_specs=[pl.BlockSpec((steps, bt, width), lambda j, i: (i, j, 0)), st_spec, st_spec],
        out_shape=[jax.ShapeDtypeStruct((t, b, width), f32), jax.ShapeDtypeStruct((b, nstate), f32),
                   jax.ShapeDtypeStruct((b, nstate), f32)],
        scratch_shapes=[pltpu.VMEM((steps * bt, nstate), f32), pltpu.VMEM((steps * bt, nstate), f32)],
        compiler_params=_params("arbitrary", "arbitrary"),
        name="s5_mixer",
    )(u3, h0r, h0i, ar, ai, bre, bim, cre_m, cim_m, d, gw, gb)
    return y, hr, hi


def _s5_discretize(lam_re, lam_im, log_dt, b_re, b_im, c_re, c_im):
    g, p = lam_re.shape
    dt = jnp.exp(log_dt.astype(f32))[:, None]
    lr = lam_re.astype(f32)
    li = lam_im.astype(f32)
    mag = jnp.exp(lr * dt)
    ar = mag * jnp.cos(li * dt)
    ai = mag * jnp.sin(li * dt)
    den = lr * lr + li * li
    cr = ((ar - 1.0) * lr + ai * li) / den
    ci = (ai * lr - (ar - 1.0) * li) / den
    br = b_re.astype(f32)
    bi = b_im.astype(f32)
    bbar_r = cr[..., None] * br - ci[..., None] * bi
    bbar_i = cr[..., None] * bi + ci[..., None] * br
    c = br.shape[-1]
    gb = S5_BLOCK_CHANNELS // c
    assert g % gb == 0
    eye = jnp.eye(gb, dtype=f32)
    nblk = g // gb
    bd_in = lambda m: jnp.einsum('ngpc,gh->ngchp', m.reshape(nblk, gb, p, c), eye).reshape(
        nblk, gb * c, gb * p).astype(bf16)
    bd_out = lambda m: jnp.einsum('ngcp,gh->ngphc', m.reshape(nblk, gb, c, p), eye).reshape(
        nblk, gb * p, gb * c).astype(bf16)
    return (ar.reshape(1, g * p), ai.reshape(1, g * p), bd_in(bbar_r), bd_in(bbar_i),
            bd_out(c_re.astype(f32)), bd_out(-c_im.astype(f32)))


def _rwkv_kernel(p_ref, prev_ref, s0_ref, mu_ref, w0_ref, w1_ref, w2_ref, a0_ref, a1_ref, a2_ref, g1_ref, g2_ref,
                 kk_ref, ka_ref, rk_ref, lnw_ref, lnb_ref, y_ref, sout_ref, prev_s, st_s, *, steps, nsq):
    i = pl.program_id(1)
    L = steps
    L2 = 2 * L
    nb = p_ref.shape[0]
    w = p_ref.shape[-1] // 4
    npair = w // HEAD_PAIR

    row2 = lax.broadcasted_iota(jnp.int32, (L2, HEAD_PAIR), 0)
    lane2 = lax.broadcasted_iota(jnp.int32, (L2, HEAD_PAIR), 1)
    own = (row2 < L) == (lane2 < RWKV_HEAD)
    ri = lax.broadcasted_iota(jnp.int32, (L2, L2), 0)
    ci = lax.broadcasted_iota(jnp.int32, (L2, L2), 1)
    same = (ri < L) == (ci < L)
    strict = same & (ci < ri)
    incl = same & (ci <= ri)
    eye = (ri == ci).astype(f32)
    sr = lax.broadcasted_iota(jnp.int32, (HEAD_PAIR, HEAD_PAIR), 0)
    sc = lax.broadcasted_iota(jnp.int32, (HEAD_PAIR, HEAD_PAIR), 1)
    sdiag = (sr < RWKV_HEAD) == (sc < RWKV_HEAD)

    @pl.when(i == 0)
    def _():
        prev_s[...] = prev_ref[...]
        for bb in range(nb):
            for pr in range(npair):
                sp = s0_ref[bb, pr * HEAD_PAIR:(pr + 1) * HEAD_PAIR, :]
                st_s[bb * npair + pr] = jnp.where(sdiag, jnp.concatenate([sp, sp], axis=1), 0.0)

    trow = lax.broadcasted_iota(jnp.int32, (L, 1), 0)
    ps, deltas = [], []
    for bb in range(nb):
        pb = p_ref[bb]
        shifted = jnp.where(trow == 0, prev_s[bb], pltpu.roll(pb, 1, axis=0))
        prev_s[bb] = pb[L - 1:L, :]
        ps.append(pb)
        deltas.append(shifted - pb)
    p = jnp.concatenate(ps, axis=0) if nb > 1 else ps[0]
    delta = jnp.concatenate(deltas, axis=0) if nb > 1 else deltas[0]
    mu = mu_ref[...]
    pz = p[:, 3 * w:]
    dz = delta[:, 3 * w:]
    r = p[:, :w] + delta[:, :w] * mu[0:1]
    k = p[:, w:2 * w] + delta[:, w:2 * w] * mu[1:2]
    v = p[:, 2 * w:3 * w] + delta[:, 2 * w:3 * w] * mu[2:3]
    zw = pz + dz * mu[3:4]
    za = pz + dz * mu[4:5]
    zg = pz + dz * mu[5:6]
    logw = -DECAY_SCALE * jax.nn.sigmoid(w0_ref[...] + _dot(jnp.tanh(_dot(zw, w1_ref[...])), w2_ref[...]))
    a = jax.nn.sigmoid(a0_ref[...] + _dot(_dot(za, a1_ref[...]), a2_ref[...]))
    g = _dot(jax.nn.sigmoid(_dot(zg, g1_ref[...])), g2_ref[...])
    kkraw = k * kk_ref[...]
    kmod = k * (1.0 + (a - 1.0) * ka_ref[...])
    rk = r * kmod * rk_ref[...]

    tr = lax.broadcasted_iota(jnp.int32, (nb * L, nb * L), 0)
    tc = lax.broadcasted_iota(jnp.int32, (nb * L, nb * L), 1)
    tri = ((tc <= tr) & (tc >= (tr // L) * L)).astype(bf16)
    hi = logw.astype(bf16)
    rem = logw - hi.astype(f32)
    mid = rem.astype(bf16)
    lo = (rem - mid.astype(f32)).astype(bf16)
    cum = (jnp.dot(tri, hi, preferred_element_type=f32) + jnp.dot(tri, mid, preferred_element_type=f32)
           + jnp.dot(tri, lo, preferred_element_type=f32))

    stack = lambda x: jnp.concatenate([x, x], axis=0)
    chains = [(bb, pr) for bb in range(nb) for pr in range(npair)]
    nch = range(len(chains))
    rs = [slice(bb * L, (bb + 1) * L) for bb, _ in chains]
    sls = [slice(pr * HEAD_PAIR, (pr + 1) * HEAD_PAIR) for _, pr in chains]
    cut = lambda x, c: stack(x[rs[c], sls[c]])
    own_of = lambda x, c: jnp.where(own, cut(x, c), 0.0)

    wt, rm, km, bm, kap, vmf, vm, gram = [], [], [], [], [], [], [], []
    for c in nch:
        cum_c = cut(cum, c)
        wt_c = jnp.exp(cum_c)
        winv = jnp.exp(-cum_c)
        wprev = jnp.exp(cum_c - cut(logw, c))
        kkr = own_of(kkraw, c)
        kkn = kkr * lax.rsqrt(jnp.maximum(jnp.sum(kkr * kkr, axis=-1, keepdims=True), 1e-24))
        wt.append(wt_c)
        rm.append((own_of(r, c) * wt_c).astype(bf16))
        km.append((own_of(kmod, c) * winv).astype(bf16))
        bm.append((kkn * cut(a, c) * winv).astype(bf16))
        kap.append((kkn * wprev).astype(bf16))
        vmf.append(own_of(v, c))
        vm.append(vmf[c].astype(bf16))
        gram.append(_dg(jnp.concatenate([kap[c], rm[c]], axis=0), jnp.concatenate([bm[c], km[c]], axis=0), _NT))
    n_mat = [jnp.where(strict, x[:L2, :L2], 0.0) for x in gram]
    p_mat = [jnp.where(strict, x[:L2, L2:], 0.0) for x in gram]
    g_mat = [jnp.where(incl, x[L2:, :L2], 0.0) for x in gram]
    q_mat = [jnp.where(incl, x[L2:, L2:], 0.0) for x in gram]

    neg = [-x for x in n_mat]
    inv = [eye + x for x in neg]
    if nsq > 0:
        pw = [_dot(x, x) for x in neg]
        for _ in range(nsq - 1):
            both = [_dot(pw[c], jnp.concatenate([inv[c], pw[c]], axis=1)) for c in nch]
            inv = [inv[c] + both[c][:, :L2] for c in nch]
            pw = [x[:, L2:] for x in both]
        inv = [inv[c] + _dot(pw[c], inv[c]) for c in nch]

    s_old = [st_s[c] for c in nch]
    rhs = [_dg(kap[c], s_old[c], _NT) + _dot(p_mat[c], vm[c]) for c in nch]
    ym0 = [_dg(rm[c], s_old[c], _NT) for c in nch]
    um = [(-_dot(inv[c], rhs[c])).astype(bf16) for c in nch]
    vu = [jnp.concatenate([vm[c], um[c]], axis=0) for c in nch]
    ym = [ym0[c] + _dot(jnp.concatenate([q_mat[c], g_mat[c]], axis=1), vu[c]) for c in nch]
    for c in nch:
        st_s[c] = (s_old[c] + _dg(vu[c], jnp.concatenate([km[c], bm[c]], axis=0), _TN)) * wt[c][L - 1:L, :]

    for c, (bb, pr) in enumerate(chains):
        mean = jnp.sum(ym[c], axis=-1, keepdims=True) * (1.0 / RWKV_HEAD)
        yc = jnp.where(own, ym[c] - mean, 0.0)
        var = jnp.sum(yc * yc, axis=-1, keepdims=True) * (1.0 / RWKV_HEAD)
        yn = yc * lax.rsqrt(var + GN_EPS)
        bonus = jnp.sum(own_of(rk, c), axis=-1, keepdims=True) * vmf[c]
        mix = (yn[:L] + yn[L:]) * lnw_ref[:, sls[c]] + lnb_ref[:, sls[c]] + (bonus[:L] + bonus[L:])
        y_ref[bb, :, sls[c]] = mix * g[rs[c], sls[c]]

    @pl.when(i == pl.num_programs(1) - 1)
    def _():
        for c, (bb, pr) in enumerate(chains):
            s_new = st_s[c]
            sout_ref[bb, pr * HEAD_PAIR:(pr + 1) * HEAD_PAIR, :] = s_new[:, :RWKV_HEAD] + s_new[:, RWKV_HEAD:]


def _rwkv(prk, prev, s0, t, b, mu, w0, w1, w2, a0, a1, a2, g1, g2, k_k, k_a, r_k, ln_w, ln_b):
    w = prk.shape[-1] // 4
    steps = min(WKV_CHUNK, t)
    nb = min(WKV_BATCH * max(1, WKV_CHUNK // (4 * steps)), b)
    assert t % steps == 0 and w % HEAD_PAIR == 0 and b % nb == 0
    nsq = max((steps - 1).bit_length() - 1, 0)
    row = lambda x: x.reshape(1, -1).astype(f32)
    wb = lambda x: x.astype(bf16)
    weights = [mu.astype(f32), row(w0), wb(w1), wb(w2), row(a0), wb(a1), wb(a2), wb(g1), wb(g2),
               row(k_k), row(k_a), row(r_k), row(ln_w), row(ln_b)]
    state_spec = pl.BlockSpec((nb, w, RWKV_HEAD), lambda j, i: (j, 0, 0))
    y, s_fin = pl.pallas_call(
        functools.partial(_rwkv_kernel, steps=steps, nsq=nsq),
        grid=(b // nb, t // steps),
        in_specs=[pl.BlockSpec((nb, steps, 4 * w), lambda j, i: (j, i, 0)),
                  pl.BlockSpec((nb, 1, 4 * w), lambda j, i: (j, 0, 0)), state_spec]
                 + [_full(x.shape) for x in weights],
        out_specs=[pl.BlockSpec((nb, steps, w), lambda j, i: (j, i, 0)), state_spec],
        out_shape=[jax.ShapeDtypeStruct((b, t, w), f32), jax.ShapeDtypeStruct((b, w, RWKV_HEAD), f32)],
        scratch_shapes=[pltpu.VMEM((nb, 1, 4 * w), f32),
                        pltpu.VMEM((nb * (w // HEAD_PAIR), HEAD_PAIR, HEAD_PAIR), f32)],
        compiler_params=_params("arbitrary", "arbitrary"),
        name="rwkv7_mixer",
    )(prk.reshape(b, t, 4 * w), prev.reshape(b, 1, 4 * w).astype(f32), s0.reshape(b, w, RWKV_HEAD).astype(f32),
      *weights)
    return y.reshape(b * t, w), s_fin.reshape(b, w // RWKV_HEAD, RWKV_HEAD, RWKV_HEAD)


def _outproj_kernel(x_ref, s5_ref, rw_ref, wo_ref, g_ref, wq_ref, x1_ref, q_ref, *, s5w):
    x1 = x_ref[...] + _dot(s5_ref[...], wo_ref[:s5w, :]) + _dot(rw_ref[...], wo_ref[s5w:, :])
    x1_ref[...] = x1
    q_ref[...] = _dot(_rms(x1, g_ref[...]), wq_ref[...]).astype(bf16)


def _outproj(x, s5y, rwy, w_out, g_ca, w_cq):
    rows, d = x.shape
    s5w = s5y.shape[1]
    rww = rwy.shape[1]
    tm = _tile(rows)
    rt = lambda c: pl.BlockSpec((tm, c), lambda i: (i, 0))
    return pl.pallas_call(
        functools.partial(_outproj_kernel, s5w=s5w),
        grid=(rows // tm,),
        in_specs=[rt(d), rt(s5w), rt(rww), _full(w_out.shape), _full((1, d)), _full(w_cq.shape)],
        out_specs=[rt(d), rt(d)],
        out_shape=[jax.ShapeDtypeStruct((rows, d), f32), jax.ShapeDtypeStruct((rows, d), bf16)],
        compiler_params=_params("parallel"),
        name="outproj_q",
    )(x, s5y, rwy, w_out, g_ca.reshape(1, d), w_cq)


def _kv_to_head_tiles(x):
    b, m, h, hd = x.shape
    return x.reshape(b, m, h, hd // LANES, LANES).transpose(0, 1, 3, 2, 4).reshape(b * m * (hd // LANES) * h, LANES)


def _kv_from_head_tiles(x, b, m, h, hd):
    return x.reshape(b, m, hd // LANES, h, LANES).transpose(0, 1, 3, 2, 4).reshape(b, m, h, hd)


def _memkv_kernel(m_ref, g_ref, w_ref, k_ref, v_ref):
    tm, d = m_ref.shape
    hd = d // MEM_HEADS
    parts = hd // LANES
    per_token = parts * MEM_HEADS
    nb = _rms(m_ref[...], g_ref[...]).astype(bf16)
    for ref, cols in ((k_ref, slice(0, d)), (v_ref, slice(d, 2 * d))):
        kv = jnp.dot(nb, w_ref[:, cols], preferred_element_type=f32)
        for h in range(MEM_HEADS):
            for c in range(parts):
                lo = h * hd + c * LANES
                ref[pl.ds(c * MEM_HEADS + h, tm, stride=per_token), :] = kv[:, lo:lo + LANES]


def _memkv(mem, g, w):
    rows, d = mem.shape
    tm = _tile(rows)
    per_token = d // LANES
    out = pl.BlockSpec((tm * per_token, LANES), lambda i: (i, 0))
    return pl.pallas_call(
        _memkv_kernel,
        grid=(rows // tm,),
        in_specs=[pl.BlockSpec((tm, d), lambda i: (i, 0)), _full((1, d)), _full(w.shape)],
        out_specs=[out, out],
        out_shape=[jax.ShapeDtypeStruct((rows * per_token, LANES), f32)] * 2,
        compiler_params=_params("parallel"),
        name="memory_kv",
    )(mem, g.reshape(1, d), w)


def _attn_kernel(q_ref, x1_ref, k_ref, v_ref, wo_ref, x2_ref):
    d = q_ref.shape[-1]
    hd = d // MEM_HEADS
    parts = hd // LANES
    per_token = parts * MEM_HEADS
    n_mem = k_ref.shape[0] // per_token
    scale = hd ** -0.5
    q = q_ref[...]

    def head(ref, h):
        return jnp.concatenate([ref[pl.ds(c * MEM_HEADS + h, n_mem, stride=per_token), :].astype(bf16)
                                for c in range(parts)], axis=-1)

    heads = []
    for h in range(MEM_HEADS):
        s = _dg(q[:, h * hd:(h + 1) * hd], head(k_ref, h), _NT) * scale
        e = jnp.exp(s - jnp.max(s, axis=-1, keepdims=True))
        prob = e / jnp.sum(e, axis=-1, keepdims=True)
        heads.append(_dot(prob, head(v_ref, h)))
    o = jnp.concatenate(heads, axis=-1)
    x2_ref[...] = x1_ref[...] + _dot(o, wo_ref[...])


def _attn(q, x1, mk, mv, w_co, t, b):
    d = q.shape[-1]
    rows_per_b = mk.shape[0] // b
    tt = min(ROW_TILE, t)
    assert t % tt == 0
    nt = t // tt
    xt = pl.BlockSpec((tt, d), lambda j, i: (j * nt + i, 0))
    kv = pl.BlockSpec((rows_per_b, LANES), lambda j, i: (j, 0))
    return pl.pallas_call(
        _attn_kernel,
        grid=(b, nt),
        in_specs=[xt, xt, kv, kv, _full(w_co.shape)],
        out_specs=xt,
        out_shape=jax.ShapeDtypeStruct((b * t, d), f32),
        compiler_params=_params("parallel", "parallel"),
        name="mem_attention",
    )(q, x1, mk, mv, w_co)


def _two_group_specs(tm, cols, na_tiles):
    a = pl.BlockSpec((tm, cols), lambda i, *_: (jnp.minimum(i, na_tiles - 1), 0))
    b = pl.BlockSpec((tm, cols), lambda i, *_: (jnp.maximum(i - na_tiles, 0), 0))
    return a, b


def _split2(x):
    hi = x.astype(bf16)
    return hi, (x - hi.astype(f32)).astype(bf16)


def _token_chunk(j, rows, nchunk, first=0):
    return pl.ds(first * nchunk + j, rows, stride=nchunk)


def _to_token_tiles(ref, x, first=0):
    rows, d = x.shape
    nchunk = d // LANES
    for j in range(nchunk):
        ref[_token_chunk(j, rows, nchunk, first), :] = x[:, j * LANES:(j + 1) * LANES]


def _token_rows(t, nchunk):
    return pl.ds(pl.multiple_of(t * nchunk, nchunk), nchunk)


def _router_kernel(xa_ref, xb_ref, g_ref, rw_ref, rb_ref, ht_ref, e_ref, rank_ref, gate_ref, cnt_ref,
                   *, n_exp, na_tiles):
    i = pl.program_id(0)
    tm, d = xa_ref.shape

    @pl.when(i == 0)
    def _():
        cnt_ref[...] = jnp.zeros_like(cnt_ref)

    h = _rms(jnp.where(i < na_tiles, xa_ref[...], xb_ref[...]), g_ref[...])
    _to_token_tiles(ht_ref, h)

    hh, hl = _split2(h)
    wh, wl = _split2(rw_ref[...])
    logits = (jnp.dot(hh, wh, preferred_element_type=f32) + jnp.dot(hh, wl, preferred_element_type=f32)
              + jnp.dot(hl, wh, preferred_element_type=f32) + rb_ref[...])

    lane = lax.broadcasted_iota(jnp.int32, (tm, n_exp), 1)
    work = logits
    sels, tops = [], []
    for _ in range(TOP_K):
        m = jnp.max(work, axis=-1, keepdims=True)
        idx = jnp.min(jnp.where(work == m, lane, n_exp), axis=-1, keepdims=True)
        sel = lane == idx
        sels.append(sel)
        tops.append(m)
        work = jnp.where(sel, -jnp.inf, work)
    chosen = sels[0] | sels[1] | sels[2] | sels[3]

    onehot = jnp.where(chosen, 1.0, 0.0)
    rr = lax.broadcasted_iota(jnp.int32, (tm, tm), 0)
    cc = lax.broadcasted_iota(jnp.int32, (tm, tm), 1)
    before = (cc < rr).astype(bf16)
    rank = jnp.dot(before, onehot.astype(bf16), preferred_element_type=f32) + cnt_ref[...]
    cnt_ref[...] = cnt_ref[...] + jnp.sum(onehot, axis=0, keepdims=True)

    den = sum(jnp.exp(t - tops[0]) for t in tops)
    e_cols, r_cols, g_cols = [], [], []
    for sel, top in zip(sels, tops):
        e_cols.append(jnp.sum(jnp.where(sel, lane, 0), axis=-1, keepdims=True))
        r_cols.append(jnp.sum(jnp.where(sel, rank, 0.0), axis=-1, keepdims=True))
        g_cols.append(jnp.exp(top - tops[0]) / den)
    e_ref[...] = jnp.concatenate(e_cols, axis=-1)
    rank_ref[...] = jnp.concatenate(r_cols, axis=-1).astype(jnp.int32)
    gate_ref[...] = jnp.concatenate(g_cols, axis=-1)


def _router(xa, xb, g, rw, rb):
    d = xa.shape[1]
    n = xa.shape[0] + xb.shape[0]
    n_exp = rw.shape[1]
    tm = ROW_TILE
    assert xa.shape[0] % tm == 0 and xb.shape[0] % tm == 0
    na_tiles = xa.shape[0] // tm
    sa, sb = _two_group_specs(tm, d, na_tiles)
    rt = lambda c: pl.BlockSpec((tm, c), lambda i: (i, 0))
    return pl.pallas_call(
        functools.partial(_router_kernel, n_exp=n_exp, na_tiles=na_tiles),
        grid=(n // tm,),
        in_specs=[sa, sb, _full((1, d)), _full((d, n_exp)), _full((1, n_exp))],
        out_specs=[pl.BlockSpec((tm * (d // LANES), LANES), lambda i: (i, 0)), rt(TOP_K), rt(TOP_K), rt(TOP_K),
                   _full((1, n_exp))],
        out_shape=[jax.ShapeDtypeStruct((n * (d // LANES), LANES), f32), jax.ShapeDtypeStruct((n, TOP_K), jnp.int32),
                   jax.ShapeDtypeStruct((n, TOP_K), jnp.int32), jax.ShapeDtypeStruct((n, TOP_K), f32),
                   jax.ShapeDtypeStruct((1, n_exp), f32)],
        compiler_params=_params("arbitrary"),
        name="moe_router",
    )(xa, xb, g.reshape(1, d), rw.astype(f32), rb.reshape(1, n_exp).astype(f32))


def _invert_kernel(dest_ref, zs_ref, ze_ref, tok_ref, *, n):
    def for_each_unmapped_row(fn):
        def segment(seg, c):
            return lax.fori_loop(zs_ref[seg], ze_ref[seg], fn, c)
        lax.fori_loop(0, zs_ref.shape[0], segment, 0)

    fill = lambda r, c: _set(tok_ref, r, 0, c)
    for_each_unmapped_row(fill)
    for kk in range(TOP_K):
        def place(t, c, kk=kk):
            return _set(tok_ref, dest_ref[kk * n + t], t, c)
        lax.fori_loop(0, n, place, 0, unroll=8)


def _set(ref, idx, val, carry):
    ref[idx] = val
    return carry


def _invert(dest_km, zero_start, zero_end, n, n_rows):
    grid_spec = pltpu.PrefetchScalarGridSpec(
        num_scalar_prefetch=3, grid=(1,), in_specs=[],
        out_specs=pl.BlockSpec(memory_space=pltpu.SMEM))
    return pl.pallas_call(
        functools.partial(_invert_kernel, n=n),
        grid_spec=grid_spec,
        out_shape=jax.ShapeDtypeStruct((n_rows,), jnp.int32),
        compiler_params=_params("arbitrary"),
        name="moe_invert",
    )(dest_km, zero_start, zero_end)


def _expert_kernel(be_ref, nv_ref, nxt_ref, tok_ref, ht_hbm, wgu_hbm, bgu_ref, wd_hbm, bd_ref, y_ref,
                   xbuf, xsem, wgu_f, wd_f, wsem, wgu_b, wd_b, *, layer, blk):
    i = pl.program_id(0)
    slot = i % 2
    nvalid = nv_ref[0]
    dff = wd_b.shape[0]
    nchunk = wgu_b.shape[0] // LANES
    last_block = pl.num_programs(0) - 1

    def gather(block, into):
        for r in range(blk):
            tok = tok_ref[block * blk + r]
            pltpu.make_async_copy(ht_hbm.at[_token_rows(tok, nchunk)], xbuf.at[into, _token_rows(r, nchunk)],
                                  xsem.at[into]).start(priority=r % DMA_PRIORITIES)

    def weight_copies(e):
        return (pltpu.make_async_copy(wgu_hbm.at[layer, e], wgu_f, wsem.at[0]),
                pltpu.make_async_copy(wd_hbm.at[layer, e], wd_f, wsem.at[1]))

    @pl.when(i == 0)
    def _():
        gather(0, 0)
        for cp in weight_copies(be_ref[0]):
            cp.start()

    @pl.when((i < nvalid) & ((i == 0) | (be_ref[i] != be_ref[jnp.maximum(i - 1, 0)])))
    def _():
        for cp in weight_copies(be_ref[i]):
            cp.wait()
        wgu_b[...] = wgu_f[...].astype(bf16)
        wd_b[...] = wd_f[...].astype(bf16)

        @pl.when(nxt_ref[i] >= 0)
        def _():
            for cp in weight_copies(nxt_ref[i]):
                cp.start()

    @pl.when(i < nvalid)
    def _():
        pltpu.make_async_copy(xbuf.at[slot], xbuf.at[slot], xsem.at[slot]).wait()
        gather(jnp.minimum(i + 1, last_block), 1 - slot)
        x = jnp.concatenate([xbuf[slot, _token_chunk(j, blk, nchunk), :].astype(bf16) for j in range(nchunk)],
                            axis=-1)
        hgu = jnp.dot(x, wgu_b[...], preferred_element_type=f32) + bgu_ref[0, 0]
        hg = jnp.minimum(hgu[:, :dff], SWIGLU_LIMIT)
        hu = jnp.clip(hgu[:, dff:], -SWIGLU_LIMIT, SWIGLU_LIMIT)
        act = (hu + 1.0) * hg * jax.nn.sigmoid(SWIGLU_ALPHA * hg)
        _to_token_tiles(y_ref, _dot(act, wd_b[...]) + bd_ref[0, 0])

    @pl.when(i >= nvalid)
    def _():
        y_ref[...] = jnp.zeros_like(y_ref)

    @pl.when(i == nvalid)
    def _():
        pltpu.make_async_copy(xbuf.at[slot], xbuf.at[slot], xsem.at[slot]).wait()


def _experts(ht, row_tok, block_e, nvalid, next_e, w_gu, b_gu, w_down, b_down, l, blk):
    depth, n_exp, d, dff2 = w_gu.shape
    dff = dff2 // 2
    nchunk = d // LANES
    n_rows = row_tok.shape[0]
    bspec = lambda cols: pl.BlockSpec((1, 1, 1, cols), lambda i, be, *_: (l, be[i], 0, 0))
    grid_spec = pltpu.PrefetchScalarGridSpec(
        num_scalar_prefetch=4,
        grid=(n_rows // blk,),
        in_specs=[pl.BlockSpec(memory_space=pl.ANY), pl.BlockSpec(memory_space=pl.ANY), bspec(dff2),
                  pl.BlockSpec(memory_space=pl.ANY), bspec(d)],
        out_specs=pl.BlockSpec((blk * nchunk, LANES), lambda i, *_: (i, 0)),
        scratch_shapes=[pltpu.VMEM((2, blk * nchunk, LANES), f32), pltpu.SemaphoreType.DMA((2,)),
                        pltpu.VMEM((d, dff2), f32), pltpu.VMEM((dff, d), f32), pltpu.SemaphoreType.DMA((2,)),
                        pltpu.VMEM((d, dff2), bf16), pltpu.VMEM((dff, d), bf16)],
    )
    return pl.pallas_call(
        functools.partial(_expert_kernel, layer=l, blk=blk),
        grid_spec=grid_spec,
        out_shape=jax.ShapeDtypeStruct((n_rows * nchunk, LANES), f32),
        compiler_params=_params("arbitrary"),
        name="moe_experts",
    )(block_e, nvalid, next_e, row_tok, ht, w_gu, b_gu.reshape(depth, n_exp, 1, dff2), w_down,
      b_down.reshape(depth, n_exp, 1, d))


def _combine_kernel(dest_ref, xa_ref, xb_ref, gate_ref, yb_hbm, g_ref, oa_ref, ob_ref, ybuf, sem,
                    *, tm, n, na_tiles, final):
    i = pl.program_id(0)
    slot = i % 2
    nchunk = ybuf.shape[2] // tm

    last_tile = pl.num_programs(0) - 1

    def copy_row(tile, into, kk, t, prio):
        row = dest_ref[kk * n + tile * tm + t]
        pltpu.make_async_copy(yb_hbm.at[_token_rows(row, nchunk)], ybuf.at[into, kk, _token_rows(t, nchunk)],
                              sem.at[into]).start(priority=prio)

    @pl.when(i == 0)
    def _():
        for kk in range(TOP_K):
            def issue(pair, c, kk=kk):
                for prio in range(DMA_PRIORITIES):
                    copy_row(0, 0, kk, pair * DMA_PRIORITIES + prio, prio)
                return c
            lax.fori_loop(0, tm // DMA_PRIORITIES, issue, 0, unroll=4)

    pltpu.make_async_copy(ybuf.at[slot], ybuf.at[slot], sem.at[slot]).wait()
    nxt = jnp.minimum(i + 1, last_tile)
    for kk in range(TOP_K):
        for t in range(tm):
            copy_row(nxt, 1 - slot, kk, t, t % DMA_PRIORITIES)
    gate = gate_ref[...]
    chunks = []
    for j in range(nchunk):
        acc = gate[:, 0:1] * ybuf[slot, 0, _token_chunk(j, tm, nchunk), :]
        for kk in range(1, TOP_K):
            acc = acc + gate[:, kk:kk + 1] * ybuf[slot, kk, _token_chunk(j, tm, nchunk), :]
        chunks.append(acc)
    x3 = jnp.where(i < na_tiles, xa_ref[...], xb_ref[...]) + jnp.concatenate(chunks, axis=-1)
    out = _rms(x3, g_ref[...]) if final else x3

    @pl.when(i < na_tiles)
    def _():
        oa_ref[...] = out

    @pl.when(i >= na_tiles)
    def _():
        ob_ref[...] = out

    @pl.when(i == last_tile)
    def _():
        pltpu.make_async_copy(ybuf.at[1 - slot], ybuf.at[1 - slot], sem.at[1 - slot]).wait()


def _combine(dest_km, xa, xb, gate, yb, g_final, final):
    d = xa.shape[1]
    n = xa.shape[0] + xb.shape[0]
    tm = COMBINE_TILE
    assert xa.shape[0] % tm == 0 and xb.shape[0] % tm == 0
    na_tiles = xa.shape[0] // tm
    sa, sb = _two_group_specs(tm, d, na_tiles)
    grid_spec = pltpu.PrefetchScalarGridSpec(
        num_scalar_prefetch=1,
        grid=(n // tm,),
        in_specs=[sa, sb, pl.BlockSpec((tm, TOP_K), lambda i, dst: (i, 0)), pl.BlockSpec(memory_space=pl.ANY),
                  pl.BlockSpec((1, d), lambda i, dst: (0, 0))],
        out_specs=[sa, sb],
        scratch_shapes=[pltpu.VMEM((2, TOP_K, tm * (d // LANES), LANES), f32), pltpu.SemaphoreType.DMA((2,))],
    )
    return pl.pallas_call(
        functools.partial(_combine_kernel, tm=tm, n=n, na_tiles=na_tiles, final=final),
        grid_spec=grid_spec,
        out_shape=[jax.ShapeDtypeStruct(xa.shape, f32), jax.ShapeDtypeStruct(xb.shape, f32)],
        compiler_params=_params("arbitrary"),
        name="moe_combine",
    )(dest_km, xa, xb, gate, yb, g_final.reshape(1, d))


def _moe(xa, xb, g_ffn, router_w, router_b, w_gu, b_gu, w_down, b_down, l, g_out, final):
    n = xa.shape[0] + xb.shape[0]
    n_exp = router_w.shape[1]
    blk = MOE_ROWS
    ht, top_e, top_rank, gate, counts = _router(xa, xb, g_ffn, router_w, router_b)

    counts = counts.reshape(n_exp).astype(jnp.int32)
    padded = (counts + blk - 1) // blk * blk
    pad_end = jnp.cumsum(padded)
    pad_start = pad_end - padded
    dest_km = (pad_start[top_e] + top_rank).T.reshape(-1)
    n_rows = -(-(n * TOP_K) // blk) * blk + n_exp * blk
    block_start = jnp.arange(n_rows // blk, dtype=jnp.int32) * blk
    block_e = jnp.minimum(jnp.sum(pad_end[None, :] <= block_start[:, None], axis=1), n_exp - 1).astype(jnp.int32)
    nvalid = (pad_end[-1:] // blk).astype(jnp.int32)
    zero_start = jnp.concatenate([pad_start + counts, pad_end[-1:]]).astype(jnp.int32)
    zero_end = jnp.concatenate([pad_end, jnp.full((1,), n_rows, jnp.int32)]).astype(jnp.int32)

    has_rows = padded > 0
    ids = jnp.arange(n_exp, dtype=jnp.int32)
    after = jnp.where(has_rows[None, :] & (ids[None, :] > ids[:, None]), ids[None, :], n_exp)
    next_of = jnp.min(after, axis=1)
    next_e = jnp.where(next_of < n_exp, next_of, -1).astype(jnp.int32)[block_e]

    row_tok = _invert(dest_km, zero_start, zero_end, n, n_rows)
    yb = _experts(ht, row_tok, block_e, nvalid, next_e, w_gu, b_gu, w_down, b_down, l, blk)
    return _combine(dest_km, xa, xb, gate, yb, g_out, final)


def _mixers(x, t, b, h0r, h0i, prev, s0, P, l):
    s5w = P['s5_d'].shape[-1]
    u, prk = _inproj(x, P['norm_mix_g'][l], P['w_in'][l].astype(bf16), s5w)
    ar, ai, bre, bim, cre_m, cim_m = _s5_discretize(P['s5_lambda_re'][l], P['s5_lambda_im'][l], P['s5_log_dt'][l],
                                                    P['s5_b_re'][l], P['s5_b_im'][l], P['s5_c_re'][l], P['s5_c_im'][l])
    s5y, hr, hi = _s5(jnp.transpose(u.reshape(b, t, s5w), (1, 0, 2)), h0r, h0i, ar, ai, bre, bim, cre_m, cim_m,
                      P['s5_d'][l].reshape(1, s5w).astype(f32), P['s5_glu_w'][l].astype(bf16),
                      P['s5_glu_b'][l].reshape(1, s5w).astype(f32))
    s5y = jnp.transpose(s5y, (1, 0, 2)).reshape(b * t, s5w)
    rwy, s_fin = _rwkv(prk, prev, s0, t, b, P['rwkv_mu'][l], P['rwkv_w0'][l], P['rwkv_w1'][l], P['rwkv_w2'][l],
                       P['rwkv_a0'][l], P['rwkv_a1'][l], P['rwkv_a2'][l], P['rwkv_g1'][l], P['rwkv_g2'][l],
                       P['rwkv_k_k'][l], P['rwkv_k_a'][l], P['rwkv_r_k'][l], P['rwkv_ln_w'][l], P['rwkv_ln_b'][l])
    x1, q = _outproj(x, s5y, rwy, P['w_out'][l].astype(bf16), P['norm_ca_g'][l], P['w_cq'][l].astype(bf16))
    shift_last = prk.reshape(b, t, -1)[:, t - 1]
    return x1, q, hr, hi, shift_last, s_fin


def kernel(x_prompt, x_sample, mem_prompt, state_s5_re, state_s5_im, state_rwkv_shift, state_rwkv_wkv, cache_mem_k, cache_mem_v, norm_mix_g, w_in, s5_lambda_re, s5_lambda_im, s5_log_dt, s5_b_re, s5_b_im, s5_c_re, s5_c_im, s5_d, s5_glu_w, s5_glu_b, rwkv_mu, rwkv_w0, rwkv_w1, rwkv_w2, rwkv_a0, rwkv_a1, rwkv_a2, rwkv_g1, rwkv_g2, rwkv_k_k, rwkv_k_a, rwkv_r_k, rwkv_ln_w, rwkv_ln_b, w_out, norm_ca_g, norm_mem_g, w_cq, w_mem_kv, w_co, norm_ffn_g, router_w, router_b, ex_w_gu, ex_b_gu, ex_w_down, ex_b_down, norm_final_g):
    P = dict(norm_mix_g=norm_mix_g, w_in=w_in, s5_lambda_re=s5_lambda_re, s5_lambda_im=s5_lambda_im,
             s5_log_dt=s5_log_dt, s5_b_re=s5_b_re, s5_b_im=s5_b_im, s5_c_re=s5_c_re, s5_c_im=s5_c_im,
             s5_d=s5_d, s5_glu_w=s5_glu_w, s5_glu_b=s5_glu_b, rwkv_mu=rwkv_mu, rwkv_w0=rwkv_w0,
             rwkv_w1=rwkv_w1, rwkv_w2=rwkv_w2, rwkv_a0=rwkv_a0, rwkv_a1=rwkv_a1, rwkv_a2=rwkv_a2,
             rwkv_g1=rwkv_g1, rwkv_g2=rwkv_g2, rwkv_k_k=rwkv_k_k, rwkv_k_a=rwkv_k_a, rwkv_r_k=rwkv_r_k,
             rwkv_ln_w=rwkv_ln_w, rwkv_ln_b=rwkv_ln_b, w_out=w_out, norm_ca_g=norm_ca_g, w_cq=w_cq)
    depth = norm_mix_g.shape[0]
    bp, tp, d = x_prompt.shape
    bs, ts, _ = x_sample.shape
    n_mem = mem_prompt.shape[1]
    s5_groups, s5_state = s5_lambda_re.shape[1:]
    nstate = s5_groups * s5_state
    rw_in = state_rwkv_shift.shape[-1]
    heads = state_rwkv_wkv.shape[2]
    mh = cache_mem_k.shape[3]
    assert mh == MEM_HEADS

    xp = x_prompt.reshape(bp * tp, d)
    xs = x_sample.reshape(bs * ts, d)
    outs = {k: [] for k in ('p_re', 'p_im', 'p_sh', 'p_wkv', 'p_mk', 'p_mv', 's_re', 's_im', 's_sh', 's_wkv')}
    for l in range(depth):
        mk, mv = _memkv(mem_prompt.reshape(bp * n_mem, d), norm_mem_g[l], w_mem_kv[l].astype(bf16))
        zeros_state = jnp.zeros((bp, nstate), f32)
        x1p, qp, re_p, im_p, sh_p, wkv_p = _mixers(
            xp, tp, bp, zeros_state, zeros_state, jnp.zeros((bp, rw_in), f32),
            jnp.zeros((bp, heads, RWKV_HEAD, RWKV_HEAD), f32), P, l)
        x1s, qs, re_s, im_s, sh_s, wkv_s = _mixers(
            xs, ts, bs, state_s5_re[l].reshape(bs, nstate), state_s5_im[l].reshape(bs, nstate),
            state_rwkv_shift[l], state_rwkv_wkv[l], P, l)
        wco = w_co[l].astype(bf16)
        x2p = _attn(qp, x1p, mk, mv, wco, tp, bp)
        x2s = _attn(qs, x1s, _kv_to_head_tiles(cache_mem_k[l]), _kv_to_head_tiles(cache_mem_v[l]), wco, ts, bs)
        last = l == depth - 1
        xp, xs = _moe(x2p, x2s, norm_ffn_g[l], router_w[l], router_b[l], ex_w_gu, ex_b_gu, ex_w_down, ex_b_down,
                      l, norm_final_g, last)
        outs['p_re'].append(re_p.reshape(bp, s5_groups, s5_state))
        outs['p_im'].append(im_p.reshape(bp, s5_groups, s5_state))
        outs['p_sh'].append(sh_p)
        outs['p_wkv'].append(wkv_p)
        outs['p_mk'].append(_kv_from_head_tiles(mk, bp, n_mem, mh, d // mh))
        outs['p_mv'].append(_kv_from_head_tiles(mv, bp, n_mem, mh, d // mh))
        outs['s_re'].append(re_s.reshape(bs, s5_groups, s5_state))
        outs['s_im'].append(im_s.reshape(bs, s5_groups, s5_state))
        outs['s_sh'].append(sh_s)
        outs['s_wkv'].append(wkv_s)
    st = lambda k: jnp.stack(outs[k])
    return (xp.reshape(bp, tp, d), xs.reshape(bs, ts, d), st('p_re'), st('p_im'), st('p_sh'), st('p_wkv'),
            st('p_mk'), st('p_mv'), st('s_re'), st('s_im'), st('s_sh'), st('s_wkv'))
```

```python
import functools

import jax
import jax.numpy as jnp
from jax import lax
from jax.experimental import pallas as pl
from jax.experimental.pallas import tpu as pltpu

f32 = jnp.float32
bf16 = jnp.bfloat16

RWKV_HEAD = 64
HEAD_PAIR = 2 * RWKV_HEAD
DECAY_SCALE = 0.6065306597
MEM_HEADS = 4
TOP_K = 4
SWIGLU_LIMIT = 7.0
SWIGLU_ALPHA = 1.702
RMS_EPS = 1e-5
GN_EPS = 64e-5

V7X_VMEM_LIMIT = 56 * 1024 * 1024
ROW_TILE = 512
MOE_ROWS = 256
COMBINE_TILE = 256
WKV_CHUNK = 64
WKV_BATCH = 2
S5_CHUNK = 64
S5_BLOCK_CHANNELS = 128
LANES = 128
DMA_PRIORITIES = 2
ATTN_SHORT_BATCH = 4

_NT = (((1,), (1,)), ((), ()))
_TN = (((0,), (0,)), ((), ()))


def _params(*sem):
    return pltpu.CompilerParams(dimension_semantics=sem, vmem_limit_bytes=V7X_VMEM_LIMIT)


def _dot(a, b):
    return jnp.dot(a.astype(bf16), b.astype(bf16), preferred_element_type=f32)


def _dg(a, b, dims):
    return lax.dot_general(a.astype(bf16), b.astype(bf16), dims, preferred_element_type=f32)


def _rms(x, g):
    return x * lax.rsqrt(jnp.mean(x * x, axis=-1, keepdims=True) + RMS_EPS) * g


def _full(shape):
    n = len(shape)
    return pl.BlockSpec(shape, lambda *_: (0,) * n)


def _tile(rows, tile=ROW_TILE):
    t = min(tile, rows)
    assert rows % t == 0
    return t


def _inproj_kernel(x_ref, g_ref, w_ref, u_ref, p_ref, *, s5w):
    nb = _rms(x_ref[...], g_ref[...]).astype(bf16)
    u_ref[...] = jnp.dot(nb, w_ref[:, :s5w], preferred_element_type=f32)
    p_ref[...] = jnp.dot(nb, w_ref[:, s5w:], preferred_element_type=f32)


def _inproj(x, g, w, s5w):
    rows, d = x.shape
    inw = w.shape[1]
    tm = _tile(rows)
    return pl.pallas_call(
        functools.partial(_inproj_kernel, s5w=s5w),
        grid=(rows // tm,),
        in_specs=[pl.BlockSpec((tm, d), lambda i: (i, 0)), _full((1, d)), _full((d, inw))],
        out_specs=[pl.BlockSpec((tm, s5w), lambda i: (i, 0)), pl.BlockSpec((tm, inw - s5w), lambda i: (i, 0))],
        out_shape=[jax.ShapeDtypeStruct((rows, s5w), f32), jax.ShapeDtypeStruct((rows, inw - s5w), f32)],
        compiler_params=_params("parallel"),
        name="inproj",
    )(x, g.reshape(1, d), w)


def _s5_kernel(u_ref, h0r_ref, h0i_ref, ar_ref, ai_ref, bre_ref, bim_ref, cre_ref, cim_ref, d_ref, gw_ref, gb_ref,
               y_ref, hr_ref, hi_ref, xr_s, xi_s, *, steps, bt, lane_chunk):
    i = pl.program_id(1)
    width = u_ref.shape[-1]
    nstate = ar_ref.shape[-1]

    @pl.when(i == 0)
    def _():
        hr_ref[...] = h0r_ref[...]
        hi_ref[...] = h0i_ref[...]

    u = u_ref[...].reshape(steps * bt, width)
    ub = u.astype(bf16)
    nblk, cb, sb = bre_ref.shape
    for j in range(nblk):
        uj = ub[:, j * cb:(j + 1) * cb]
        xr_s[:, j * sb:(j + 1) * sb] = jnp.dot(uj, bre_ref[j], preferred_element_type=f32)
        xi_s[:, j * sb:(j + 1) * sb] = jnp.dot(uj, bim_ref[j], preferred_element_type=f32)

    for c in range(nstate // lane_chunk):
        sl = slice(c * lane_chunk, (c + 1) * lane_chunk)
        arb = jnp.broadcast_to(ar_ref[:, sl], (bt, lane_chunk))
        aib = jnp.broadcast_to(ai_ref[:, sl], (bt, lane_chunk))

        def step(t, carry, sl=sl, arb=arb, aib=aib):
            hr, hi = carry
            r0 = pl.multiple_of(t * bt, bt)
            nhr = arb * hr - aib * hi + xr_s[pl.ds(r0, bt), sl]
            nhi = arb * hi + aib * hr + xi_s[pl.ds(r0, bt), sl]
            xr_s[pl.ds(r0, bt), sl] = nhr
            xi_s[pl.ds(r0, bt), sl] = nhi
            return nhr, nhi

        hr, hi = lax.fori_loop(0, steps, step, (hr_ref[:, sl], hi_ref[:, sl]), unroll=2)
        hr_ref[:, sl] = hr
        hi_ref[:, sl] = hi

    y = jnp.concatenate(
        [jnp.dot(xr_s[:, j * sb:(j + 1) * sb].astype(bf16), cre_ref[j], preferred_element_type=f32)
         + jnp.dot(xi_s[:, j * sb:(j + 1) * sb].astype(bf16), cim_ref[j], preferred_element_type=f32)
         for j in range(nblk)], axis=-1) + d_ref[...] * u
    z = jax.nn.gelu(y)
    out = z * jax.nn.sigmoid(_dot(z, gw_ref[...]) + gb_ref[...])
    y_ref[...] = out.reshape(steps, bt, width)


def _s5(u3, h0r, h0i, ar, ai, bre, bim, cre_m, cim_m, d, gw, gb):
    t, b, width = u3.shape
    nstate = ar.shape[-1]
    bt = 8
    steps = min(S5_CHUNK, t)
    assert b % bt == 0 and t % steps == 0
    kern = functools.partial(_s5_kernel, steps=steps, bt=bt, lane_chunk=512)
    st_spec = pl.BlockSpec((bt, nstate), lambda j, i: (j, 0))
    y, hr, hi = pl.pallas_call(
        kern,
        grid=(b // bt, t // steps),
        in_specs=[pl.BlockSpec((steps, bt, width), lambda j, i: (i, j, 0)), st_spec, st_spec,
                  _full((1, nstate)), _full((1, nstate)), _full(bre.shape), _full(bim.shape),
                  _full(cre_m.shape), _full(cim_m.shape), _full((1, width)), _full((width, width)),
                  _full((1, width))],
        out_specs=[pl.BlockSpec((steps, bt, width), lambda j, i: (i, j, 0)), st_spec, st_spec],
        out_shape=[jax.ShapeDtypeStruct((t, b, width), f32), jax.ShapeDtypeStruct((b, nstate), f32),
                   jax.ShapeDtypeStruct((b, nstate), f32)],
        scratch_shapes=[pltpu.VMEM((steps * bt, nstate), f32), pltpu.VMEM((steps * bt, nstate), f32)],
        compiler_params=_params("arbitrary", "arbitrary"),
        name="s5_mixer",
    )(u3, h0r, h0i, ar, ai, bre, bim, cre_m, cim_m, d, gw, gb)
    return y, hr, hi


def _s5_discretize(lam_re, lam_im, log_dt, b_re, b_im, c_re, c_im):
    g, p = lam_re.shape
    dt = jnp.exp(log_dt.astype(f32))[:, None]
    lr = lam_re.astype(f32)
    li = lam_im.astype(f32)
    mag = jnp.exp(lr * dt)
    ar = mag * jnp.cos(li * dt)
    ai = mag * jnp.sin(li * dt)
    den = lr * lr + li * li
    cr = ((ar - 1.0) * lr + ai * li) / den
    ci = (ai * lr - (ar - 1.0) * li) / den
    br = b_re.astype(f32)
    bi = b_im.astype(f32)
    bbar_r = cr[..., None] * br - ci[..., None] * bi
    bbar_i = cr[..., None] * bi + ci[..., None] * br
    c = br.shape[-1]
    gb = S5_BLOCK_CHANNELS // c
    assert g % gb == 0
    eye = jnp.eye(gb, dtype=f32)
    nblk = g // gb
    bd_in = lambda m: jnp.einsum('ngpc,gh->ngchp', m.reshape(nblk, gb, p, c), eye).reshape(
        nblk, gb * c, gb * p).astype(bf16)
    bd_out = lambda m: jnp.einsum('ngcp,gh->ngphc', m.reshape(nblk, gb, c, p), eye).reshape(
        nblk, gb * p, gb * c).astype(bf16)
    return (ar.reshape(1, g * p), ai.reshape(1, g * p), bd_in(bbar_r), bd_in(bbar_i),
            bd_out(c_re.astype(f32)), bd_out(-c_im.astype(f32)))


def _rwkv_kernel(p_ref, prev_ref, s0_ref, mu_ref, w0_ref, w1_ref, w2_ref, a0_ref, a1_ref, a2_ref, g1_ref, g2_ref,
                 kk_ref, ka_ref, rk_ref, lnw_ref, lnb_ref, y_ref, sout_ref, prev_s, st_s, *, steps, nsq):
    i = pl.program_id(1)
    L = steps
    L2 = 2 * L
    nb = p_ref.shape[0]
    w = p_ref.shape[-1] // 4
    npair = w // HEAD_PAIR

    row2 = lax.broadcasted_iota(jnp.int32, (L2, HEAD_PAIR), 0)
    lane2 = lax.broadcasted_iota(jnp.int32, (L2, HEAD_PAIR), 1)
    own = (row2 < L) == (lane2 < RWKV_HEAD)
    ri = lax.broadcasted_iota(jnp.int32, (L2, L2), 0)
    ci = lax.broadcasted_iota(jnp.int32, (L2, L2), 1)
    same = (ri < L) == (ci < L)
    strict = same & (ci < ri)
    incl = same & (ci <= ri)
    eye = (ri == ci).astype(f32)
    sr = lax.broadcasted_iota(jnp.int32, (HEAD_PAIR, HEAD_PAIR), 0)
    sc = lax.broadcasted_iota(jnp.int32, (HEAD_PAIR, HEAD_PAIR), 1)
    sdiag = (sr < RWKV_HEAD) == (sc < RWKV_HEAD)

    @pl.when(i == 0)
    def _():
        prev_s[...] = prev_ref[...]
        for bb in range(nb):
            for pr in range(npair):
                sp = s0_ref[bb, pr * HEAD_PAIR:(pr + 1) * HEAD_PAIR, :]
                st_s[bb * npair + pr] = jnp.where(sdiag, jnp.concatenate([sp, sp], axis=1), 0.0)

    trow = lax.broadcasted_iota(jnp.int32, (L, 1), 0)
    ps, deltas = [], []
    for bb in range(nb):
        pb = p_ref[bb]
        shifted = jnp.where(trow == 0, prev_s[bb], pltpu.roll(pb, 1, axis=0))
        prev_s[bb] = pb[L - 1:L, :]
        ps.append(pb)
        deltas.append(shifted - pb)
    p = jnp.concatenate(ps, axis=0) if nb > 1 else ps[0]
    delta = jnp.concatenate(deltas, axis=0) if nb > 1 else deltas[0]
    mu = mu_ref[...]
    pz = p[:, 3 * w:]
    dz = delta[:, 3 * w:]
    r = p[:, :w] + delta[:, :w] * mu[0:1]
    k = p[:, w:2 * w] + delta[:, w:2 * w] * mu[1:2]
    v = p[:, 2 * w:3 * w] + delta[:, 2 * w:3 * w] * mu[2:3]
    zw = pz + dz * mu[3:4]
    za = pz + dz * mu[4:5]
    zg = pz + dz * mu[5:6]
    logw = -DECAY_SCALE * jax.nn.sigmoid(w0_ref[...] + _dot(jnp.tanh(_dot(zw, w1_ref[...])), w2_ref[...]))
    a = jax.nn.sigmoid(a0_ref[...] + _dot(_dot(za, a1_ref[...]), a2_ref[...]))
    g = _dot(jax.nn.sigmoid(_dot(zg, g1_ref[...])), g2_ref[...])
    kkraw = k * kk_ref[...]
    kmod = k * (1.0 + (a - 1.0) * ka_ref[...])
    rk = r * kmod * rk_ref[...]

    tr = lax.broadcasted_iota(jnp.int32, (nb * L, nb * L), 0)
    tc = lax.broadcasted_iota(jnp.int32, (nb * L, nb * L), 1)
    tri = ((tc <= tr) & (tc >= (tr // L) * L)).astype(bf16)
    hi = logw.astype(bf16)
    rem = logw - hi.astype(f32)
    mid = rem.astype(bf16)
    lo = (rem - mid.astype(f32)).astype(bf16)
    cum = (jnp.dot(tri, hi, preferred_element_type=f32) + jnp.dot(tri, mid, preferred_element_type=f32)
           + jnp.dot(tri, lo, preferred_element_type=f32))

    stack = lambda x: jnp.concatenate([x, x], axis=0)
    chains = [(bb, pr) for bb in range(nb) for pr in range(npair)]
    nch = range(len(chains))
    rs = [slice(bb * L, (bb + 1) * L) for bb, _ in chains]
    sls = [slice(pr * HEAD_PAIR, (pr + 1) * HEAD_PAIR) for _, pr in chains]
    cut = lambda x, c: stack(x[rs[c], sls[c]])
    own_of = lambda x, c: jnp.where(own, cut(x, c), 0.0)

    wt, rm, km, bm, kap, vmf, vm, gram = [], [], [], [], [], [], [], []
    for c in nch:
        cum_c = cut(cum, c)
        wt_c = jnp.exp(cum_c)
        winv = jnp.exp(-cum_c)
        wprev = jnp.exp(cum_c - cut(logw, c))
        kkr = own_of(kkraw, c)
        kkn = kkr * lax.rsqrt(jnp.maximum(jnp.sum(kkr * kkr, axis=-1, keepdims=True), 1e-24))
        wt.append(wt_c)
        rm.append((own_of(r, c) * wt_c).astype(bf16))
        km.append((own_of(kmod, c) * winv).astype(bf16))
        bm.append((kkn * cut(a, c) * winv).astype(bf16))
        kap.append((kkn * wprev).astype(bf16))
        vmf.append(own_of(v, c))
        vm.append(vmf[c].astype(bf16))
        gram.append(_dg(jnp.concatenate([kap[c], rm[c]], axis=0), jnp.concatenate([bm[c], km[c]], axis=0), _NT))
    n_mat = [jnp.where(strict, x[:L2, :L2], 0.0) for x in gram]
    p_mat = [jnp.where(strict, x[:L2, L2:], 0.0) for x in gram]
    g_mat = [jnp.where(incl, x[L2:, :L2], 0.0) for x in gram]
    q_mat = [jnp.where(incl, x[L2:, L2:], 0.0) for x in gram]

    neg = [-x for x in n_mat]
    inv = [eye + x for x in neg]
    if nsq > 0:
        pw = [_dot(x, x) for x in neg]
        for _ in range(nsq - 1):
            both = [_dot(pw[c], jnp.concatenate([inv[c], pw[c]], axis=1)) for c in nch]
            inv = [inv[c] + both[c][:, :L2] for c in nch]
            pw = [x[:, L2:] for x in both]
        inv = [inv[c] + _dot(pw[c], inv[c]) for c in nch]

    s_old = [st_s[c] for c in nch]
    rhs = [_dg(kap[c], s_old[c], _NT) + _dot(p_mat[c], vm[c]) for c in nch]
    ym0 = [_dg(rm[c], s_old[c], _NT) for c in nch]
    um = [(-_dot(inv[c], rhs[c])).astype(bf16) for c in nch]
    vu = [jnp.concatenate([vm[c], um[c]], axis=0) for c in nch]
    ym = [ym0[c] + _dot(jnp.concatenate([q_mat[c], g_mat[c]], axis=1), vu[c]) for c in nch]
    for c in nch:
        st_s[c] = (s_old[c] + _dg(vu[c], jnp.concatenate([km[c], bm[c]], axis=0), _TN)) * wt[c][L - 1:L, :]

    for c, (bb, pr) in enumerate(chains):
        mean = jnp.sum(ym[c], axis=-1, keepdims=True) * (1.0 / RWKV_HEAD)
        yc = jnp.where(own, ym[c] - mean, 0.0)
        var = jnp.sum(yc * yc, axis=-1, keepdims=True) * (1.0 / RWKV_HEAD)
        yn = yc * lax.rsqrt(var + GN_EPS)
        bonus = jnp.sum(own_of(rk, c), axis=-1, keepdims=True) * vmf[c]
        mix = (yn[:L] + yn[L:]) * lnw_ref[:, sls[c]] + lnb_ref[:, sls[c]] + (bonus[:L] + bonus[L:])
        y_ref[bb, :, sls[c]] = mix * g[rs[c], sls[c]]

    @pl.when(i == pl.num_programs(1) - 1)
    def _():
        for c, (bb, pr) in enumerate(chains):
            s_new = st_s[c]
            sout_ref[bb, pr * HEAD_PAIR:(pr + 1) * HEAD_PAIR, :] = s_new[:, :RWKV_HEAD] + s_new[:, RWKV_HEAD:]


def _rwkv(prk, prev, s0, t, b, mu, w0, w1, w2, a0, a1, a2, g1, g2, k_k, k_a, r_k, ln_w, ln_b):
    w = prk.shape[-1] // 4
    steps = min(WKV_CHUNK, t)
    nb = min(WKV_BATCH * max(1, WKV_CHUNK // (4 * steps)), b)
    assert t % steps == 0 and w % HEAD_PAIR == 0 and b % nb == 0
    nsq = max((steps - 1).bit_length() - 1, 0)
    row = lambda x: x.reshape(1, -1).astype(f32)
    wb = lambda x: x.astype(bf16)
    weights = [mu.astype(f32), row(w0), wb(w1), wb(w2), row(a0), wb(a1), wb(a2), wb(g1), wb(g2),
               row(k_k), row(k_a), row(r_k), row(ln_w), row(ln_b)]
    state_spec = pl.BlockSpec((nb, w, RWKV_HEAD), lambda j, i: (j, 0, 0))
    y, s_fin = pl.pallas_call(
        functools.partial(_rwkv_kernel, steps=steps, nsq=nsq),
        grid=(b // nb, t // steps),
        in_specs=[pl.BlockSpec((nb, steps, 4 * w), lambda j, i: (j, i, 0)),
                  pl.BlockSpec((nb, 1, 4 * w), lambda j, i: (j, 0, 0)), state_spec]
                 + [_full(x.shape) for x in weights],
        out_specs=[pl.BlockSpec((nb, steps, w), lambda j, i: (j, i, 0)), state_spec],
        out_shape=[jax.ShapeDtypeStruct((b, t, w), f32), jax.ShapeDtypeStruct((b, w, RWKV_HEAD), f32)],
        scratch_shapes=[pltpu.VMEM((nb, 1, 4 * w), f32),
                        pltpu.VMEM((nb * (w // HEAD_PAIR), HEAD_PAIR, HEAD_PAIR), f32)],
        compiler_params=_params("arbitrary", "arbitrary"),
        name="rwkv7_mixer",
    )(prk.reshape(b, t, 4 * w), prev.reshape(b, 1, 4 * w).astype(f32), s0.reshape(b, w, RWKV_HEAD).astype(f32),
      *weights)
    return y.reshape(b * t, w), s_fin.reshape(b, w // RWKV_HEAD, RWKV_HEAD, RWKV_HEAD)


def _outproj_kernel(x_ref, s5_ref, rw_ref, wo_ref, g_ref, wq_ref, x1_ref, q_ref, *, s5w):
    x1 = x_ref[...] + _dot(s5_ref[...], wo_ref[:s5w, :]) + _dot(rw_ref[...], wo_ref[s5w:, :])
    x1_ref[...] = x1
    q_ref[...] = _dot(_rms(x1, g_ref[...]), wq_ref[...]).astype(bf16)


def _outproj(x, s5y, rwy, w_out, g_ca, w_cq):
    rows, d = x.shape
    s5w = s5y.shape[1]
    rww = rwy.shape[1]
    tm = _tile(rows)
    rt = lambda c: pl.BlockSpec((tm, c), lambda i: (i, 0))
    return pl.pallas_call(
        functools.partial(_outproj_kernel, s5w=s5w),
        grid=(rows // tm,),
        in_specs=[rt(d), rt(s5w), rt(rww), _full(w_out.shape), _full((1, d)), _full(w_cq.shape)],
        out_specs=[rt(d), rt(d)],
        out_shape=[jax.ShapeDtypeStruct((rows, d), f32), jax.ShapeDtypeStruct((rows, d), bf16)],
        compiler_params=_params("parallel"),
        name="outproj_q",
    )(x, s5y, rwy, w_out, g_ca.reshape(1, d), w_cq)


def _kv_to_head_tiles(x):
    b, m, h, hd = x.shape
    return x.reshape(b, m, h, hd // LANES, LANES).transpose(0, 1, 3, 2, 4).reshape(b * m * (hd // LANES) * h, LANES)


def _kv_from_head_tiles(x, b, m, h, hd):
    return x.reshape(b, m, hd // LANES, h, LANES).transpose(0, 1, 3, 2, 4).reshape(b, m, h, hd)


def _memkv_kernel(m_ref, g_ref, w_ref, k_ref, v_ref):
    tm, d = m_ref.shape
    hd = d // MEM_HEADS
    parts = hd // LANES
    per_token = parts * MEM_HEADS
    nb = _rms(m_ref[...], g_ref[...]).astype(bf16)
    for ref, cols in ((k_ref, slice(0, d)), (v_ref, slice(d, 2 * d))):
        kv = jnp.dot(nb, w_ref[:, cols], preferred_element_type=f32)
        for h in range(MEM_HEADS):
            for c in range(parts):
                lo = h * hd + c * LANES
                ref[pl.ds(c * MEM_HEADS + h, tm, stride=per_token), :] = kv[:, lo:lo + LANES]


def _memkv(mem, g, w):
    rows, d = mem.shape
    tm = _tile(rows)
    per_token = d // LANES
    out = pl.BlockSpec((tm * per_token, LANES), lambda i: (i, 0))
    return pl.pallas_call(
        _memkv_kernel,
        grid=(rows // tm,),
        in_specs=[pl.BlockSpec((tm, d), lambda i: (i, 0)), _full((1, d)), _full(w.shape)],
        out_specs=[out, out],
        out_shape=[jax.ShapeDtypeStruct((rows * per_token, LANES), f32)] * 2,
        compiler_params=_params("parallel"),
        name="memory_kv",
    )(mem, g.reshape(1, d), w)


def _attn_kernel(q_ref, x1_ref, k_ref, v_ref, wo_ref, x2_ref, *, nbb):
    d = q_ref.shape[-1]
    hd = d // MEM_HEADS
    parts = hd // LANES
    per_token = parts * MEM_HEADS
    n_mem = k_ref.shape[0] // (per_token * nbb)
    rows = q_ref.shape[0] // nbb
    scale = hd ** -0.5
    q = q_ref[...]

    def head(ref, bb, h):
        first = bb * n_mem * per_token
        return jnp.concatenate([ref[pl.ds(first + c * MEM_HEADS + h, n_mem, stride=per_token), :].astype(bf16)
                                for c in range(parts)], axis=-1)

    heads = []
    for h in range(MEM_HEADS):
        s = [_dg(q[bb * rows:(bb + 1) * rows, h * hd:(h + 1) * hd], head(k_ref, bb, h), _NT) * scale
             for bb in range(nbb)]
        e = [jnp.exp(x - jnp.max(x, axis=-1, keepdims=True)) for x in s]
        prob = [x / jnp.sum(x, axis=-1, keepdims=True) for x in e]
        o = [_dot(prob[bb], head(v_ref, bb, h)) for bb in range(nbb)]
        heads.append(jnp.concatenate(o, axis=0) if nbb > 1 else o[0])
    o = jnp.concatenate(heads, axis=-1)
    x2_ref[...] = x1_ref[...] + _dot(o, wo_ref[...])


def _attn(q, x1, mk, mv, w_co, t, b):
    d = q.shape[-1]
    rows_per_b = mk.shape[0] // b
    tt = min(ROW_TILE, t)
    assert t % tt == 0
    nt = t // tt
    nbb = min(ATTN_SHORT_BATCH, b) if (nt == 1 and tt * ATTN_SHORT_BATCH <= ROW_TILE) else 1
    assert b % nbb == 0
    xt = pl.BlockSpec((nbb * tt, d), lambda j, i: (j * nt + i, 0))
    kv = pl.BlockSpec((nbb * rows_per_b, LANES), lambda j, i: (j, 0))
    return pl.pallas_call(
        functools.partial(_attn_kernel, nbb=nbb),
        grid=(b // nbb, nt),
        in_specs=[xt, xt, kv, kv, _full(w_co.shape)],
        out_specs=xt,
        out_shape=jax.ShapeDtypeStruct((b * t, d), f32),
        compiler_params=_params("parallel", "parallel"),
        name="mem_attention",
    )(q, x1, mk, mv, w_co)


def _two_group_specs(tm, cols, na_tiles):
    a = pl.BlockSpec((tm, cols), lambda i, *_: (jnp.minimum(i, na_tiles - 1), 0))
    b = pl.BlockSpec((tm, cols), lambda i, *_: (jnp.maximum(i - na_tiles, 0), 0))
    return a, b


def _split2(x):
    hi = x.astype(bf16)
    return hi, (x - hi.astype(f32)).astype(bf16)


def _token_chunk(j, rows, nchunk, first=0):
    return pl.ds(first * nchunk + j, rows, stride=nchunk)


def _to_token_tiles(ref, x, first=0):
    rows, d = x.shape
    nchunk = d // LANES
    for j in range(nchunk):
        ref[_token_chunk(j, rows, nchunk, first), :] = x[:, j * LANES:(j + 1) * LANES]


def _token_rows(t, nchunk):
    return pl.ds(pl.multiple_of(t * nchunk, nchunk), nchunk)


def _router_kernel(xa_ref, xb_ref, g_ref, rw_ref, rb_ref, ht_ref, e_ref, rank_ref, gate_ref, cnt_ref,
                   *, n_exp, na_tiles):
    i = pl.program_id(0)
    tm, d = xa_ref.shape

    @pl.when(i == 0)
    def _():
        cnt_ref[...] = jnp.zeros_like(cnt_ref)

    h = _rms(jnp.where(i < na_tiles, xa_ref[...], xb_ref[...]), g_ref[...])
    _to_token_tiles(ht_ref, h)

    hh, hl = _split2(h)
    wh, wl = _split2(rw_ref[...])
    logits = (jnp.dot(hh, wh, preferred_element_type=f32) + jnp.dot(hh, wl, preferred_element_type=f32)
              + jnp.dot(hl, wh, preferred_element_type=f32) + rb_ref[...])

    lane = lax.broadcasted_iota(jnp.int32, (tm, n_exp), 1)
    work = logits
    sels, tops = [], []
    for _ in range(TOP_K):
        m = jnp.max(work, axis=-1, keepdims=True)
        idx = jnp.min(jnp.where(work == m, lane, n_exp), axis=-1, keepdims=True)
        sel = lane == idx
        sels.append(sel)
        tops.append(m)
        work = jnp.where(sel, -jnp.inf, work)
    chosen = sels[0] | sels[1] | sels[2] | sels[3]

    onehot = jnp.where(chosen, 1.0, 0.0)
    rr = lax.broadcasted_iota(jnp.int32, (tm, tm), 0)
    cc = lax.broadcasted_iota(jnp.int32, (tm, tm), 1)
    before = (cc < rr).astype(bf16)
    rank = jnp.dot(before, onehot.astype(bf16), preferred_element_type=f32) + cnt_ref[...]
    cnt_ref[...] = cnt_ref[...] + jnp.sum(onehot, axis=0, keepdims=True)

    den = sum(jnp.exp(t - tops[0]) for t in tops)
    e_cols, r_cols, g_cols = [], [], []
    for sel, top in zip(sels, tops):
        e_cols.append(jnp.sum(jnp.where(sel, lane, 0), axis=-1, keepdims=True))
        r_cols.append(jnp.sum(jnp.where(sel, rank, 0.0), axis=-1, keepdims=True))
        g_cols.append(jnp.exp(top - tops[0]) / den)
    e_ref[...] = jnp.concatenate(e_cols, axis=-1)
    rank_ref[...] = jnp.concatenate(r_cols, axis=-1).astype(jnp.int32)
    gate_ref[...] = jnp.concatenate(g_cols, axis=-1)


def _router(xa, xb, g, rw, rb):
    d = xa.shape[1]
    n = xa.shape[0] + xb.shape[0]
    n_exp = rw.shape[1]
    tm = ROW_TILE
    assert xa.shape[0] % tm == 0 and xb.shape[0] % tm == 0
    na_tiles = xa.shape[0] // tm
    sa, sb = _two_group_specs(tm, d, na_tiles)
    rt = lambda c: pl.BlockSpec((tm, c), lambda i: (i, 0))
    return pl.pallas_call(
        functools.partial(_router_kernel, n_exp=n_exp, na_tiles=na_tiles),
        grid=(n // tm,),
        in_specs=[sa, sb, _full((1, d)), _full((d, n_exp)), _full((1, n_exp))],
        out_specs=[pl.BlockSpec((tm * (d // LANES), LANES), lambda i: (i, 0)), rt(TOP_K), rt(TOP_K), rt(TOP_K),
                   _full((1, n_exp))],
        out_shape=[jax.ShapeDtypeStruct((n * (d // LANES), LANES), f32), jax.ShapeDtypeStruct((n, TOP_K), jnp.int32),
                   jax.ShapeDtypeStruct((n, TOP_K), jnp.int32), jax.ShapeDtypeStruct((n, TOP_K), f32),
                   jax.ShapeDtypeStruct((1, n_exp), f32)],
        compiler_params=_params("arbitrary"),
        name="moe_router",
    )(xa, xb, g.reshape(1, d), rw.astype(f32), rb.reshape(1, n_exp).astype(f32))


def _invert_kernel(dest_ref, zs_ref, ze_ref, tok_ref, *, n):
    def for_each_unmapped_row(fn):
        def segment(seg, c):
            return lax.fori_loop(zs_ref[seg], ze_ref[seg], fn, c)
        lax.fori_loop(0, zs_ref.shape[0], segment, 0)

    fill = lambda r, c: _set(tok_ref, r, 0, c)
    for_each_unmapped_row(fill)
    for kk in range(TOP_K):
        def place(t, c, kk=kk):
            return _set(tok_ref, dest_ref[kk * n + t], t, c)
        lax.fori_loop(0, n, place, 0, unroll=8)


def _set(ref, idx, val, carry):
    ref[idx] = val
    return carry


def _invert(dest_km, zero_start, zero_end, n, n_rows):
    grid_spec = pltpu.PrefetchScalarGridSpec(
        num_scalar_prefetch=3, grid=(1,), in_specs=[],
        out_specs=pl.BlockSpec(memory_space=pltpu.SMEM))
    return pl.pallas_call(
        functools.partial(_invert_kernel, n=n),
        grid_spec=grid_spec,
        out_shape=jax.ShapeDtypeStruct((n_rows,), jnp.int32),
        compiler_params=_params("arbitrary"),
        name="moe_invert",
    )(dest_km, zero_start, zero_end)


def _expert_kernel(be_ref, nv_ref, nxt_ref, tok_ref, ht_hbm, wgu_hbm, bgu_ref, wd_hbm, bd_ref, y_ref,
                   xbuf, xsem, wgu_f, wd_f, wsem, wgu_b, wd_b, *, layer, blk):
    i = pl.program_id(0)
    slot = i % 2
    nvalid = nv_ref[0]
    dff = wd_b.shape[0]
    nchunk = wgu_b.shape[0] // LANES
    last_block = pl.num_programs(0) - 1

    def gather(block, into):
        for r in range(blk):
            tok = tok_ref[block * blk + r]
            pltpu.make_async_copy(ht_hbm.at[_token_rows(tok, nchunk)], xbuf.at[into, _token_rows(r, nchunk)],
                                  xsem.at[into]).start(priority=0)

    def weight_copies(e):
        return (pltpu.make_async_copy(wgu_hbm.at[layer, e], wgu_f, wsem.at[0]),
                pltpu.make_async_copy(wd_hbm.at[layer, e], wd_f, wsem.at[1]))

    @pl.when(i == 0)
    def _():
        gather(0, 0)
        for cp in weight_copies(be_ref[0]):
            cp.start(priority=1)

    @pl.when((i < nvalid) & ((i == 0) | (be_ref[i] != be_ref[jnp.maximum(i - 1, 0)])))
    def _():
        for cp in weight_copies(be_ref[i]):
            cp.wait()
        wgu_b[...] = wgu_f[...].astype(bf16)
        wd_b[...] = wd_f[...].astype(bf16)

        @pl.when(nxt_ref[i] >= 0)
        def _():
            for cp in weight_copies(nxt_ref[i]):
                cp.start(priority=1)

    @pl.when(i < nvalid)
    def _():
        pltpu.make_async_copy(xbuf.at[slot], xbuf.at[slot], xsem.at[slot]).wait()
        gather(jnp.minimum(i + 1, last_block), 1 - slot)
        x = jnp.concatenate([xbuf[slot, _token_chunk(j, blk, nchunk), :].astype(bf16) for j in range(nchunk)],
                            axis=-1)
        hgu = jnp.dot(x, wgu_b[...], preferred_element_type=f32) + bgu_ref[0, 0]
        hg = jnp.minimum(hgu[:, :dff], SWIGLU_LIMIT)
        hu = jnp.clip(hgu[:, dff:], -SWIGLU_LIMIT, SWIGLU_LIMIT)
        act = (hu + 1.0) * hg * jax.nn.sigmoid(SWIGLU_ALPHA * hg)
        _to_token_tiles(y_ref, _dot(act, wd_b[...]) + bd_ref[0, 0])

    @pl.when(i >= nvalid)
    def _():
        y_ref[...] = jnp.zeros_like(y_ref)

    @pl.when(i == nvalid)
    def _():
        pltpu.make_async_copy(xbuf.at[slot], xbuf.at[slot], xsem.at[slot]).wait()


def _experts(ht, row_tok, block_e, nvalid, next_e, w_gu, b_gu, w_down, b_down, l, blk):
    depth, n_exp, d, dff2 = w_gu.shape
    dff = dff2 // 2
    nchunk = d // LANES
    n_rows = row_tok.shape[0]
    bspec = lambda cols: pl.BlockSpec((1, 1, 1, cols), lambda i, be, *_: (l, be[i], 0, 0))
    grid_spec = pltpu.PrefetchScalarGridSpec(
        num_scalar_prefetch=4,
        grid=(n_rows // blk,),
        in_specs=[pl.BlockSpec(memory_space=pl.ANY), pl.BlockSpec(memory_space=pl.ANY), bspec(dff2),
                  pl.BlockSpec(memory_space=pl.ANY), bspec(d)],
        out_specs=pl.BlockSpec((blk * nchunk, LANES), lambda i, *_: (i, 0)),
        scratch_shapes=[pltpu.VMEM((2, blk * nchunk, LANES), f32), pltpu.SemaphoreType.DMA((2,)),
                        pltpu.VMEM((d, dff2), f32), pltpu.VMEM((dff, d), f32), pltpu.SemaphoreType.DMA((2,)),
                        pltpu.VMEM((d, dff2), bf16), pltpu.VMEM((dff, d), bf16)],
    )
    return pl.pallas_call(
        functools.partial(_expert_kernel, layer=l, blk=blk),
        grid_spec=grid_spec,
        out_shape=jax.ShapeDtypeStruct((n_rows * nchunk, LANES), f32),
        compiler_params=_params("arbitrary"),
        name="moe_experts",
    )(block_e, nvalid, next_e, row_tok, ht, w_gu, b_gu.reshape(depth, n_exp, 1, dff2), w_down,
      b_down.reshape(depth, n_exp, 1, d))


def _combine_kernel(dest_ref, xa_ref, xb_ref, gate_ref, yb_hbm, g_ref, oa_ref, ob_ref, ybuf, sem,
                    *, tm, n, na_tiles, final):
    i = pl.program_id(0)
    slot = i % 2
    nchunk = ybuf.shape[2] // tm

    last_tile = pl.num_programs(0) - 1

    def copy_row(tile, into, kk, t, prio):
        row = dest_ref[kk * n + tile * tm + t]
        pltpu.make_async_copy(yb_hbm.at[_token_rows(row, nchunk)], ybuf.at[into, kk, _token_rows(t, nchunk)],
                              sem.at[into]).start(priority=prio)

    @pl.when(i == 0)
    def _():
        for kk in range(TOP_K):
            def issue(pair, c, kk=kk):
                for prio in range(DMA_PRIORITIES):
                    copy_row(0, 0, kk, pair * DMA_PRIORITIES + prio, prio)
                return c
            lax.fori_loop(0, tm // DMA_PRIORITIES, issue, 0, unroll=4)

    pltpu.make_async_copy(ybuf.at[slot], ybuf.at[slot], sem.at[slot]).wait()
    nxt = jnp.minimum(i + 1, last_tile)
    for kk in range(TOP_K):
        for t in range(tm):
            copy_row(nxt, 1 - slot, kk, t, t % DMA_PRIORITIES)
    gate = gate_ref[...]
    chunks = []
    for j in range(nchunk):
        acc = gate[:, 0:1] * ybuf[slot, 0, _token_chunk(j, tm, nchunk), :]
        for kk in range(1, TOP_K):
            acc = acc + gate[:, kk:kk + 1] * ybuf[slot, kk, _token_chunk(j, tm, nchunk), :]
        chunks.append(acc)
    x3 = jnp.where(i < na_tiles, xa_ref[...], xb_ref[...]) + jnp.concatenate(chunks, axis=-1)
    out = _rms(x3, g_ref[...]) if final else x3

    @pl.when(i < na_tiles)
    def _():
        oa_ref[...] = out

    @pl.when(i >= na_tiles)
    def _():
        ob_ref[...] = out

    @pl.when(i == last_tile)
    def _():
        pltpu.make_async_copy(ybuf.at[1 - slot], ybuf.at[1 - slot], sem.at[1 - slot]).wait()


def _combine(dest_km, xa, xb, gate, yb, g_final, final):
    d = xa.shape[1]
    n = xa.shape[0] + xb.shape[0]
    tm = COMBINE_TILE
    assert xa.shape[0] % tm == 0 and xb.shape[0] % tm == 0
    na_tiles = xa.shape[0] // tm
    sa, sb = _two_group_specs(tm, d, na_tiles)
    grid_spec = pltpu.PrefetchScalarGridSpec(
        num_scalar_prefetch=1,
        grid=(n // tm,),
        in_specs=[sa, sb, pl.BlockSpec((tm, TOP_K), lambda i, dst: (i, 0)), pl.BlockSpec(memory_space=pl.ANY),
                  pl.BlockSpec((1, d), lambda i, dst: (0, 0))],
        out_specs=[sa, sb],
        scratch_shapes=[pltpu.VMEM((2, TOP_K, tm * (d // LANES), LANES), f32), pltpu.SemaphoreType.DMA((2,))],
    )
    return pl.pallas_call(
        functools.partial(_combine_kernel, tm=tm, n=n, na_tiles=na_tiles, final=final),
        grid_spec=grid_spec,
        out_shape=[jax.ShapeDtypeStruct(xa.shape, f32), jax.ShapeDtypeStruct(xb.shape, f32)],
        compiler_params=_params("arbitrary"),
        name="moe_combine",
    )(dest_km, xa, xb, gate, yb, g_final.reshape(1, d))


def _moe(xa, xb, g_ffn, router_w, router_b, w_gu, b_gu, w_down, b_down, l, g_out, final):
    n = xa.shape[0] + xb.shape[0]
    n_exp = router_w.shape[1]
    blk = MOE_ROWS
    ht, top_e, top_rank, gate, counts = _router(xa, xb, g_ffn, router_w, router_b)

    counts = counts.reshape(n_exp).astype(jnp.int32)
    padded = (counts + blk - 1) // blk * blk
    pad_end = jnp.cumsum(padded)
    pad_start = pad_end - padded
    dest_km = (pad_start[top_e] + top_rank).T.reshape(-1)
    n_rows = -(-(n * TOP_K) // blk) * blk + n_exp * blk
    block_start = jnp.arange(n_rows // blk, dtype=jnp.int32) * blk
    block_e = jnp.minimum(jnp.sum(pad_end[None, :] <= block_start[:, None], axis=1), n_exp - 1).astype(jnp.int32)
    nvalid = (pad_end[-1:] // blk).astype(jnp.int32)
    zero_start = jnp.concatenate([pad_start + counts, pad_end[-1:]]).astype(jnp.int32)
    zero_end = jnp.concatenate([pad_end, jnp.full((1,), n_rows, jnp.int32)]).astype(jnp.int32)

    has_rows = padded > 0
    ids = jnp.arange(n_exp, dtype=jnp.int32)
    after = jnp.where(has_rows[None, :] & (ids[None, :] > ids[:, None]), ids[None, :], n_exp)
    next_of = jnp.min(after, axis=1)
    next_e = jnp.where(next_of < n_exp, next_of, -1).astype(jnp.int32)[block_e]

    row_tok = _invert(dest_km, zero_start, zero_end, n, n_rows)
    yb = _experts(ht, row_tok, block_e, nvalid, next_e, w_gu, b_gu, w_down, b_down, l, blk)
    return _combine(dest_km, xa, xb, gate, yb, g_out, final)


def _mixers(x, t, b, h0r, h0i, prev, s0, P, l):
    s5w = P['s5_d'].shape[-1]
    u, prk = _inproj(x, P['norm_mix_g'][l], P['w_in'][l].astype(bf16), s5w)
    ar, ai, bre, bim, cre_m, cim_m = _s5_discretize(P['s5_lambda_re'][l], P['s5_lambda_im'][l], P['s5_log_dt'][l],
                                                    P['s5_b_re'][l], P['s5_b_im'][l], P['s5_c_re'][l], P['s5_c_im'][l])
    s5y, hr, hi = _s5(jnp.transpose(u.reshape(b, t, s5w), (1, 0, 2)), h0r, h0i, ar, ai, bre, bim, cre_m, cim_m,
                      P['s5_d'][l].reshape(1, s5w).astype(f32), P['s5_glu_w'][l].astype(bf16),
                      P['s5_glu_b'][l].reshape(1, s5w).astype(f32))
    s5y = jnp.transpose(s5y, (1, 0, 2)).reshape(b * t, s5w)
    rwy, s_fin = _rwkv(prk, prev, s0, t, b, P['rwkv_mu'][l], P['rwkv_w0'][l], P['rwkv_w1'][l], P['rwkv_w2'][l],
                       P['rwkv_a0'][l], P['rwkv_a1'][l], P['rwkv_a2'][l], P['rwkv_g1'][l], P['rwkv_g2'][l],
                       P['rwkv_k_k'][l], P['rwkv_k_a'][l], P['rwkv_r_k'][l], P['rwkv_ln_w'][l], P['rwkv_ln_b'][l])
    x1, q = _outproj(x, s5y, rwy, P['w_out'][l].astype(bf16), P['norm_ca_g'][l], P['w_cq'][l].astype(bf16))
    shift_last = prk.reshape(b, t, -1)[:, t - 1]
    return x1, q, hr, hi, shift_last, s_fin


def kernel(x_prompt, x_sample, mem_prompt, state_s5_re, state_s5_im, state_rwkv_shift, state_rwkv_wkv, cache_mem_k, cache_mem_v, norm_mix_g, w_in, s5_lambda_re, s5_lambda_im, s5_log_dt, s5_b_re, s5_b_im, s5_c_re, s5_c_im, s5_d, s5_glu_w, s5_glu_b, rwkv_mu, rwkv_w0, rwkv_w1, rwkv_w2, rwkv_a0, rwkv_a1, rwkv_a2, rwkv_g1, rwkv_g2, rwkv_k_k, rwkv_k_a, rwkv_r_k, rwkv_ln_w, rwkv_ln_b, w_out, norm_ca_g, norm_mem_g, w_cq, w_mem_kv, w_co, norm_ffn_g, router_w, router_b, ex_w_gu, ex_b_gu, ex_w_down, ex_b_down, norm_final_g):
    P = dict(norm_mix_g=norm_mix_g, w_in=w_in, s5_lambda_re=s5_lambda_re, s5_lambda_im=s5_lambda_im,
             s5_log_dt=s5_log_dt, s5_b_re=s5_b_re, s5_b_im=s5_b_im, s5_c_re=s5_c_re, s5_c_im=s5_c_im,
             s5_d=s5_d, s5_glu_w=s5_glu_w, s5_glu_b=s5_glu_b, rwkv_mu=rwkv_mu, rwkv_w0=rwkv_w0,
             rwkv_w1=rwkv_w1, rwkv_w2=rwkv_w2, rwkv_a0=rwkv_a0, rwkv_a1=rwkv_a1, rwkv_a2=rwkv_a2,
             rwkv_g1=rwkv_g1, rwkv_g2=rwkv_g2, rwkv_k_k=rwkv_k_k, rwkv_k_a=rwkv_k_a, rwkv_r_k=rwkv_r_k,
             rwkv_ln_w=rwkv_ln_w, rwkv_ln_b=rwkv_ln_b, w_out=w_out, norm_ca_g=norm_ca_g, w_cq=w_cq)
    depth = norm_mix_g.shape[0]
    bp, tp, d = x_prompt.shape
    bs, ts, _ = x_sample.shape
    n_mem = mem_prompt.shape[1]
    s5_groups, s5_state = s5_lambda_re.shape[1:]
    nstate = s5_groups * s5_state
    rw_in = state_rwkv_shift.shape[-1]
    heads = state_rwkv_wkv.shape[2]
    mh = cache_mem_k.shape[3]
    assert mh == MEM_HEADS

    xp = x_prompt.reshape(bp * tp, d)
    xs = x_sample.reshape(bs * ts, d)
    outs = {k: [] for k in ('p_re', 'p_im', 'p_sh', 'p_wkv', 'p_mk', 'p_mv', 's_re', 's_im', 's_sh', 's_wkv')}
    for l in range(depth):
        mk, mv = _memkv(mem_prompt.reshape(bp * n_mem, d), norm_mem_g[l], w_mem_kv[l].astype(bf16))
        zeros_state = jnp.zeros((bp, nstate), f32)
        x1p, qp, re_p, im_p, sh_p, wkv_p = _mixers(
            xp, tp, bp, zeros_state, zeros_state, jnp.zeros((bp, rw_in), f32),
            jnp.zeros((bp, heads, RWKV_HEAD, RWKV_HEAD), f32), P, l)
        x1s, qs, re_s, im_s, sh_s, wkv_s = _mixers(
            xs, ts, bs, state_s5_re[l].reshape(bs, nstate), state_s5_im[l].reshape(bs, nstate),
            state_rwkv_shift[l], state_rwkv_wkv[l], P, l)
        wco = w_co[l].astype(bf16)
        x2p = _attn(qp, x1p, mk, mv, wco, tp, bp)
        x2s = _attn(qs, x1s, _kv_to_head_tiles(cache_mem_k[l]), _kv_to_head_tiles(cache_mem_v[l]), wco, ts, bs)
        last = l == depth - 1
        xp, xs = _moe(x2p, x2s, norm_ffn_g[l], router_w[l], router_b[l], ex_w_gu, ex_b_gu, ex_w_down, ex_b_down,
                      l, norm_final_g, last)
        outs['p_re'].append(re_p.reshape(bp, s5_groups, s5_state))
        outs['p_im'].append(im_p.reshape(bp, s5_groups, s5_state))
        outs['p_sh'].append(sh_p)
        outs['p_wkv'].append(wkv_p)
        outs['p_mk'].append(_kv_from_head_tiles(mk, bp, n_mem, mh, d // mh))
        outs['p_mv'].append(_kv_from_head_tiles(mv, bp, n_mem, mh, d // mh))
        outs['s_re'].append(re_s.reshape(bs, s5_groups, s5_state))
        outs['s_im'].append(im_s.reshape(bs, s5_groups, s5_state))
        outs['s_sh'].append(sh_s)
        outs['s_wkv'].append(wkv_s)
    st = lambda k: jnp.stack(outs[k])
    return (xp.reshape(bp, tp, d), xs.reshape(bs, ts, d), st('p_re'), st('p_im'), st('p_sh'), st('p_wkv'),
            st('p_mk'), st('p_mv'), st('s_re'), st('s_im'), st('s_sh'), st('s_wkv'))
```

```python
import functools

import jax
import jax.numpy as jnp
from jax import lax
from jax.experimental import pallas as pl
from jax.experimental.pallas import tpu as pltpu

f32 = jnp.float32
bf16 = jnp.bfloat16

RWKV_HEAD = 64
HEAD_PAIR = 2 * RWKV_HEAD
DECAY_SCALE = 0.6065306597
MEM_HEADS = 4
TOP_K = 4
SWIGLU_LIMIT = 7.0
SWIGLU_ALPHA = 1.702
RMS_EPS = 1e-5
GN_EPS = 64e-5

V7X_VMEM_LIMIT = 56 * 1024 * 1024
ROW_TILE = 512
MOE_ROWS = 256
COMBINE_TILE = 256
WKV_CHUNK = 64
WKV_BATCH = 2
S5_CHUNK = 64
S5_BLOCK_CHANNELS = 128
LANES = 128
DMA_PRIORITIES = 2
ATTN_SHORT_BATCH = 4

_NT = (((1,), (1,)), ((), ()))
_TN = (((0,), (0,)), ((), ()))


def _params(*sem):
    return pltpu.CompilerParams(dimension_semantics=sem, vmem_limit_bytes=V7X_VMEM_LIMIT)


def _dot(a, b):
    return jnp.dot(a.astype(bf16), b.astype(bf16), preferred_element_type=f32)


def _dg(a, b, dims):
    return lax.dot_general(a.astype(bf16), b.astype(bf16), dims, preferred_element_type=f32)


def _rms(x, g):
    return x * lax.rsqrt(jnp.mean(x * x, axis=-1, keepdims=True) + RMS_EPS) * g


def _full(shape):
    n = len(shape)
    return pl.BlockSpec(shape, lambda *_: (0,) * n)


def _tile(rows, tile=ROW_TILE):
    t = min(tile, rows)
    assert rows % t == 0
    return t


def _inproj_kernel(x_ref, g_ref, w_ref, u_ref, p_ref, *, s5w):
    nb = _rms(x_ref[...], g_ref[...]).astype(bf16)
    u_ref[...] = jnp.dot(nb, w_ref[:, :s5w], preferred_element_type=f32)
    p_ref[...] = jnp.dot(nb, w_ref[:, s5w:], preferred_element_type=f32)


def _inproj(x, g, w, s5w):
    rows, d = x.shape
    inw = w.shape[1]
    tm = _tile(rows)
    return pl.pallas_call(
        functools.partial(_inproj_kernel, s5w=s5w),
        grid=(rows // tm,),
        in_specs=[pl.BlockSpec((tm, d), lambda i: (i, 0)), _full((1, d)), _full((d, inw))],
        out_specs=[pl.BlockSpec((tm, s5w), lambda i: (i, 0)), pl.BlockSpec((tm, inw - s5w), lambda i: (i, 0))],
        out_shape=[jax.ShapeDtypeStruct((rows, s5w), f32), jax.ShapeDtypeStruct((rows, inw - s5w), f32)],
        compiler_params=_params("parallel"),
        name="inproj",
    )(x, g.reshape(1, d), w)


def _s5_kernel(u_ref, h0r_ref, h0i_ref, ar_ref, ai_ref, bre_ref, bim_ref, cre_ref, cim_ref, d_ref, gw_ref, gb_ref,
               y_ref, hr_ref, hi_ref, xr_s, xi_s, *, steps, bt, lane_chunk):
    i = pl.program_id(1)
    width = u_ref.shape[-1]
    nstate = ar_ref.shape[-1]

    @pl.when(i == 0)
    def _():
        hr_ref[...] = h0r_ref[...]
        hi_ref[...] = h0i_ref[...]

    u = u_ref[...].reshape(steps * bt, width)
    ub = u.astype(bf16)
    nblk, cb, sb = bre_ref.shape
    for j in range(nblk):
        uj = ub[:, j * cb:(j + 1) * cb]
        xr_s[:, j * sb:(j + 1) * sb] = jnp.dot(uj, bre_ref[j], preferred_element_type=f32)
        xi_s[:, j * sb:(j + 1) * sb] = jnp.dot(uj, bim_ref[j], preferred_element_type=f32)

    for c in range(nstate // lane_chunk):
        sl = slice(c * lane_chunk, (c + 1) * lane_chunk)
        arb = jnp.broadcast_to(ar_ref[:, sl], (bt, lane_chunk))
        aib = jnp.broadcast_to(ai_ref[:, sl], (bt, lane_chunk))

        def step(t, carry, sl=sl, arb=arb, aib=aib):
            hr, hi = carry
            r0 = pl.multiple_of(t * bt, bt)
            nhr = arb * hr - aib * hi + xr_s[pl.ds(r0, bt), sl]
            nhi = arb * hi + aib * hr + xi_s[pl.ds(r0, bt), sl]
            xr_s[pl.ds(r0, bt), sl] = nhr
            xi_s[pl.ds(r0, bt), sl] = nhi
            return nhr, nhi

        hr, hi = lax.fori_loop(0, steps, step, (hr_ref[:, sl], hi_ref[:, sl]), unroll=2)
        hr_ref[:, sl] = hr
        hi_ref[:, sl] = hi

    y = jnp.concatenate(
        [jnp.dot(xr_s[:, j * sb:(j + 1) * sb].astype(bf16), cre_ref[j], preferred_element_type=f32)
         + jnp.dot(xi_s[:, j * sb:(j + 1) * sb].astype(bf16), cim_ref[j], preferred_element_type=f32)
         for j in range(nblk)], axis=-1) + d_ref[...] * u
    z = jax.nn.gelu(y)
    out = z * jax.nn.sigmoid(_dot(z, gw_ref[...]) + gb_ref[...])
    y_ref[...] = out.reshape(steps, bt, width)


def _s5(u3, h0r, h0i, ar, ai, bre, bim, cre_m, cim_m, d, gw, gb):
    t, b, width = u3.shape
    nstate = ar.shape[-1]
    bt = 8
    steps = min(S5_CHUNK, t)
    assert b % bt == 0 and t % steps == 0
    kern = functools.partial(_s5_kernel, steps=steps, bt=bt, lane_chunk=512)
    st_spec = pl.BlockSpec((bt, nstate), lambda j, i: (j, 0))
    y, hr, hi = pl.pallas_call(
        kern,
        grid=(b // bt, t // steps),
        in_specs=[pl.BlockSpec((steps, bt, width), lambda j, i: (i, j, 0)), st_spec, st_spec,
                  _full((1, nstate)), _full((1, nstate)), _full(bre.shape), _full(bim.shape),
                  _full(cre_m.shape), _full(cim_m.shape), _full((1, width)), _full((width, width)),
                  _full((1, width))],
        out_specs=[pl.BlockSpec((steps, bt, width), lambda j, i: (i, j, 0)), st_spec, st_spec],
        out_shape=[jax.ShapeDtypeStruct((t, b, width), f32), jax.ShapeDtypeStruct((b, nstate), f32),
                   jax.ShapeDtypeStruct((b, nstate), f32)],
        scratch_shapes=[pltpu.VMEM((steps * bt, nstate), f32), pltpu.VMEM((steps * bt, nstate), f32)],
        compiler_params=_params("arbitrary", "arbitrary"),
        name="s5_mixer",
    )(u3, h0r, h0i, ar, ai, bre, bim, cre_m, cim_m, d, gw, gb)
    return y, hr, hi


def _s5_discretize(lam_re, lam_im, log_dt, b_re, b_im, c_re, c_im):
    g, p = lam_re.shape
    dt = jnp.exp(log_dt.astype(f32))[:, None]
    lr = lam_re.astype(f32)
    li = lam_im.astype(f32)
    mag = jnp.exp(lr * dt)
    ar = mag * jnp.cos(li * dt)
    ai = mag * jnp.sin(li * dt)
    den = lr * lr + li * li
    cr = ((ar - 1.0) * lr + ai * li) / den
    ci = (ai * lr - (ar - 1.0) * li) / den
    br = b_re.astype(f32)
    bi = b_im.astype(f32)
    bbar_r = cr[..., None] * br - ci[..., None] * bi
    bbar_i = cr[..., None] * bi + ci[..., None] * br
    c = br.shape[-1]
    gb = S5_BLOCK_CHANNELS // c
    assert g % gb == 0
    eye = jnp.eye(gb, dtype=f32)
    nblk = g // gb
    bd_in = lambda m: jnp.einsum('ngpc,gh->ngchp', m.reshape(nblk, gb, p, c), eye).reshape(
        nblk, gb * c, gb * p).astype(bf16)
    bd_out = lambda m: jnp.einsum('ngcp,gh->ngphc', m.reshape(nblk, gb, c, p), eye).reshape(
        nblk, gb * p, gb * c).astype(bf16)
    return (ar.reshape(1, g * p), ai.reshape(1, g * p), bd_in(bbar_r), bd_in(bbar_i),
            bd_out(c_re.astype(f32)), bd_out(-c_im.astype(f32)))


def _rwkv_kernel(p_ref, prev_ref, s0_ref, mu_ref, w0_ref, w1_ref, w2_ref, a0_ref, a1_ref, a2_ref, g1_ref, g2_ref,
                 kk_ref, ka_ref, rk_ref, lnw_ref, lnb_ref, y_ref, sout_ref, prev_s, st_s, *, steps, nsq):
    i = pl.program_id(1)
    L = steps
    L2 = 2 * L
    nb = p_ref.shape[0]
    w = p_ref.shape[-1] // 4
    npair = w // HEAD_PAIR

    row2 = lax.broadcasted_iota(jnp.int32, (L2, HEAD_PAIR), 0)
    lane2 = lax.broadcasted_iota(jnp.int32, (L2, HEAD_PAIR), 1)
    own = (row2 < L) == (lane2 < RWKV_HEAD)
    ri = lax.broadcasted_iota(jnp.int32, (L2, L2), 0)
    ci = lax.broadcasted_iota(jnp.int32, (L2, L2), 1)
    same = (ri < L) == (ci < L)
    strict = same & (ci < ri)
    incl = same & (ci <= ri)
    eye = (ri == ci).astype(f32)
    sr = lax.broadcasted_iota(jnp.int32, (HEAD_PAIR, HEAD_PAIR), 0)
    sc = lax.broadcasted_iota(jnp.int32, (HEAD_PAIR, HEAD_PAIR), 1)
    sdiag = (sr < RWKV_HEAD) == (sc < RWKV_HEAD)

    @pl.when(i == 0)
    def _():
        prev_s[...] = prev_ref[...]
        for bb in range(nb):
            for pr in range(npair):
                sp = s0_ref[bb, pr * HEAD_PAIR:(pr + 1) * HEAD_PAIR, :]
                st_s[bb * npair + pr] = jnp.where(sdiag, jnp.concatenate([sp, sp], axis=1), 0.0)

    trow = lax.broadcasted_iota(jnp.int32, (L, 1), 0)
    ps, deltas = [], []
    for bb in range(nb):
        pb = p_ref[bb]
        shifted = jnp.where(trow == 0, prev_s[bb], pltpu.roll(pb, 1, axis=0))
        prev_s[bb] = pb[L - 1:L, :]
        ps.append(pb)
        deltas.append(shifted - pb)
    p = jnp.concatenate(ps, axis=0) if nb > 1 else ps[0]
    delta = jnp.concatenate(deltas, axis=0) if nb > 1 else deltas[0]
    mu = mu_ref[...]
    pz = p[:, 3 * w:]
    dz = delta[:, 3 * w:]
    r = p[:, :w] + delta[:, :w] * mu[0:1]
    k = p[:, w:2 * w] + delta[:, w:2 * w] * mu[1:2]
    v = p[:, 2 * w:3 * w] + delta[:, 2 * w:3 * w] * mu[2:3]
    zw = pz + dz * mu[3:4]
    za = pz + dz * mu[4:5]
    zg = pz + dz * mu[5:6]
    logw = -DECAY_SCALE * jax.nn.sigmoid(w0_ref[...] + _dot(jnp.tanh(_dot(zw, w1_ref[...])), w2_ref[...]))
    a = jax.nn.sigmoid(a0_ref[...] + _dot(_dot(za, a1_ref[...]), a2_ref[...]))
    g = _dot(jax.nn.sigmoid(_dot(zg, g1_ref[...])), g2_ref[...])
    kkraw = k * kk_ref[...]
    kmod = k * (1.0 + (a - 1.0) * ka_ref[...])
    rk = r * kmod * rk_ref[...]

    tr = lax.broadcasted_iota(jnp.int32, (nb * L, nb * L), 0)
    tc = lax.broadcasted_iota(jnp.int32, (nb * L, nb * L), 1)
    tri = ((tc <= tr) & (tc >= (tr // L) * L)).astype(bf16)
    hi = logw.astype(bf16)
    rem = logw - hi.astype(f32)
    mid = rem.astype(bf16)
    lo = (rem - mid.astype(f32)).astype(bf16)
    cum = (jnp.dot(tri, hi, preferred_element_type=f32) + jnp.dot(tri, mid, preferred_element_type=f32)
           + jnp.dot(tri, lo, preferred_element_type=f32))

    stack = lambda x: jnp.concatenate([x, x], axis=0)
    chains = [(bb, pr) for bb in range(nb) for pr in range(npair)]
    nch = range(len(chains))
    rs = [slice(bb * L, (bb + 1) * L) for bb, _ in chains]
    sls = [slice(pr * HEAD_PAIR, (pr + 1) * HEAD_PAIR) for _, pr in chains]
    cut = lambda x, c: stack(x[rs[c], sls[c]])
    own_of = lambda x, c: jnp.where(own, cut(x, c), 0.0)

    wt, rm, km, bm, kap, vmf, vm, gram = [], [], [], [], [], [], [], []
    for c in nch:
        cum_c = cut(cum, c)
        wt_c = jnp.exp(cum_c)
        winv = jnp.exp(-cum_c)
        wprev = jnp.exp(cum_c - cut(logw, c))
        kkr = own_of(kkraw, c)
        kkn = kkr * lax.rsqrt(jnp.maximum(jnp.sum(kkr * kkr, axis=-1, keepdims=True), 1e-24))
        wt.append(wt_c)
        rm.append((own_of(r, c) * wt_c).astype(bf16))
        km.append((own_of(kmod, c) * winv).astype(bf16))
        bm.append((kkn * cut(a, c) * winv).astype(bf16))
        kap.append((kkn * wprev).astype(bf16))
        vmf.append(own_of(v, c))
        vm.append(vmf[c].astype(bf16))
        gram.append(_dg(jnp.concatenate([kap[c], rm[c]], axis=0), jnp.concatenate([bm[c], km[c]], axis=0), _NT))
    n_mat = [jnp.where(strict, x[:L2, :L2], 0.0) for x in gram]
    p_mat = [jnp.where(strict, x[:L2, L2:], 0.0) for x in gram]
    g_mat = [jnp.where(incl, x[L2:, :L2], 0.0) for x in gram]
    q_mat = [jnp.where(incl, x[L2:, L2:], 0.0) for x in gram]

    neg = [-x for x in n_mat]
    inv = [eye + x for x in neg]
    if nsq > 0:
        pw = [_dot(x, x) for x in neg]
        for _ in range(nsq - 1):
            both = [_dot(pw[c], jnp.concatenate([inv[c], pw[c]], axis=1)) for c in nch]
            inv = [inv[c] + both[c][:, :L2] for c in nch]
            pw = [x[:, L2:] for x in both]
        inv = [inv[c] + _dot(pw[c], inv[c]) for c in nch]

    s_old = [st_s[c] for c in nch]
    rhs = [_dg(kap[c], s_old[c], _NT) + _dot(p_mat[c], vm[c]) for c in nch]
    ym0 = [_dg(rm[c], s_old[c], _NT) for c in nch]
    um = [(-_dot(inv[c], rhs[c])).astype(bf16) for c in nch]
    vu = [jnp.concatenate([vm[c], um[c]], axis=0) for c in nch]
    ym = [ym0[c] + _dot(jnp.concatenate([q_mat[c], g_mat[c]], axis=1), vu[c]) for c in nch]
    for c in nch:
        st_s[c] = (s_old[c] + _dg(vu[c], jnp.concatenate([km[c], bm[c]], axis=0), _TN)) * wt[c][L - 1:L, :]

    for c, (bb, pr) in enumerate(chains):
        mean = jnp.sum(ym[c], axis=-1, keepdims=True) * (1.0 / RWKV_HEAD)
        yc = jnp.where(own, ym[c] - mean, 0.0)
        var = jnp.sum(yc * yc, axis=-1, keepdims=True) * (1.0 / RWKV_HEAD)
        yn = yc * lax.rsqrt(var + GN_EPS)
        bonus = jnp.sum(own_of(rk, c), axis=-1, keepdims=True) * vmf[c]
        mix = (yn[:L] + yn[L:]) * lnw_ref[:, sls[c]] + lnb_ref[:, sls[c]] + (bonus[:L] + bonus[L:])
        y_ref[bb, :, sls[c]] = mix * g[rs[c], sls[c]]

    @pl.when(i == pl.num_programs(1) - 1)
    def _():
        for c, (bb, pr) in enumerate(chains):
            s_new = st_s[c]
            sout_ref[bb, pr * HEAD_PAIR:(pr + 1) * HEAD_PAIR, :] = s_new[:, :RWKV_HEAD] + s_new[:, RWKV_HEAD:]


def _rwkv(prk, prev, s0, t, b, mu, w0, w1, w2, a0, a1, a2, g1, g2, k_k, k_a, r_k, ln_w, ln_b):
    w = prk.shape[-1] // 4
    steps = min(WKV_CHUNK, t)
    nb = min(WKV_BATCH * max(1, WKV_CHUNK // (4 * steps)), b)
    assert t % steps == 0 and w % HEAD_PAIR == 0 and b % nb == 0
    nsq = max((steps - 1).bit_length() - 1, 0)
    row = lambda x: x.reshape(1, -1).astype(f32)
    wb = lambda x: x.astype(bf16)
    weights = [mu.astype(f32), row(w0), wb(w1), wb(w2), row(a0), wb(a1), wb(a2), wb(g1), wb(g2),
               row(k_k), row(k_a), row(r_k), row(ln_w), row(ln_b)]
    state_spec = pl.BlockSpec((nb, w, RWKV_HEAD), lambda j, i: (j, 0, 0))
    y, s_fin = pl.pallas_call(
        functools.partial(_rwkv_kernel, steps=steps, nsq=nsq),
        grid=(b // nb, t // steps),
        in_specs=[pl.BlockSpec((nb, steps, 4 * w), lambda j, i: (j, i, 0)),
                  pl.BlockSpec((nb, 1, 4 * w), lambda j, i: (j, 0, 0)), state_spec]
                 + [_full(x.shape) for x in weights],
        out_specs=[pl.BlockSpec((nb, steps, w), lambda j, i: (j, i, 0)), state_spec],
        out_shape=[jax.ShapeDtypeStruct((b, t, w), f32), jax.ShapeDtypeStruct((b, w, RWKV_HEAD), f32)],
        scratch_shapes=[pltpu.VMEM((nb, 1, 4 * w), f32),
                        pltpu.VMEM((nb * (w // HEAD_PAIR), HEAD_PAIR, HEAD_PAIR), f32)],
        compiler_params=_params("arbitrary", "arbitrary"),
        name="rwkv7_mixer",
    )(prk.reshape(b, t, 4 * w), prev.reshape(b, 1, 4 * w).astype(f32), s0.reshape(b, w, RWKV_HEAD).astype(f32),
      *weights)
    return y.reshape(b * t, w), s_fin.reshape(b, w // RWKV_HEAD, RWKV_HEAD, RWKV_HEAD)


def _outproj_kernel(x_ref, s5_ref, rw_ref, wo_ref, g_ref, wq_ref, x1_ref, q_ref, *, s5w):
    x1 = x_ref[...] + _dot(s5_ref[...], wo_ref[:s5w, :]) + _dot(rw_ref[...], wo_ref[s5w:, :])
    x1_ref[...] = x1
    q_ref[...] = _dot(_rms(x1, g_ref[...]), wq_ref[...]).astype(bf16)


def _outproj(x, s5y, rwy, w_out, g_ca, w_cq):
    rows, d = x.shape
    s5w = s5y.shape[1]
    rww = rwy.shape[1]
    tm = _tile(rows)
    rt = lambda c: pl.BlockSpec((tm, c), lambda i: (i, 0))
    return pl.pallas_call(
        functools.partial(_outproj_kernel, s5w=s5w),
        grid=(rows // tm,),
        in_specs=[rt(d), rt(s5w), rt(rww), _full(w_out.shape), _full((1, d)), _full(w_cq.shape)],
        out_specs=[rt(d), rt(d)],
        out_shape=[jax.ShapeDtypeStruct((rows, d), f32), jax.ShapeDtypeStruct((rows, d), bf16)],
        compiler_params=_params("parallel"),
        name="outproj_q",
    )(x, s5y, rwy, w_out, g_ca.reshape(1, d), w_cq)


def _kv_to_head_tiles(x):
    b, m, h, hd = x.shape
    return x.reshape(b, m, h, hd // LANES, LANES).transpose(0, 1, 3, 2, 4).reshape(b * m * (hd // LANES) * h, LANES)


def _kv_from_head_tiles(x, b, m, h, hd):
    return x.reshape(b, m, hd // LANES, h, LANES).transpose(0, 1, 3, 2, 4).reshape(b, m, h, hd)


def _memkv_kernel(m_ref, g_ref, w_ref, k_ref, v_ref):
    tm, d = m_ref.shape
    hd = d // MEM_HEADS
    parts = hd // LANES
    per_token = parts * MEM_HEADS
    nb = _rms(m_ref[...], g_ref[...]).astype(bf16)
    for ref, cols in ((k_ref, slice(0, d)), (v_ref, slice(d, 2 * d))):
        kv = jnp.dot(nb, w_ref[:, cols], preferred_element_type=f32)
        for h in range(MEM_HEADS):
            for c in range(parts):
                lo = h * hd + c * LANES
                ref[pl.ds(c * MEM_HEADS + h, tm, stride=per_token), :] = kv[:, lo:lo + LANES]


def _memkv(mem, g, w):
    rows, d = mem.shape
    tm = _tile(rows)
    per_token = d // LANES
    out = pl.BlockSpec((tm * per_token, LANES), lambda i: (i, 0))
    return pl.pallas_call(
        _memkv_kernel,
        grid=(rows // tm,),
        in_specs=[pl.BlockSpec((tm, d), lambda i: (i, 0)), _full((1, d)), _full(w.shape)],
        out_specs=[out, out],
        out_shape=[jax.ShapeDtypeStruct((rows * per_token, LANES), f32)] * 2,
        compiler_params=_params("parallel"),
        name="memory_kv",
    )(mem, g.reshape(1, d), w)


def _attn_kernel(q_ref, x1_ref, k_ref, v_ref, wo_ref, x2_ref, *, nbb):
    d = q_ref.shape[-1]
    hd = d // MEM_HEADS
    parts = hd // LANES
    per_token = parts * MEM_HEADS
    n_mem = k_ref.shape[0] // (per_token * nbb)
    rows = q_ref.shape[0] // nbb
    scale = hd ** -0.5
    q = q_ref[...]

    def head(ref, bb, h):
        first = bb * n_mem * per_token
        return jnp.concatenate([ref[pl.ds(first + c * MEM_HEADS + h, n_mem, stride=per_token), :].astype(bf16)
                                for c in range(parts)], axis=-1)

    heads = []
    for h in range(MEM_HEADS):
        s = [_dg(q[bb * rows:(bb + 1) * rows, h * hd:(h + 1) * hd], head(k_ref, bb, h), _NT) * scale
             for bb in range(nbb)]
        e = [jnp.exp(x - jnp.max(x, axis=-1, keepdims=True)) for x in s]
        prob = [x / jnp.sum(x, axis=-1, keepdims=True) for x in e]
        o = [_dot(prob[bb], head(v_ref, bb, h)) for bb in range(nbb)]
        heads.append(jnp.concatenate(o, axis=0) if nbb > 1 else o[0])
    o = jnp.concatenate(heads, axis=-1)
    x2_ref[...] = x1_ref[...] + _dot(o, wo_ref[...])


def _attn(q, x1, mk, mv, w_co, t, b):
    d = q.shape[-1]
    rows_per_b = mk.shape[0] // b
    tt = min(ROW_TILE, t)
    assert t % tt == 0
    nt = t // tt
    nbb = min(ATTN_SHORT_BATCH, b) if (nt == 1 and tt * ATTN_SHORT_BATCH <= ROW_TILE) else 1
    assert b % nbb == 0
    xt = pl.BlockSpec((nbb * tt, d), lambda j, i: (j * nt + i, 0))
    kv = pl.BlockSpec((nbb * rows_per_b, LANES), lambda j, i: (j, 0))
    return pl.pallas_call(
        functools.partial(_attn_kernel, nbb=nbb),
        grid=(b // nbb, nt),
        in_specs=[xt, xt, kv, kv, _full(w_co.shape)],
        out_specs=xt,
        out_shape=jax.ShapeDtypeStruct((b * t, d), f32),
        compiler_params=_params("parallel", "parallel"),
        name="mem_attention",
    )(q, x1, mk, mv, w_co)


def _two_group_specs(tm, cols, na_tiles):
    a = pl.BlockSpec((tm, cols), lambda i, *_: (jnp.minimum(i, na_tiles - 1), 0))
    b = pl.BlockSpec((tm, cols), lambda i, *_: (jnp.maximum(i - na_tiles, 0), 0))
    return a, b


def _split2(x):
    hi = x.astype(bf16)
    return hi, (x - hi.astype(f32)).astype(bf16)


def _token_chunk(j, rows, nchunk, first=0):
    return pl.ds(first * nchunk + j, rows, stride=nchunk)


def _to_token_tiles(ref, x, first=0):
    rows, d = x.shape
    nchunk = d // LANES
    for j in range(nchunk):
        ref[_token_chunk(j, rows, nchunk, first), :] = x[:, j * LANES:(j + 1) * LANES]


def _token_rows(t, nchunk):
    return pl.ds(pl.multiple_of(t * nchunk, nchunk), nchunk)


def _router_kernel(xa_ref, xb_ref, g_ref, rw_ref, rb_ref, ht_ref, e_ref, rank_ref, gate_ref, cnt_ref,
                   *, n_exp, na_tiles):
    i = pl.program_id(0)
    tm, d = xa_ref.shape

    @pl.when(i == 0)
    def _():
        cnt_ref[...] = jnp.zeros_like(cnt_ref)

    h = _rms(jnp.where(i < na_tiles, xa_ref[...], xb_ref[...]), g_ref[...])
    _to_token_tiles(ht_ref, h)

    hh, hl = _split2(h)
    wh, wl = _split2(rw_ref[...])
    logits = (jnp.dot(hh, wh, preferred_element_type=f32) + jnp.dot(hh, wl, preferred_element_type=f32)
              + jnp.dot(hl, wh, preferred_element_type=f32) + rb_ref[...])

    lane = lax.broadcasted_iota(jnp.int32, (tm, n_exp), 1)
    work = logits
    sels, tops = [], []
    for _ in range(TOP_K):
        m = jnp.max(work, axis=-1, keepdims=True)
        idx = jnp.min(jnp.where(work == m, lane, n_exp), axis=-1, keepdims=True)
        sel = lane == idx
        sels.append(sel)
        tops.append(m)
        work = jnp.where(sel, -jnp.inf, work)
    chosen = sels[0] | sels[1] | sels[2] | sels[3]

    onehot = jnp.where(chosen, 1.0, 0.0)
    rr = lax.broadcasted_iota(jnp.int32, (tm, tm), 0)
    cc = lax.broadcasted_iota(jnp.int32, (tm, tm), 1)
    before = (cc < rr).astype(bf16)
    rank = jnp.dot(before, onehot.astype(bf16), preferred_element_type=f32) + cnt_ref[...]
    cnt_ref[...] = cnt_ref[...] + jnp.sum(onehot, axis=0, keepdims=True)

    den = sum(jnp.exp(t - tops[0]) for t in tops)
    e_cols, r_cols, g_cols = [], [], []
    for sel, top in zip(sels, tops):
        e_cols.append(jnp.sum(jnp.where(sel, lane, 0), axis=-1, keepdims=True))
        r_cols.append(jnp.sum(jnp.where(sel, rank, 0.0), axis=-1, keepdims=True))
        g_cols.append(jnp.exp(top - tops[0]) / den)
    e_ref[...] = jnp.concatenate(e_cols, axis=-1)
    rank_ref[...] = jnp.concatenate(r_cols, axis=-1).astype(jnp.int32)
    gate_ref[...] = jnp.concatenate(g_cols, axis=-1)


def _router(xa, xb, g, rw, rb):
    d = xa.shape[1]
    n = xa.shape[0] + xb.shape[0]
    n_exp = rw.shape[1]
    tm = ROW_TILE
    assert xa.shape[0] % tm == 0 and xb.shape[0] % tm == 0
    na_tiles = xa.shape[0] // tm
    sa, sb = _two_group_specs(tm, d, na_tiles)
    rt = lambda c: pl.BlockSpec((tm, c), lambda i: (i, 0))
    return pl.pallas_call(
        functools.partial(_router_kernel, n_exp=n_exp, na_tiles=na_tiles),
        grid=(n // tm,),
        in_specs=[sa, sb, _full((1, d)), _full((d, n_exp)), _full((1, n_exp))],
        out_specs=[pl.BlockSpec((tm * (d // LANES), LANES), lambda i: (i, 0)), rt(TOP_K), rt(TOP_K), rt(TOP_K),
                   _full((1, n_exp))],
        out_shape=[jax.ShapeDtypeStruct((n * (d // LANES), LANES), f32), jax.ShapeDtypeStruct((n, TOP_K), jnp.int32),
                   jax.ShapeDtypeStruct((n, TOP_K), jnp.int32), jax.ShapeDtypeStruct((n, TOP_K), f32),
                   jax.ShapeDtypeStruct((1, n_exp), f32)],
        compiler_params=_params("arbitrary"),
        name="moe_router",
    )(xa, xb, g.reshape(1, d), rw.astype(f32), rb.reshape(1, n_exp).astype(f32))


def _invert_kernel(dest_ref, zs_ref, ze_ref, slot_ref, *, n):
    n_seg = zs_ref.shape[0]

    def padding(seg, nxt):
        def fill(r, s):
            slot_ref[r] = s
            return s + 1
        return lax.fori_loop(zs_ref[seg], ze_ref[seg], fill, nxt)
    lax.fori_loop(0, n_seg - 1, padding, n * TOP_K)

    def tail(r, c):
        slot_ref[r] = r
        return c
    lax.fori_loop(zs_ref[n_seg - 1], ze_ref[n_seg - 1], tail, 0)

    for kk in range(TOP_K):
        def place(t, c, kk=kk):
            slot_ref[dest_ref[kk * n + t]] = t * TOP_K + kk
            return c
        lax.fori_loop(0, n, place, 0, unroll=8)


def _invert(dest_km, zero_start, zero_end, n, n_rows):
    grid_spec = pltpu.PrefetchScalarGridSpec(
        num_scalar_prefetch=3, grid=(1,), in_specs=[],
        out_specs=pl.BlockSpec(memory_space=pltpu.SMEM))
    return pl.pallas_call(
        functools.partial(_invert_kernel, n=n),
        grid_spec=grid_spec,
        out_shape=jax.ShapeDtypeStruct((n_rows,), jnp.int32),
        compiler_params=_params("arbitrary"),
        name="moe_invert",
    )(dest_km, zero_start, zero_end)


def _expert_kernel(be_ref, nv_ref, nxt_ref, tok_ref, slot_ref, ht_hbm, wgu_hbm, bgu_ref, wd_hbm, bd_ref, y_hbm,
                   xbuf, xsem, ybuf, ysem, zbuf, zsem, wgu_f, wd_f, wsem, wgu_b, wd_b, *, layer, blk):
    i = pl.program_id(0)
    slot = i % 2
    nvalid = nv_ref[0]
    dff = wd_b.shape[0]
    nchunk = wgu_b.shape[0] // LANES
    last_block = pl.num_programs(0) - 1

    def gather(block, into):
        for r in range(blk):
            tok = tok_ref[block * blk + r]
            pltpu.make_async_copy(ht_hbm.at[_token_rows(tok, nchunk)], xbuf.at[into, _token_rows(r, nchunk)],
                                  xsem.at[into]).start(priority=0)

    def block_sent(s):
        return pltpu.make_async_copy(ybuf.at[s], ybuf.at[s], ysem.at[s])

    def send_row(r, frm, dst):
        pltpu.make_async_copy(ybuf.at[frm, _token_rows(r, nchunk)], y_hbm.at[_token_rows(dst, nchunk)],
                              ysem.at[frm]).start(priority=1)

    def weight_copies(e):
        return (pltpu.make_async_copy(wgu_hbm.at[layer, e], wgu_f, wsem.at[0]),
                pltpu.make_async_copy(wd_hbm.at[layer, e], wd_f, wsem.at[1]))

    @pl.when(i == 0)
    def _():
        ybuf[...] = jnp.zeros_like(ybuf)
        zbuf[...] = jnp.zeros_like(zbuf)
        gather(0, 0)
        for cp in weight_copies(be_ref[0]):
            cp.start(priority=1)

    @pl.when((i >= 1) & (i <= nvalid))
    def _():
        block_sent(slot).wait()

    @pl.when((i < nvalid) & ((i == 0) | (be_ref[i] != be_ref[jnp.maximum(i - 1, 0)])))
    def _():
        for cp in weight_copies(be_ref[i]):
            cp.wait()
        wgu_b[...] = wgu_f[...].astype(bf16)
        wd_b[...] = wd_f[...].astype(bf16)

        @pl.when(nxt_ref[i] >= 0)
        def _():
            for cp in weight_copies(nxt_ref[i]):
                cp.start(priority=1)

    @pl.when(i < nvalid)
    def _():
        pltpu.make_async_copy(xbuf.at[slot], xbuf.at[slot], xsem.at[slot]).wait()
        gather(jnp.minimum(i + 1, last_block), 1 - slot)
        for r in range(blk):
            send_row(r, 1 - slot, slot_ref[i * blk + r])
        x = jnp.concatenate([xbuf[slot, _token_chunk(j, blk, nchunk), :].astype(bf16) for j in range(nchunk)],
                            axis=-1)
        hgu = jnp.dot(x, wgu_b[...], preferred_element_type=f32) + bgu_ref[0, 0]
        hg = jnp.minimum(hgu[:, :dff], SWIGLU_LIMIT)
        hu = jnp.clip(hgu[:, dff:], -SWIGLU_LIMIT, SWIGLU_LIMIT)
        act = (hu + 1.0) * hg * jax.nn.sigmoid(SWIGLU_ALPHA * hg)
        y = _dot(act, wd_b[...]) + bd_ref[0, 0]
        for j in range(nchunk):
            ybuf[slot, _token_chunk(j, blk, nchunk), :] = y[:, j * LANES:(j + 1) * LANES]

    @pl.when(i == nvalid)
    def _():
        pltpu.make_async_copy(xbuf.at[slot], xbuf.at[slot], xsem.at[slot]).wait()

        def send(r, c):
            send_row(r, 1 - slot, slot_ref[i * blk + r])
            return c
        lax.fori_loop(0, blk, send, 0, unroll=8)
        block_sent(1 - slot).wait()

    @pl.when(i >= nvalid)
    def _():
        first = pl.multiple_of(i * (blk * nchunk), blk * nchunk)
        fill = pltpu.make_async_copy(zbuf, y_hbm.at[pl.ds(first, blk * nchunk)], zsem)
        fill.start(priority=1)
        fill.wait()


def _experts(ht, row_slot, block_e, nvalid, next_e, w_gu, b_gu, w_down, b_down, l, blk):
    depth, n_exp, d, dff2 = w_gu.shape
    dff = dff2 // 2
    nchunk = d // LANES
    n_rows = row_slot.shape[0]
    n_blocks = n_rows // blk
    assert n_blocks >= 2
    n_tok = ht.shape[0] // nchunk
    row_tok = jnp.minimum(row_slot // TOP_K, n_tok - 1)
    send_slot = jnp.concatenate([n_rows + jnp.arange(blk, dtype=jnp.int32), row_slot])
    bspec = lambda cols: pl.BlockSpec((1, 1, 1, cols), lambda i, be, *_: (l, be[i], 0, 0))
    grid_spec = pltpu.PrefetchScalarGridSpec(
        num_scalar_prefetch=5,
        grid=(n_blocks,),
        in_specs=[pl.BlockSpec(memory_space=pl.ANY), pl.BlockSpec(memory_space=pl.ANY), bspec(dff2),
                  pl.BlockSpec(memory_space=pl.ANY), bspec(d)],
        out_specs=pl.BlockSpec(memory_space=pl.ANY),
        scratch_shapes=[pltpu.VMEM((2, blk * nchunk, LANES), f32), pltpu.SemaphoreType.DMA((2,)),
                        pltpu.VMEM((2, blk * nchunk, LANES), f32), pltpu.SemaphoreType.DMA((2,)),
                        pltpu.VMEM((blk * nchunk, LANES), f32), pltpu.SemaphoreType.DMA(()),
                        pltpu.VMEM((d, dff2), f32), pltpu.VMEM((dff, d), f32), pltpu.SemaphoreType.DMA((2,)),
                        pltpu.VMEM((d, dff2), bf16), pltpu.VMEM((dff, d), bf16)],
    )
    return pl.pallas_call(
        functools.partial(_expert_kernel, layer=l, blk=blk),
        grid_spec=grid_spec,
        out_shape=jax.ShapeDtypeStruct(((n_rows + blk) * nchunk, LANES), f32),
        compiler_params=_params("arbitrary"),
        name="moe_experts",
    )(block_e, nvalid, next_e, row_tok, send_slot, ht, w_gu, b_gu.reshape(depth, n_exp, 1, dff2), w_down,
      b_down.reshape(depth, n_exp, 1, d))


def _combine_kernel(gate_ref, xa_ref, xb_ref, y_ref, g_ref, oa_ref, ob_ref, acc_s, *, na_tiles, final):
    i = pl.program_id(0)
    tm, d = xa_ref.shape
    nchunk = d // LANES
    for t in range(tm):
        first = (i * tm + t) * TOP_K
        acc = gate_ref[first] * y_ref[t * TOP_K * nchunk:(t * TOP_K + 1) * nchunk, :]
        for kk in range(1, TOP_K):
            row = (t * TOP_K + kk) * nchunk
            acc = acc + gate_ref[first + kk] * y_ref[row:row + nchunk, :]
        acc_s[t * nchunk:(t + 1) * nchunk, :] = acc
    moe = jnp.concatenate([acc_s[_token_chunk(j, tm, nchunk), :] for j in range(nchunk)], axis=-1)
    x3 = jnp.where(i < na_tiles, xa_ref[...], xb_ref[...]) + moe
    out = _rms(x3, g_ref[...]) if final else x3

    @pl.when(i < na_tiles)
    def _():
        oa_ref[...] = out

    @pl.when(i >= na_tiles)
    def _():
        ob_ref[...] = out


def _combine(xa, xb, gate, y_slots, g_final, final):
    d = xa.shape[1]
    n = xa.shape[0] + xb.shape[0]
    tm = COMBINE_TILE
    assert xa.shape[0] % tm == 0 and xb.shape[0] % tm == 0
    na_tiles = xa.shape[0] // tm
    sa, sb = _two_group_specs(tm, d, na_tiles)
    nchunk = d // LANES
    grid_spec = pltpu.PrefetchScalarGridSpec(
        num_scalar_prefetch=1,
        grid=(n // tm,),
        in_specs=[sa, sb, pl.BlockSpec((tm * TOP_K * nchunk, LANES), lambda i, gt: (i, 0)),
                  pl.BlockSpec((1, d), lambda i, gt: (0, 0))],
        out_specs=[sa, sb],
        scratch_shapes=[pltpu.VMEM((tm * nchunk, LANES), f32)],
    )
    return pl.pallas_call(
        functools.partial(_combine_kernel, na_tiles=na_tiles, final=final),
        grid_spec=grid_spec,
        out_shape=[jax.ShapeDtypeStruct(xa.shape, f32), jax.ShapeDtypeStruct(xb.shape, f32)],
        compiler_params=_params("arbitrary"),
        name="moe_combine",
    )(gate.reshape(-1), xa, xb, y_slots, g_final.reshape(1, d))


def _moe(xa, xb, g_ffn, router_w, router_b, w_gu, b_gu, w_down, b_down, l, g_out, final):
    n = xa.shape[0] + xb.shape[0]
    n_exp = router_w.shape[1]
    blk = MOE_ROWS
    ht, top_e, top_rank, gate, counts = _router(xa, xb, g_ffn, router_w, router_b)

    counts = counts.reshape(n_exp).astype(jnp.int32)
    padded = (counts + blk - 1) // blk * blk
    pad_end = jnp.cumsum(padded)
    pad_start = pad_end - padded
    dest_km = (pad_start[top_e] + top_rank).T.reshape(-1)
    n_rows = -(-(n * TOP_K) // blk) * blk + n_exp * blk
    block_start = jnp.arange(n_rows // blk, dtype=jnp.int32) * blk
    block_e = jnp.minimum(jnp.sum(pad_end[None, :] <= block_start[:, None], axis=1), n_exp - 1).astype(jnp.int32)
    nvalid = (pad_end[-1:] // blk).astype(jnp.int32)
    zero_start = jnp.concatenate([pad_start + counts, pad_end[-1:]]).astype(jnp.int32)
    zero_end = jnp.concatenate([pad_end, jnp.full((1,), n_rows, jnp.int32)]).astype(jnp.int32)

    has_rows = padded > 0
    ids = jnp.arange(n_exp, dtype=jnp.int32)
    after = jnp.where(has_rows[None, :] & (ids[None, :] > ids[:, None]), ids[None, :], n_exp)
    next_of = jnp.min(after, axis=1)
    next_e = jnp.where(next_of < n_exp, next_of, -1).astype(jnp.int32)[block_e]

    row_slot = _invert(dest_km, zero_start, zero_end, n, n_rows)
    y_slots = _experts(ht, row_slot, block_e, nvalid, next_e, w_gu, b_gu, w_down, b_down, l, blk)
    return _combine(xa, xb, gate, y_slots, g_out, final)


def _mixers(x, t, b, h0r, h0i, prev, s0, P, l):
    s5w = P['s5_d'].shape[-1]
    u, prk = _inproj(x, P['norm_mix_g'][l], P['w_in'][l].astype(bf16), s5w)
    ar, ai, bre, bim, cre_m, cim_m = _s5_discretize(P['s5_lambda_re'][l], P['s5_lambda_im'][l], P['s5_log_dt'][l],
                                                    P['s5_b_re'][l], P['s5_b_im'][l], P['s5_c_re'][l], P['s5_c_im'][l])
    s5y, hr, hi = _s5(jnp.transpose(u.reshape(b, t, s5w), (1, 0, 2)), h0r, h0i, ar, ai, bre, bim, cre_m, cim_m,
                      P['s5_d'][l].reshape(1, s5w).astype(f32), P['s5_glu_w'][l].astype(bf16),
                      P['s5_glu_b'][l].reshape(1, s5w).astype(f32))
    s5y = jnp.transpose(s5y, (1, 0, 2)).reshape(b * t, s5w)
    rwy, s_fin = _rwkv(prk, prev, s0, t, b, P['rwkv_mu'][l], P['rwkv_w0'][l], P['rwkv_w1'][l], P['rwkv_w2'][l],
                       P['rwkv_a0'][l], P['rwkv_a1'][l], P['rwkv_a2'][l], P['rwkv_g1'][l], P['rwkv_g2'][l],
                       P['rwkv_k_k'][l], P['rwkv_k_a'][l], P['rwkv_r_k'][l], P['rwkv_ln_w'][l], P['rwkv_ln_b'][l])
    x1, q = _outproj(x, s5y, rwy, P['w_out'][l].astype(bf16), P['norm_ca_g'][l], P['w_cq'][l].astype(bf16))
    shift_last = prk.reshape(b, t, -1)[:, t - 1]
    return x1, q, hr, hi, shift_last, s_fin


def kernel(x_prompt, x_sample, mem_prompt, state_s5_re, state_s5_im, state_rwkv_shift, state_rwkv_wkv, cache_mem_k, cache_mem_v, norm_mix_g, w_in, s5_lambda_re, s5_lambda_im, s5_log_dt, s5_b_re, s5_b_im, s5_c_re, s5_c_im, s5_d, s5_glu_w, s5_glu_b, rwkv_mu, rwkv_w0, rwkv_w1, rwkv_w2, rwkv_a0, rwkv_a1, rwkv_a2, rwkv_g1, rwkv_g2, rwkv_k_k, rwkv_k_a, rwkv_r_k, rwkv_ln_w, rwkv_ln_b, w_out, norm_ca_g, norm_mem_g, w_cq, w_mem_kv, w_co, norm_ffn_g, router_w, router_b, ex_w_gu, ex_b_gu, ex_w_down, ex_b_down, norm_final_g):
    P = dict(norm_mix_g=norm_mix_g, w_in=w_in, s5_lambda_re=s5_lambda_re, s5_lambda_im=s5_lambda_im,
             s5_log_dt=s5_log_dt, s5_b_re=s5_b_re, s5_b_im=s5_b_im, s5_c_re=s5_c_re, s5_c_im=s5_c_im,
             s5_d=s5_d, s5_glu_w=s5_glu_w, s5_glu_b=s5_glu_b, rwkv_mu=rwkv_mu, rwkv_w0=rwkv_w0,
             rwkv_w1=rwkv_w1, rwkv_w2=rwkv_w2, rwkv_a0=rwkv_a0, rwkv_a1=rwkv_a1, rwkv_a2=rwkv_a2,
             rwkv_g1=rwkv_g1, rwkv_g2=rwkv_g2, rwkv_k_k=rwkv_k_k, rwkv_k_a=rwkv_k_a, rwkv_r_k=rwkv_r_k,
             rwkv_ln_w=rwkv_ln_w, rwkv_ln_b=rwkv_ln_b, w_out=w_out, norm_ca_g=norm_ca_g, w_cq=w_cq)
    depth = norm_mix_g.shape[0]
    bp, tp, d = x_prompt.shape
    bs, ts, _ = x_sample.shape
    n_mem = mem_prompt.shape[1]
    s5_groups, s5_state = s5_lambda_re.shape[1:]
    nstate = s5_groups * s5_state
    rw_in = state_rwkv_shift.shape[-1]
    heads = state_rwkv_wkv.shape[2]
    mh = cache_mem_k.shape[3]
    assert mh == MEM_HEADS

    xp = x_prompt.reshape(bp * tp, d)
    xs = x_sample.reshape(bs * ts, d)
    outs = {k: [] for k in ('p_re', 'p_im', 'p_sh', 'p_wkv', 'p_mk', 'p_mv', 's_re', 's_im', 's_sh', 's_wkv')}
    for l in range(depth):
        mk, mv = _memkv(mem_prompt.reshape(bp * n_mem, d), norm_mem_g[l], w_mem_kv[l].astype(bf16))
        zeros_state = jnp.zeros((bp, nstate), f32)
        x1p, qp, re_p, im_p, sh_p, wkv_p = _mixers(
            xp, tp, bp, zeros_state, zeros_state, jnp.zeros((bp, rw_in), f32),
            jnp.zeros((bp, heads, RWKV_HEAD, RWKV_HEAD), f32), P, l)
        x1s, qs, re_s, im_s, sh_s, wkv_s = _mixers(
            xs, ts, bs, state_s5_re[l].reshape(bs, nstate), state_s5_im[l].reshape(bs, nstate),
            state_rwkv_shift[l], state_rwkv_wkv[l], P, l)
        wco = w_co[l].astype(bf16)
        x2p = _attn(qp, x1p, mk, mv, wco, tp, bp)
        x2s = _attn(qs, x1s, _kv_to_head_tiles(cache_mem_k[l]), _kv_to_head_tiles(cache_mem_v[l]), wco, ts, bs)
        last = l == depth - 1
        xp, xs = _moe(x2p, x2s, norm_ffn_g[l], router_w[l], router_b[l], ex_w_gu, ex_b_gu, ex_w_down, ex_b_down,
                      l, norm_final_g, last)
        outs['p_re'].append(re_p.reshape(bp, s5_groups, s5_state))
        outs['p_im'].append(im_p.reshape(bp, s5_groups, s5_state))
        outs['p_sh'].append(sh_p)
        outs['p_wkv'].append(wkv_p)
        outs['p_mk'].append(_kv_from_head_tiles(mk, bp, n_mem, mh, d // mh))
        outs['p_mv'].append(_kv_from_head_tiles(mv, bp, n_mem, mh, d // mh))
        outs['s_re'].append(re_s.reshape(bs, s5_groups, s5_state))
        outs['s_im'].append(im_s.reshape(bs, s5_groups, s5_state))
        outs['s_sh'].append(sh_s)
        outs['s_wkv'].append(wkv_s)
    st = lambda k: jnp.stack(outs[k])
    return (xp.reshape(bp, tp, d), xs.reshape(bs, ts, d), st('p_re'), st('p_im'), st('p_sh'), st('p_wkv'),
            st('p_mk'), st('p_mv'), st('s_re'), st('s_im'), st('s_sh'), st('s_wkv'))
```

```python
import functools

import jax
import jax.numpy as jnp
from jax import lax
from jax.experimental import pallas as pl
from jax.experimental.pallas import tpu as pltpu

f32 = jnp.float32
bf16 = jnp.bfloat16

RWKV_HEAD = 64
HEAD_PAIR = 2 * RWKV_HEAD
DECAY_SCALE = 0.6065306597
MEM_HEADS = 4
TOP_K = 4
SWIGLU_LIMIT = 7.0
SWIGLU_ALPHA = 1.702
RMS_EPS = 1e-5
GN_EPS = 64e-5

V7X_VMEM_LIMIT = 56 * 1024 * 1024
ROW_TILE = 512
MOE_ROWS = 256
COMBINE_TILE = 256
WKV_CHUNK = 64
WKV_BATCH = 4
S5_CHUNK = 64
S5_BLOCK_CHANNELS = 128
LANES = 128
DMA_PRIORITIES = 2
ATTN_SHORT_BATCH = 4

_NT = (((1,), (1,)), ((), ()))
_TN = (((0,), (0,)), ((), ()))


def _params(*sem):
    return pltpu.CompilerParams(dimension_semantics=sem, vmem_limit_bytes=V7X_VMEM_LIMIT)


def _dot(a, b):
    return jnp.dot(a.astype(bf16), b.astype(bf16), preferred_element_type=f32)


def _dg(a, b, dims):
    return lax.dot_general(a.astype(bf16), b.astype(bf16), dims, preferred_element_type=f32)


def _rms(x, g):
    return x * lax.rsqrt(jnp.mean(x * x, axis=-1, keepdims=True) + RMS_EPS) * g


def _full(shape):
    n = len(shape)
    return pl.BlockSpec(shape, lambda *_: (0,) * n)


def _tile(rows, tile=ROW_TILE):
    t = min(tile, rows)
    assert rows % t == 0
    return t


def _inproj_kernel(x_ref, g_ref, w_ref, u_ref, p_ref, *, s5w):
    nb = _rms(x_ref[...], g_ref[...]).astype(bf16)
    u_ref[...] = jnp.dot(nb, w_ref[:, :s5w], preferred_element_type=f32)
    p_ref[...] = jnp.dot(nb, w_ref[:, s5w:], preferred_element_type=f32)


def _inproj(x, g, w, s5w):
    rows, d = x.shape
    inw = w.shape[1]
    tm = _tile(rows)
    return pl.pallas_call(
        functools.partial(_inproj_kernel, s5w=s5w),
        grid=(rows // tm,),
        in_specs=[pl.BlockSpec((tm, d), lambda i: (i, 0)), _full((1, d)), _full((d, inw))],
        out_specs=[pl.BlockSpec((tm, s5w), lambda i: (i, 0)), pl.BlockSpec((tm, inw - s5w), lambda i: (i, 0))],
        out_shape=[jax.ShapeDtypeStruct((rows, s5w), f32), jax.ShapeDtypeStruct((rows, inw - s5w), f32)],
        compiler_params=_params("parallel"),
        name="inproj",
    )(x, g.reshape(1, d), w)


def _s5_kernel(u_ref, h0r_ref, h0i_ref, ar_ref, ai_ref, bre_ref, bim_ref, cre_ref, cim_ref, d_ref, gw_ref, gb_ref,
               y_ref, hr_ref, hi_ref, xr_s, xi_s, *, steps, bt, lane_chunk):
    i = pl.program_id(1)
    width = u_ref.shape[-1]
    nstate = ar_ref.shape[-1]

    @pl.when(i == 0)
    def _():
        hr_ref[...] = h0r_ref[...]
        hi_ref[...] = h0i_ref[...]

    u = u_ref[...].reshape(steps * bt, width)
    ub = u.astype(bf16)
    nblk, cb, sb = bre_ref.shape
    for j in range(nblk):
        uj = ub[:, j * cb:(j + 1) * cb]
        xr_s[:, j * sb:(j + 1) * sb] = jnp.dot(uj, bre_ref[j], preferred_element_type=f32)
        xi_s[:, j * sb:(j + 1) * sb] = jnp.dot(uj, bim_ref[j], preferred_element_type=f32)

    for c in range(nstate // lane_chunk):
        sl = slice(c * lane_chunk, (c + 1) * lane_chunk)
        arb = jnp.broadcast_to(ar_ref[:, sl], (bt, lane_chunk))
        aib = jnp.broadcast_to(ai_ref[:, sl], (bt, lane_chunk))

        def step(t, carry, sl=sl, arb=arb, aib=aib):
            hr, hi = carry
            r0 = pl.multiple_of(t * bt, bt)
            nhr = arb * hr - aib * hi + xr_s[pl.ds(r0, bt), sl]
            nhi = arb * hi + aib * hr + xi_s[pl.ds(r0, bt), sl]
            xr_s[pl.ds(r0, bt), sl] = nhr
            xi_s[pl.ds(r0, bt), sl] = nhi
            return nhr, nhi

        hr, hi = lax.fori_loop(0, steps, step, (hr_ref[:, sl], hi_ref[:, sl]), unroll=2)
        hr_ref[:, sl] = hr
        hi_ref[:, sl] = hi

    y = jnp.concatenate(
        [jnp.dot(xr_s[:, j * sb:(j + 1) * sb].astype(bf16), cre_ref[j], preferred_element_type=f32)
         + jnp.dot(xi_s[:, j * sb:(j + 1) * sb].astype(bf16), cim_ref[j], preferred_element_type=f32)
         for j in range(nblk)], axis=-1) + d_ref[...] * u
    z = jax.nn.gelu(y)
    out = z * jax.nn.sigmoid(_dot(z, gw_ref[...]) + gb_ref[...])
    y_ref[...] = out.reshape(steps, bt, width)


def _s5(u3, h0r, h0i, ar, ai, bre, bim, cre_m, cim_m, d, gw, gb):
    t, b, width = u3.shape
    nstate = ar.shape[-1]
    bt = 8
    steps = min(S5_CHUNK, t)
    assert b % bt == 0 and t % steps == 0
    kern = functools.partial(_s5_kernel, steps=steps, bt=bt, lane_chunk=512)
    st_spec = pl.BlockSpec((bt, nstate), lambda j, i: (j, 0))
    y, hr, hi = pl.pallas_call(
        kern,
        grid=(b // bt, t // steps),
        in_specs=[pl.BlockSpec((steps, bt, width), lambda j, i: (i, j, 0)), st_spec, st_spec,
                  _full((1, nstate)), _full((1, nstate)), _full(bre.shape), _full(bim.shape),
                  _full(cre_m.shape), _full(cim_m.shape), _full((1, width)), _full((width, width)),
                  _full((1, width))],
        out_specs=[pl.BlockSpec((steps, bt, width), lambda j, i: (i, j, 0)), st_spec, st_spec],
        out_shape=[jax.ShapeDtypeStruct((t, b, width), f32), jax.ShapeDtypeStruct((b, nstate), f32),
                   jax.ShapeDtypeStruct((b, nstate), f32)],
        scratch_shapes=[pltpu.VMEM((steps * bt, nstate), f32), pltpu.VMEM((steps * bt, nstate), f32)],
        compiler_params=_params("arbitrary", "arbitrary"),
        name="s5_mixer",
    )(u3, h0r, h0i, ar, ai, bre, bim, cre_m, cim_m, d, gw, gb)
    return y, hr, hi


def _s5_discretize(lam_re, lam_im, log_dt, b_re, b_im, c_re, c_im):
    g, p = lam_re.shape
    dt = jnp.exp(log_dt.astype(f32))[:, None]
    lr = lam_re.astype(f32)
    li = lam_im.astype(f32)
    mag = jnp.exp(lr * dt)
    ar = mag * jnp.cos(li * dt)
    ai = mag * jnp.sin(li * dt)
    den = lr * lr + li * li
    cr = ((ar - 1.0) * lr + ai * li) / den
    ci = (ai * lr - (ar - 1.0) * li) / den
    br = b_re.astype(f32)
    bi = b_im.astype(f32)
    bbar_r = cr[..., None] * br - ci[..., None] * bi
    bbar_i = cr[..., None] * bi + ci[..., None] * br
    c = br.shape[-1]
    gb = S5_BLOCK_CHANNELS // c
    assert g % gb == 0
    eye = jnp.eye(gb, dtype=f32)
    nblk = g // gb
    bd_in = lambda m: jnp.einsum('ngpc,gh->ngchp', m.reshape(nblk, gb, p, c), eye).reshape(
        nblk, gb * c, gb * p).astype(bf16)
    bd_out = lambda m: jnp.einsum('ngcp,gh->ngphc', m.reshape(nblk, gb, c, p), eye).reshape(
        nblk, gb * p, gb * c).astype(bf16)
    return (ar.reshape(1, g * p), ai.reshape(1, g * p), bd_in(bbar_r), bd_in(bbar_i),
            bd_out(c_re.astype(f32)), bd_out(-c_im.astype(f32)))


def _rwkv_kernel(p_ref, prev_ref, s0_ref, mu_ref, w0_ref, w1_ref, w2_ref, a0_ref, a1_ref, a2_ref, g1_ref, g2_ref,
                 kk_ref, ka_ref, rk_ref, lnw_ref, lnb_ref, y_ref, sout_ref, prev_s, st_s, *, steps, nsq):
    i = pl.program_id(1)
    L = steps
    L2 = 2 * L
    nb = p_ref.shape[0]
    w = p_ref.shape[-1] // 4
    npair = w // HEAD_PAIR

    row2 = lax.broadcasted_iota(jnp.int32, (L2, HEAD_PAIR), 0)
    lane2 = lax.broadcasted_iota(jnp.int32, (L2, HEAD_PAIR), 1)
    own = (row2 < L) == (lane2 < RWKV_HEAD)
    ri = lax.broadcasted_iota(jnp.int32, (L2, L2), 0)
    ci = lax.broadcasted_iota(jnp.int32, (L2, L2), 1)
    same = (ri < L) == (ci < L)
    strict = same & (ci < ri)
    incl = same & (ci <= ri)
    eye = (ri == ci).astype(f32)
    sr = lax.broadcasted_iota(jnp.int32, (HEAD_PAIR, HEAD_PAIR), 0)
    sc = lax.broadcasted_iota(jnp.int32, (HEAD_PAIR, HEAD_PAIR), 1)
    sdiag = (sr < RWKV_HEAD) == (sc < RWKV_HEAD)

    @pl.when(i == 0)
    def _():
        prev_s[...] = prev_ref[...]
        for bb in range(nb):
            for pr in range(npair):
                sp = s0_ref[bb, pr * HEAD_PAIR:(pr + 1) * HEAD_PAIR, :]
                st_s[bb * npair + pr] = jnp.where(sdiag, jnp.concatenate([sp, sp], axis=1), 0.0)

    trow = lax.broadcasted_iota(jnp.int32, (L, 1), 0)
    ps, deltas = [], []
    for bb in range(nb):
        pb = p_ref[bb]
        shifted = jnp.where(trow == 0, prev_s[bb], pltpu.roll(pb, 1, axis=0))
        prev_s[bb] = pb[L - 1:L, :]
        ps.append(pb)
        deltas.append(shifted - pb)
    p = jnp.concatenate(ps, axis=0) if nb > 1 else ps[0]
    delta = jnp.concatenate(deltas, axis=0) if nb > 1 else deltas[0]
    mu = mu_ref[...]
    pz = p[:, 3 * w:]
    dz = delta[:, 3 * w:]
    r = p[:, :w] + delta[:, :w] * mu[0:1]
    k = p[:, w:2 * w] + delta[:, w:2 * w] * mu[1:2]
    v = p[:, 2 * w:3 * w] + delta[:, 2 * w:3 * w] * mu[2:3]
    zw = pz + dz * mu[3:4]
    za = pz + dz * mu[4:5]
    zg = pz + dz * mu[5:6]
    logw = -DECAY_SCALE * jax.nn.sigmoid(w0_ref[...] + _dot(jnp.tanh(_dot(zw, w1_ref[...])), w2_ref[...]))
    a = jax.nn.sigmoid(a0_ref[...] + _dot(_dot(za, a1_ref[...]), a2_ref[...]))
    g = _dot(jax.nn.sigmoid(_dot(zg, g1_ref[...])), g2_ref[...])
    kkraw = k * kk_ref[...]
    kmod = k * (1.0 + (a - 1.0) * ka_ref[...])
    rk = r * kmod * rk_ref[...]

    tr = lax.broadcasted_iota(jnp.int32, (nb * L, nb * L), 0)
    tc = lax.broadcasted_iota(jnp.int32, (nb * L, nb * L), 1)
    tri = ((tc <= tr) & (tc >= (tr // L) * L)).astype(bf16)
    hi = logw.astype(bf16)
    rem = logw - hi.astype(f32)
    mid = rem.astype(bf16)
    lo = (rem - mid.astype(f32)).astype(bf16)
    cum = (jnp.dot(tri, hi, preferred_element_type=f32) + jnp.dot(tri, mid, preferred_element_type=f32)
           + jnp.dot(tri, lo, preferred_element_type=f32))

    stack = lambda x: jnp.concatenate([x, x], axis=0)
    chains = [(bb, pr) for bb in range(nb) for pr in range(npair)]
    nch = range(len(chains))
    rs = [slice(bb * L, (bb + 1) * L) for bb, _ in chains]
    sls = [slice(pr * HEAD_PAIR, (pr + 1) * HEAD_PAIR) for _, pr in chains]
    cut = lambda x, c: stack(x[rs[c], sls[c]])
    own_of = lambda x, c: jnp.where(own, cut(x, c), 0.0)

    wt, rm, km, bm, kap, vmf, vm, gram = [], [], [], [], [], [], [], []
    for c in nch:
        cum_c = cut(cum, c)
        wt_c = jnp.exp(cum_c)
        winv = jnp.exp(-cum_c)
        wprev = jnp.exp(cum_c - cut(logw, c))
        kkr = own_of(kkraw, c)
        kkn = kkr * lax.rsqrt(jnp.maximum(jnp.sum(kkr * kkr, axis=-1, keepdims=True), 1e-24))
        wt.append(wt_c)
        rm.append((own_of(r, c) * wt_c).astype(bf16))
        km.append((own_of(kmod, c) * winv).astype(bf16))
        bm.append((kkn * cut(a, c) * winv).astype(bf16))
        kap.append((kkn * wprev).astype(bf16))
        vmf.append(own_of(v, c))
        vm.append(vmf[c].astype(bf16))
        gram.append(_dg(jnp.concatenate([kap[c], rm[c]], axis=0), jnp.concatenate([bm[c], km[c]], axis=0), _NT))
    n_mat = [jnp.where(strict, x[:L2, :L2], 0.0) for x in gram]
    p_mat = [jnp.where(strict, x[:L2, L2:], 0.0) for x in gram]
    g_mat = [jnp.where(incl, x[L2:, :L2], 0.0) for x in gram]
    q_mat = [jnp.where(incl, x[L2:, L2:], 0.0) for x in gram]

    neg = [-x for x in n_mat]
    inv = [eye + x for x in neg]
    if nsq > 0:
        pw = [_dot(x, x) for x in neg]
        for _ in range(nsq - 1):
            both = [_dot(pw[c], jnp.concatenate([inv[c], pw[c]], axis=1)) for c in nch]
            inv = [inv[c] + both[c][:, :L2] for c in nch]
            pw = [x[:, L2:] for x in both]
        inv = [inv[c] + _dot(pw[c], inv[c]) for c in nch]

    s_old = [st_s[c] for c in nch]
    rhs = [_dg(kap[c], s_old[c], _NT) + _dot(p_mat[c], vm[c]) for c in nch]
    ym0 = [_dg(rm[c], s_old[c], _NT) for c in nch]
    um = [(-_dot(inv[c], rhs[c])).astype(bf16) for c in nch]
    vu = [jnp.concatenate([vm[c], um[c]], axis=0) for c in nch]
    ym = [ym0[c] + _dot(jnp.concatenate([q_mat[c], g_mat[c]], axis=1), vu[c]) for c in nch]
    for c in nch:
        st_s[c] = (s_old[c] + _dg(vu[c], jnp.concatenate([km[c], bm[c]], axis=0), _TN)) * wt[c][L - 1:L, :]

    for c, (bb, pr) in enumerate(chains):
        mean = jnp.sum(ym[c], axis=-1, keepdims=True) * (1.0 / RWKV_HEAD)
        yc = jnp.where(own, ym[c] - mean, 0.0)
        var = jnp.sum(yc * yc, axis=-1, keepdims=True) * (1.0 / RWKV_HEAD)
        yn = yc * lax.rsqrt(var + GN_EPS)
        bonus = jnp.sum(own_of(rk, c), axis=-1, keepdims=True) * vmf[c]
        mix = (yn[:L] + yn[L:]) * lnw_ref[:, sls[c]] + lnb_ref[:, sls[c]] + (bonus[:L] + bonus[L:])
        y_ref[bb, :, sls[c]] = mix * g[rs[c], sls[c]]

    @pl.when(i == pl.num_programs(1) - 1)
    def _():
        for c, (bb, pr) in enumerate(chains):
            s_new = st_s[c]
            sout_ref[bb, pr * HEAD_PAIR:(pr + 1) * HEAD_PAIR, :] = s_new[:, :RWKV_HEAD] + s_new[:, RWKV_HEAD:]


def _rwkv(prk, prev, s0, t, b, mu, w0, w1, w2, a0, a1, a2, g1, g2, k_k, k_a, r_k, ln_w, ln_b):
    w = prk.shape[-1] // 4
    steps = min(WKV_CHUNK, t)
    nb = min(WKV_BATCH * max(1, WKV_CHUNK // (4 * steps)), b)
    assert t % steps == 0 and w % HEAD_PAIR == 0 and b % nb == 0
    nsq = max((steps - 1).bit_length() - 1, 0)
    row = lambda x: x.reshape(1, -1).astype(f32)
    wb = lambda x: x.astype(bf16)
    weights = [mu.astype(f32), row(w0), wb(w1), wb(w2), row(a0), wb(a1), wb(a2), wb(g1), wb(g2),
               row(k_k), row(k_a), row(r_k), row(ln_w), row(ln_b)]
    state_spec = pl.BlockSpec((nb, w, RWKV_HEAD), lambda j, i: (j, 0, 0))
    y, s_fin = pl.pallas_call(
        functools.partial(_rwkv_kernel, steps=steps, nsq=nsq),
        grid=(b // nb, t // steps),
        in_specs=[pl.BlockSpec((nb, steps, 4 * w), lambda j, i: (j, i, 0)),
                  pl.BlockSpec((nb, 1, 4 * w), lambda j, i: (j, 0, 0)), state_spec]
                 + [_full(x.shape) for x in weights],
        out_specs=[pl.BlockSpec((nb, steps, w), lambda j, i: (j, i, 0)), state_spec],
        out_shape=[jax.ShapeDtypeStruct((b, t, w), f32), jax.ShapeDtypeStruct((b, w, RWKV_HEAD), f32)],
        scratch_shapes=[pltpu.VMEM((nb, 1, 4 * w), f32),
                        pltpu.VMEM((nb * (w // HEAD_PAIR), HEAD_PAIR, HEAD_PAIR), f32)],
        compiler_params=_params("arbitrary", "arbitrary"),
        name="rwkv7_mixer",
    )(prk.reshape(b, t, 4 * w), prev.reshape(b, 1, 4 * w).astype(f32), s0.reshape(b, w, RWKV_HEAD).astype(f32),
      *weights)
    return y.reshape(b * t, w), s_fin.reshape(b, w // RWKV_HEAD, RWKV_HEAD, RWKV_HEAD)


def _outproj_kernel(x_ref, s5_ref, rw_ref, wo_ref, g_ref, wq_ref, x1_ref, q_ref, *, s5w):
    x1 = x_ref[...] + _dot(s5_ref[...], wo_ref[:s5w, :]) + _dot(rw_ref[...], wo_ref[s5w:, :])
    x1_ref[...] = x1
    q_ref[...] = _dot(_rms(x1, g_ref[...]), wq_ref[...]).astype(bf16)


def _outproj(x, s5y, rwy, w_out, g_ca, w_cq):
    rows, d = x.shape
    s5w = s5y.shape[1]
    rww = rwy.shape[1]
    tm = _tile(rows)
    rt = lambda c: pl.BlockSpec((tm, c), lambda i: (i, 0))
    return pl.pallas_call(
        functools.partial(_outproj_kernel, s5w=s5w),
        grid=(rows // tm,),
        in_specs=[rt(d), rt(s5w), rt(rww), _full(w_out.shape), _full((1, d)), _full(w_cq.shape)],
        out_specs=[rt(d), rt(d)],
        out_shape=[jax.ShapeDtypeStruct((rows, d), f32), jax.ShapeDtypeStruct((rows, d), bf16)],
        compiler_params=_params("parallel"),
        name="outproj_q",
    )(x, s5y, rwy, w_out, g_ca.reshape(1, d), w_cq)


def _kv_to_head_tiles(x):
    b, m, h, hd = x.shape
    return x.reshape(b, m, h, hd // LANES, LANES).transpose(0, 1, 3, 2, 4).reshape(b * m * (hd // LANES) * h, LANES)


def _kv_from_head_tiles(x, b, m, h, hd):
    return x.reshape(b, m, hd // LANES, h, LANES).transpose(0, 1, 3, 2, 4).reshape(b, m, h, hd)


def _memkv_kernel(m_ref, g_ref, w_ref, k_ref, v_ref):
    tm, d = m_ref.shape
    hd = d // MEM_HEADS
    parts = hd // LANES
    per_token = parts * MEM_HEADS
    nb = _rms(m_ref[...], g_ref[...]).astype(bf16)
    for ref, cols in ((k_ref, slice(0, d)), (v_ref, slice(d, 2 * d))):
        kv = jnp.dot(nb, w_ref[:, cols], preferred_element_type=f32)
        for h in range(MEM_HEADS):
            for c in range(parts):
                lo = h * hd + c * LANES
                ref[pl.ds(c * MEM_HEADS + h, tm, stride=per_token), :] = kv[:, lo:lo + LANES]


def _memkv(mem, g, w):
    rows, d = mem.shape
    tm = _tile(rows)
    per_token = d // LANES
    out = pl.BlockSpec((tm * per_token, LANES), lambda i: (i, 0))
    return pl.pallas_call(
        _memkv_kernel,
        grid=(rows // tm,),
        in_specs=[pl.BlockSpec((tm, d), lambda i: (i, 0)), _full((1, d)), _full(w.shape)],
        out_specs=[out, out],
        out_shape=[jax.ShapeDtypeStruct((rows * per_token, LANES), f32)] * 2,
        compiler_params=_params("parallel"),
        name="memory_kv",
    )(mem, g.reshape(1, d), w)


def _attn_kernel(q_ref, x1_ref, k_ref, v_ref, wo_ref, x2_ref, *, nbb):
    d = q_ref.shape[-1]
    hd = d // MEM_HEADS
    parts = hd // LANES
    per_token = parts * MEM_HEADS
    n_mem = k_ref.shape[0] // (per_token * nbb)
    rows = q_ref.shape[0] // nbb
    scale = hd ** -0.5
    q = q_ref[...]

    def head(ref, bb, h):
        first = bb * n_mem * per_token
        return jnp.concatenate([ref[pl.ds(first + c * MEM_HEADS + h, n_mem, stride=per_token), :].astype(bf16)
                                for c in range(parts)], axis=-1)

    heads = []
    for h in range(MEM_HEADS):
        s = [_dg(q[bb * rows:(bb + 1) * rows, h * hd:(h + 1) * hd], head(k_ref, bb, h), _NT) * scale
             for bb in range(nbb)]
        e = [jnp.exp(x - jnp.max(x, axis=-1, keepdims=True)) for x in s]
        prob = [x / jnp.sum(x, axis=-1, keepdims=True) for x in e]
        o = [_dot(prob[bb], head(v_ref, bb, h)) for bb in range(nbb)]
        heads.append(jnp.concatenate(o, axis=0) if nbb > 1 else o[0])
    o = jnp.concatenate(heads, axis=-1)
    x2_ref[...] = x1_ref[...] + _dot(o, wo_ref[...])


def _attn(q, x1, mk, mv, w_co, t, b):
    d = q.shape[-1]
    rows_per_b = mk.shape[0] // b
    tt = min(ROW_TILE, t)
    assert t % tt == 0
    nt = t // tt
    nbb = min(ATTN_SHORT_BATCH, b) if (nt == 1 and tt * ATTN_SHORT_BATCH <= ROW_TILE) else 1
    assert b % nbb == 0
    xt = pl.BlockSpec((nbb * tt, d), lambda j, i: (j * nt + i, 0))
    kv = pl.BlockSpec((nbb * rows_per_b, LANES), lambda j, i: (j, 0))
    return pl.pallas_call(
        functools.partial(_attn_kernel, nbb=nbb),
        grid=(b // nbb, nt),
        in_specs=[xt, xt, kv, kv, _full(w_co.shape)],
        out_specs=xt,
        out_shape=jax.ShapeDtypeStruct((b * t, d), f32),
        compiler_params=_params("parallel", "parallel"),
        name="mem_attention",
    )(q, x1, mk, mv, w_co)


def _two_group_specs(tm, cols, na_tiles):
    a = pl.BlockSpec((tm, cols), lambda i, *_: (jnp.minimum(i, na_tiles - 1), 0))
    b = pl.BlockSpec((tm, cols), lambda i, *_: (jnp.maximum(i - na_tiles, 0), 0))
    return a, b


def _split2(x):
    hi = x.astype(bf16)
    return hi, (x - hi.astype(f32)).astype(bf16)


def _token_chunk(j, rows, nchunk, first=0):
    return pl.ds(first * nchunk + j, rows, stride=nchunk)


def _to_token_tiles(ref, x, first=0):
    rows, d = x.shape
    nchunk = d // LANES
    for j in range(nchunk):
        ref[_token_chunk(j, rows, nchunk, first), :] = x[:, j * LANES:(j + 1) * LANES]


def _token_rows(t, nchunk):
    return pl.ds(pl.multiple_of(t * nchunk, nchunk), nchunk)


def _router_kernel(xa_ref, xb_ref, g_ref, rw_ref, rb_ref, ht_ref, e_ref, rank_ref, gate_ref, cnt_ref,
                   *, n_exp, na_tiles):
    i = pl.program_id(0)
    tm, d = xa_ref.shape

    @pl.when(i == 0)
    def _():
        cnt_ref[...] = jnp.zeros_like(cnt_ref)

    h = _rms(jnp.where(i < na_tiles, xa_ref[...], xb_ref[...]), g_ref[...])
    _to_token_tiles(ht_ref, h)

    hh, hl = _split2(h)
    wh, wl = _split2(rw_ref[...])
    logits = (jnp.dot(hh, wh, preferred_element_type=f32) + jnp.dot(hh, wl, preferred_element_type=f32)
              + jnp.dot(hl, wh, preferred_element_type=f32) + rb_ref[...])

    lane = lax.broadcasted_iota(jnp.int32, (tm, n_exp), 1)
    work = logits
    sels, tops = [], []
    for _ in range(TOP_K):
        m = jnp.max(work, axis=-1, keepdims=True)
        idx = jnp.min(jnp.where(work == m, lane, n_exp), axis=-1, keepdims=True)
        sel = lane == idx
        sels.append(sel)
        tops.append(m)
        work = jnp.where(sel, -jnp.inf, work)
    chosen = sels[0] | sels[1] | sels[2] | sels[3]

    onehot = jnp.where(chosen, 1.0, 0.0)
    rr = lax.broadcasted_iota(jnp.int32, (tm, tm), 0)
    cc = lax.broadcasted_iota(jnp.int32, (tm, tm), 1)
    before = (cc < rr).astype(bf16)
    rank = jnp.dot(before, onehot.astype(bf16), preferred_element_type=f32) + cnt_ref[...]
    cnt_ref[...] = cnt_ref[...] + jnp.sum(onehot, axis=0, keepdims=True)

    den = sum(jnp.exp(t - tops[0]) for t in tops)
    e_cols, r_cols, g_cols = [], [], []
    for sel, top in zip(sels, tops):
        e_cols.append(jnp.sum(jnp.where(sel, lane, 0), axis=-1, keepdims=True))
        r_cols.append(jnp.sum(jnp.where(sel, rank, 0.0), axis=-1, keepdims=True))
        g_cols.append(jnp.exp(top - tops[0]) / den)
    e_ref[...] = jnp.concatenate(e_cols, axis=-1)
    rank_ref[...] = jnp.concatenate(r_cols, axis=-1).astype(jnp.int32)
    gate_ref[...] = jnp.concatenate(g_cols, axis=-1)


def _router(xa, xb, g, rw, rb):
    d = xa.shape[1]
    n = xa.shape[0] + xb.shape[0]
    n_exp = rw.shape[1]
    tm = ROW_TILE
    assert xa.shape[0] % tm == 0 and xb.shape[0] % tm == 0
    na_tiles = xa.shape[0] // tm
    sa, sb = _two_group_specs(tm, d, na_tiles)
    rt = lambda c: pl.BlockSpec((tm, c), lambda i: (i, 0))
    return pl.pallas_call(
        functools.partial(_router_kernel, n_exp=n_exp, na_tiles=na_tiles),
        grid=(n // tm,),
        in_specs=[sa, sb, _full((1, d)), _full((d, n_exp)), _full((1, n_exp))],
        out_specs=[pl.BlockSpec((tm * (d // LANES), LANES), lambda i: (i, 0)), rt(TOP_K), rt(TOP_K), rt(TOP_K),
                   _full((1, n_exp))],
        out_shape=[jax.ShapeDtypeStruct((n * (d // LANES), LANES), f32), jax.ShapeDtypeStruct((n, TOP_K), jnp.int32),
                   jax.ShapeDtypeStruct((n, TOP_K), jnp.int32), jax.ShapeDtypeStruct((n, TOP_K), f32),
                   jax.ShapeDtypeStruct((1, n_exp), f32)],
        compiler_params=_params("arbitrary"),
        name="moe_router",
    )(xa, xb, g.reshape(1, d), rw.astype(f32), rb.reshape(1, n_exp).astype(f32))


def _invert_kernel(dest_ref, zs_ref, ze_ref, tok_ref, *, n):
    def for_each_unmapped_row(fn):
        def segment(seg, c):
            return lax.fori_loop(zs_ref[seg], ze_ref[seg], fn, c)
        lax.fori_loop(0, zs_ref.shape[0], segment, 0)

    fill = lambda r, c: _set(tok_ref, r, 0, c)
    for_each_unmapped_row(fill)
    for kk in range(TOP_K):
        def place(t, c, kk=kk):
            return _set(tok_ref, dest_ref[kk * n + t], t, c)
        lax.fori_loop(0, n, place, 0, unroll=8)


def _set(ref, idx, val, carry):
    ref[idx] = val
    return carry


def _invert(dest_km, zero_start, zero_end, n, n_rows):
    grid_spec = pltpu.PrefetchScalarGridSpec(
        num_scalar_prefetch=3, grid=(1,), in_specs=[],
        out_specs=pl.BlockSpec(memory_space=pltpu.SMEM))
    return pl.pallas_call(
        functools.partial(_invert_kernel, n=n),
        grid_spec=grid_spec,
        out_shape=jax.ShapeDtypeStruct((n_rows,), jnp.int32),
        compiler_params=_params("arbitrary"),
        name="moe_invert",
    )(dest_km, zero_start, zero_end)


def _expert_kernel(be_ref, nv_ref, nxt_ref, tok_ref, ht_hbm, wgu_hbm, bgu_ref, wd_hbm, bd_ref, y_ref,
                   xbuf, xsem, wgu_f, wd_f, wsem, wgu_b, wd_b, *, layer, blk):
    i = pl.program_id(0)
    slot = i % 2
    nvalid = nv_ref[0]
    dff = wd_b.shape[0]
    nchunk = wgu_b.shape[0] // LANES
    last_block = pl.num_programs(0) - 1

    def gather(block, into):
        for r in range(blk):
            tok = tok_ref[block * blk + r]
            pltpu.make_async_copy(ht_hbm.at[_token_rows(tok, nchunk)], xbuf.at[into, _token_rows(r, nchunk)],
                                  xsem.at[into]).start(priority=0)

    def weight_copies(e):
        return (pltpu.make_async_copy(wgu_hbm.at[layer, e], wgu_f, wsem.at[0]),
                pltpu.make_async_copy(wd_hbm.at[layer, e], wd_f, wsem.at[1]))

    @pl.when(i == 0)
    def _():
        gather(0, 0)
        for cp in weight_copies(be_ref[0]):
            cp.start(priority=1)

    @pl.when((i < nvalid) & ((i == 0) | (be_ref[i] != be_ref[jnp.maximum(i - 1, 0)])))
    def _():
        for cp in weight_copies(be_ref[i]):
            cp.wait()
        wgu_b[...] = wgu_f[...].astype(bf16)
        wd_b[...] = wd_f[...].astype(bf16)

        @pl.when(nxt_ref[i] >= 0)
        def _():
            for cp in weight_copies(nxt_ref[i]):
                cp.start(priority=1)

    @pl.when(i < nvalid)
    def _():
        pltpu.make_async_copy(xbuf.at[slot], xbuf.at[slot], xsem.at[slot]).wait()
        gather(jnp.minimum(i + 1, last_block), 1 - slot)
        x = jnp.concatenate([xbuf[slot, _token_chunk(j, blk, nchunk), :].astype(bf16) for j in range(nchunk)],
                            axis=-1)
        hgu = jnp.dot(x, wgu_b[...], preferred_element_type=f32) + bgu_ref[0, 0]
        hg = jnp.minimum(hgu[:, :dff], SWIGLU_LIMIT)
        hu = jnp.clip(hgu[:, dff:], -SWIGLU_LIMIT, SWIGLU_LIMIT)
        act = (hu + 1.0) * hg * jax.nn.sigmoid(SWIGLU_ALPHA * hg)
        _to_token_tiles(y_ref, _dot(act, wd_b[...]) + bd_ref[0, 0])

    @pl.when(i >= nvalid)
    def _():
        y_ref[...] = jnp.zeros_like(y_ref)

    @pl.when(i == nvalid)
    def _():
        pltpu.make_async_copy(xbuf.at[slot], xbuf.at[slot], xsem.at[slot]).wait()


def _experts(ht, row_tok, block_e, nvalid, next_e, w_gu, b_gu, w_down, b_down, l, blk):
    depth, n_exp, d, dff2 = w_gu.shape
    dff = dff2 // 2
    nchunk = d // LANES
    n_rows = row_tok.shape[0]
    bspec = lambda cols: pl.BlockSpec((1, 1, 1, cols), lambda i, be, *_: (l, be[i], 0, 0))
    grid_spec = pltpu.PrefetchScalarGridSpec(
        num_scalar_prefetch=4,
        grid=(n_rows // blk,),
        in_specs=[pl.BlockSpec(memory_space=pl.ANY), pl.BlockSpec(memory_space=pl.ANY), bspec(dff2),
                  pl.BlockSpec(memory_space=pl.ANY), bspec(d)],
        out_specs=pl.BlockSpec((blk * nchunk, LANES), lambda i, *_: (i, 0)),
        scratch_shapes=[pltpu.VMEM((2, blk * nchunk, LANES), f32), pltpu.SemaphoreType.DMA((2,)),
                        pltpu.VMEM((d, dff2), f32), pltpu.VMEM((dff, d), f32), pltpu.SemaphoreType.DMA((2,)),
                        pltpu.VMEM((d, dff2), bf16), pltpu.VMEM((dff, d), bf16)],
    )
    return pl.pallas_call(
        functools.partial(_expert_kernel, layer=l, blk=blk),
        grid_spec=grid_spec,
        out_shape=jax.ShapeDtypeStruct((n_rows * nchunk, LANES), f32),
        compiler_params=_params("arbitrary"),
        name="moe_experts",
    )(block_e, nvalid, next_e, row_tok, ht, w_gu, b_gu.reshape(depth, n_exp, 1, dff2), w_down,
      b_down.reshape(depth, n_exp, 1, d))


def _combine_kernel(dest_ref, gate_ref, xa_ref, xb_ref, yb_hbm, g_ref, oa_ref, ob_ref, ybuf, acc_s, sem,
                    *, tm, n, na_tiles, final):
    i = pl.program_id(0)
    slot = i % 2
    nchunk = ybuf.shape[2] // tm
    last_tile = pl.num_programs(0) - 1

    def copy_row(tile, into, kk, t, prio):
        row = dest_ref[kk * n + tile * tm + t]
        pltpu.make_async_copy(yb_hbm.at[_token_rows(row, nchunk)], ybuf.at[into, kk, _token_rows(t, nchunk)],
                              sem.at[into]).start(priority=prio)

    @pl.when(i == 0)
    def _():
        for kk in range(TOP_K):
            def issue(pair, c, kk=kk):
                for prio in range(DMA_PRIORITIES):
                    copy_row(0, 0, kk, pair * DMA_PRIORITIES + prio, prio)
                return c
            lax.fori_loop(0, tm // DMA_PRIORITIES, issue, 0, unroll=4)

    pltpu.make_async_copy(ybuf.at[slot], ybuf.at[slot], sem.at[slot]).wait()
    nxt = jnp.minimum(i + 1, last_tile)
    for kk in range(TOP_K):
        for t in range(tm):
            copy_row(nxt, 1 - slot, kk, t, t % DMA_PRIORITIES)
    for t in range(tm):
        first = (i * tm + t) * TOP_K
        acc = gate_ref[first] * ybuf[slot, 0, t * nchunk:(t + 1) * nchunk, :]
        for kk in range(1, TOP_K):
            acc = acc + gate_ref[first + kk] * ybuf[slot, kk, t * nchunk:(t + 1) * nchunk, :]
        acc_s[t * nchunk:(t + 1) * nchunk, :] = acc
    moe = jnp.concatenate([acc_s[_token_chunk(j, tm, nchunk), :] for j in range(nchunk)], axis=-1)
    x3 = jnp.where(i < na_tiles, xa_ref[...], xb_ref[...]) + moe
    out = _rms(x3, g_ref[...]) if final else x3

    @pl.when(i < na_tiles)
    def _():
        oa_ref[...] = out

    @pl.when(i >= na_tiles)
    def _():
        ob_ref[...] = out

    @pl.when(i == last_tile)
    def _():
        pltpu.make_async_copy(ybuf.at[1 - slot], ybuf.at[1 - slot], sem.at[1 - slot]).wait()


def _combine(dest_km, xa, xb, gate, yb, g_final, final):
    d = xa.shape[1]
    n = xa.shape[0] + xb.shape[0]
    tm = COMBINE_TILE
    assert xa.shape[0] % tm == 0 and xb.shape[0] % tm == 0
    na_tiles = xa.shape[0] // tm
    sa, sb = _two_group_specs(tm, d, na_tiles)
    nchunk = d // LANES
    grid_spec = pltpu.PrefetchScalarGridSpec(
        num_scalar_prefetch=2,
        grid=(n // tm,),
        in_specs=[sa, sb, pl.BlockSpec(memory_space=pl.ANY), pl.BlockSpec((1, d), lambda i, *_: (0, 0))],
        out_specs=[sa, sb],
        scratch_shapes=[pltpu.VMEM((2, TOP_K, tm * nchunk, LANES), f32), pltpu.VMEM((tm * nchunk, LANES), f32),
                        pltpu.SemaphoreType.DMA((2,))],
    )
    return pl.pallas_call(
        functools.partial(_combine_kernel, tm=tm, n=n, na_tiles=na_tiles, final=final),
        grid_spec=grid_spec,
        out_shape=[jax.ShapeDtypeStruct(xa.shape, f32), jax.ShapeDtypeStruct(xb.shape, f32)],
        compiler_params=_params("arbitrary"),
        name="moe_combine",
    )(dest_km, gate.reshape(-1), xa, xb, yb, g_final.reshape(1, d))


def _moe(xa, xb, g_ffn, router_w, router_b, w_gu, b_gu, w_down, b_down, l, g_out, final):
    n = xa.shape[0] + xb.shape[0]
    n_exp = router_w.shape[1]
    blk = MOE_ROWS
    ht, top_e, top_rank, gate, counts = _router(xa, xb, g_ffn, router_w, router_b)

    counts = counts.reshape(n_exp).astype(jnp.int32)
    padded = (counts + blk - 1) // blk * blk
    pad_end = jnp.cumsum(padded)
    pad_start = pad_end - padded
    dest_km = (pad_start[top_e] + top_rank).T.reshape(-1)
    n_rows = -(-(n * TOP_K) // blk) * blk + n_exp * blk
    block_start = jnp.arange(n_rows // blk, dtype=jnp.int32) * blk
    block_e = jnp.minimum(jnp.sum(pad_end[None, :] <= block_start[:, None], axis=1), n_exp - 1).astype(jnp.int32)
    nvalid = (pad_end[-1:] // blk).astype(jnp.int32)
    zero_start = jnp.concatenate([pad_start + counts, pad_end[-1:]]).astype(jnp.int32)
    zero_end = jnp.concatenate([pad_end, jnp.full((1,), n_rows, jnp.int32)]).astype(jnp.int32)

    has_rows = padded > 0
    ids = jnp.arange(n_exp, dtype=jnp.int32)
    after = jnp.where(has_rows[None, :] & (ids[None, :] > ids[:, None]), ids[None, :], n_exp)
    next_of = jnp.min(after, axis=1)
    next_e = jnp.where(next_of < n_exp, next_of, -1).astype(jnp.int32)[block_e]

    row_tok = _invert(dest_km, zero_start, zero_end, n, n_rows)
    yb = _experts(ht, row_tok, block_e, nvalid, next_e, w_gu, b_gu, w_down, b_down, l, blk)
    return _combine(dest_km, xa, xb, gate, yb, g_out, final)


def _mixers(x, t, b, h0r, h0i, prev, s0, P, l):
    s5w = P['s5_d'].shape[-1]
    u, prk = _inproj(x, P['norm_mix_g'][l], P['w_in'][l].astype(bf16), s5w)
    ar, ai, bre, bim, cre_m, cim_m = _s5_discretize(P['s5_lambda_re'][l], P['s5_lambda_im'][l], P['s5_log_dt'][l],
                                                    P['s5_b_re'][l], P['s5_b_im'][l], P['s5_c_re'][l], P['s5_c_im'][l])
    s5y, hr, hi = _s5(jnp.transpose(u.reshape(b, t, s5w), (1, 0, 2)), h0r, h0i, ar, ai, bre, bim, cre_m, cim_m,
                      P['s5_d'][l].reshape(1, s5w).astype(f32), P['s5_glu_w'][l].astype(bf16),
                      P['s5_glu_b'][l].reshape(1, s5w).astype(f32))
    s5y = jnp.transpose(s5y, (1, 0, 2)).reshape(b * t, s5w)
    rwy, s_fin = _rwkv(prk, prev, s0, t, b, P['rwkv_mu'][l], P['rwkv_w0'][l], P['rwkv_w1'][l], P['rwkv_w2'][l],
                       P['rwkv_a0'][l], P['rwkv_a1'][l], P['rwkv_a2'][l], P['rwkv_g1'][l], P['rwkv_g2'][l],
                       P['rwkv_k_k'][l], P['rwkv_k_a'][l], P['rwkv_r_k'][l], P['rwkv_ln_w'][l], P['rwkv_ln_b'][l])
    x1, q = _outproj(x, s5y, rwy, P['w_out'][l].astype(bf16), P['norm_ca_g'][l], P['w_cq'][l].astype(bf16))
    shift_last = prk.reshape(b, t, -1)[:, t - 1]
    return x1, q, hr, hi, shift_last, s_fin


def kernel(x_prompt, x_sample, mem_prompt, state_s5_re, state_s5_im, state_rwkv_shift, state_rwkv_wkv, cache_mem_k, cache_mem_v, norm_mix_g, w_in, s5_lambda_re, s5_lambda_im, s5_log_dt, s5_b_re, s5_b_im, s5_c_re, s5_c_im, s5_d, s5_glu_w, s5_glu_b, rwkv_mu, rwkv_w0, rwkv_w1, rwkv_w2, rwkv_a0, rwkv_a1, rwkv_a2, rwkv_g1, rwkv_g2, rwkv_k_k, rwkv_k_a, rwkv_r_k, rwkv_ln_w, rwkv_ln_b, w_out, norm_ca_g, norm_mem_g, w_cq, w_mem_kv, w_co, norm_ffn_g, router_w, router_b, ex_w_gu, ex_b_gu, ex_w_down, ex_b_down, norm_final_g):
    P = dict(norm_mix_g=norm_mix_g, w_in=w_in, s5_lambda_re=s5_lambda_re, s5_lambda_im=s5_lambda_im,
             s5_log_dt=s5_log_dt, s5_b_re=s5_b_re, s5_b_im=s5_b_im, s5_c_re=s5_c_re, s5_c_im=s5_c_im,
             s5_d=s5_d, s5_glu_w=s5_glu_w, s5_glu_b=s5_glu_b, rwkv_mu=rwkv_mu, rwkv_w0=rwkv_w0,
             rwkv_w1=rwkv_w1, rwkv_w2=rwkv_w2, rwkv_a0=rwkv_a0, rwkv_a1=rwkv_a1, rwkv_a2=rwkv_a2,
             rwkv_g1=rwkv_g1, rwkv_g2=rwkv_g2, rwkv_k_k=rwkv_k_k, rwkv_k_a=rwkv_k_a, rwkv_r_k=rwkv_r_k,
             rwkv_ln_w=rwkv_ln_w, rwkv_ln_b=rwkv_ln_b, w_out=w_out, norm_ca_g=norm_ca_g, w_cq=w_cq)
    depth = norm_mix_g.shape[0]
    bp, tp, d = x_prompt.shape
    bs, ts, _ = x_sample.shape
    n_mem = mem_prompt.shape[1]
    s5_groups, s5_state = s5_lambda_re.shape[1:]
    nstate = s5_groups * s5_state
    rw_in = state_rwkv_shift.shape[-1]
    heads = state_rwkv_wkv.shape[2]
    mh = cache_mem_k.shape[3]
    assert mh == MEM_HEADS

    xp = x_prompt.reshape(bp * tp, d)
    xs = x_sample.reshape(bs * ts, d)
    outs = {k: [] for k in ('p_re', 'p_im', 'p_sh', 'p_wkv', 'p_mk', 'p_mv', 's_re', 's_im', 's_sh', 's_wkv')}
    for l in range(depth):
        mk, mv = _memkv(mem_prompt.reshape(bp * n_mem, d), norm_mem_g[l], w_mem_kv[l].astype(bf16))
        zeros_state = jnp.zeros((bp, nstate), f32)
        x1p, qp, re_p, im_p, sh_p, wkv_p = _mixers(
            xp, tp, bp, zeros_state, zeros_state, jnp.zeros((bp, rw_in), f32),
            jnp.zeros((bp, heads, RWKV_HEAD, RWKV_HEAD), f32), P, l)
        x1s, qs, re_s, im_s, sh_s, wkv_s = _mixers(
            xs, ts, bs, state_s5_re[l].reshape(bs, nstate), state_s5_im[l].reshape(bs, nstate),
            state_rwkv_shift[l], state_rwkv_wkv[l], P, l)
        wco = w_co[l].astype(bf16)
        x2p = _attn(qp, x1p, mk, mv, wco, tp, bp)
        x2s = _attn(qs, x1s, _kv_to_head_tiles(cache_mem_k[l]), _kv_to_head_tiles(cache_mem_v[l]), wco, ts, bs)
        last = l == depth - 1
        xp, xs = _moe(x2p, x2s, norm_ffn_g[l], router_w[l], router_b[l], ex_w_gu, ex_b_gu, ex_w_down, ex_b_down,
                      l, norm_final_g, last)
        outs['p_re'].append(re_p.reshape(bp, s5_groups, s5_state))
        outs['p_im'].append(im_p.reshape(bp, s5_groups, s5_state))
        outs['p_sh'].append(sh_p)
        outs['p_wkv'].append(wkv_p)
        outs['p_mk'].append(_kv_from_head_tiles(mk, bp, n_mem, mh, d // mh))
        outs['p_mv'].append(_kv_from_head_tiles(mv, bp, n_mem, mh, d // mh))
        outs['s_re'].append(re_s.reshape(bs, s5_groups, s5_state))
        outs['s_im'].append(im_s.reshape(bs, s5_groups, s5_state))
        outs['s_sh'].append(sh_s)
        outs['s_wkv'].append(wkv_s)
    st = lambda k: jnp.stack(outs[k])
    return (xp.reshape(bp, tp, d), xs.reshape(bs, ts, d), st('p_re'), st('p_im'), st('p_sh'), st('p_wkv'),
            st('p_mk'), st('p_mv'), st('s_re'), st('s_im'), st('s_sh'), st('s_wkv'))
```

```python
import functools

import jax
import jax.numpy as jnp
from jax import lax
from jax.experimental import pallas as pl
from jax.experimental.pallas import tpu as pltpu

f32 = jnp.float32
bf16 = jnp.bfloat16

RWKV_HEAD = 64
HEAD_PAIR = 2 * RWKV_HEAD
DECAY_SCALE = 0.6065306597
MEM_HEADS = 4
TOP_K = 4
SWIGLU_LIMIT = 7.0
SWIGLU_ALPHA = 1.702
RMS_EPS = 1e-5
GN_EPS = 64e-5

V7X_VMEM_LIMIT = 56 * 1024 * 1024
ROW_TILE = 512
MOE_ROWS = 256
COMBINE_TILE = 256
WKV_CHUNK = 64
WKV_BATCH = 4
S5_CHUNK = 64
S5_BLOCK_CHANNELS = 128
LANES = 128
DMA_PRIORITIES = 2
ATTN_SHORT_BATCH = 4

_NT = (((1,), (1,)), ((), ()))
_TN = (((0,), (0,)), ((), ()))


def _params(*sem):
    return pltpu.CompilerParams(dimension_semantics=sem, vmem_limit_bytes=V7X_VMEM_LIMIT)


def _dot(a, b):
    return jnp.dot(a.astype(bf16), b.astype(bf16), preferred_element_type=f32)


def _dg(a, b, dims):
    return lax.dot_general(a.astype(bf16), b.astype(bf16), dims, preferred_element_type=f32)


def _rms(x, g):
    return x * lax.rsqrt(jnp.mean(x * x, axis=-1, keepdims=True) + RMS_EPS) * g


def _full(shape):
    n = len(shape)
    return pl.BlockSpec(shape, lambda *_: (0,) * n)


def _tile(rows, tile=ROW_TILE):
    t = min(tile, rows)
    assert rows % t == 0
    return t


def _inproj_kernel(x_ref, g_ref, w_ref, u_ref, p_ref, *, s5w):
    nb = _rms(x_ref[...], g_ref[...]).astype(bf16)
    u_ref[...] = jnp.dot(nb, w_ref[:, :s5w], preferred_element_type=f32)
    p_ref[...] = jnp.dot(nb, w_ref[:, s5w:], preferred_element_type=f32)


def _inproj(x, g, w, s5w):
    rows, d = x.shape
    inw = w.shape[1]
    tm = _tile(rows)
    return pl.pallas_call(
        functools.partial(_inproj_kernel, s5w=s5w),
        grid=(rows // tm,),
        in_specs=[pl.BlockSpec((tm, d), lambda i: (i, 0)), _full((1, d)), _full((d, inw))],
        out_specs=[pl.BlockSpec((tm, s5w), lambda i: (i, 0)), pl.BlockSpec((tm, inw - s5w), lambda i: (i, 0))],
        out_shape=[jax.ShapeDtypeStruct((rows, s5w), f32), jax.ShapeDtypeStruct((rows, inw - s5w), f32)],
        compiler_params=_params("parallel"),
        name="inproj",
    )(x, g.reshape(1, d), w)


def _s5_kernel(u_hbm, h0r_ref, h0i_ref, ar_ref, ai_ref, bre_ref, bim_ref, cre_ref, cim_ref, d_ref, gw_ref, gb_ref,
               y_hbm, hr_ref, hi_ref, xr_s, xi_s, ubuf, usem, ybuf, ysem, *, steps, bt, lane_chunk):
    batch_tile = pl.program_id(0)
    i = pl.program_id(1)
    nchunks = pl.num_programs(1)
    slot = i % 2
    width = ubuf.shape[-1]
    nstate = ar_ref.shape[-1]

    def chunk_copies(chunk, s, load):
        t0 = pl.multiple_of(chunk * steps, steps)
        for b in range(bt):
            hbm = (u_hbm if load else y_hbm).at[batch_tile * bt + b, pl.ds(t0, steps), :]
            if load:
                yield pltpu.make_async_copy(hbm, ubuf.at[s, :, b, :], usem.at[s])
            else:
                yield pltpu.make_async_copy(ybuf.at[s, :, b, :], hbm, ysem.at[s])

    @pl.when(i == 0)
    def _():
        hr_ref[...] = h0r_ref[...]
        hi_ref[...] = h0i_ref[...]
        for cp in chunk_copies(0, 0, True):
            cp.start()

    @pl.when(i + 1 < nchunks)
    def _():
        for cp in chunk_copies(i + 1, 1 - slot, True):
            cp.start()

    @pl.when(i >= 2)
    def _():
        for cp in chunk_copies(i - 2, slot, False):
            cp.wait()

    for cp in chunk_copies(i, slot, True):
        cp.wait()
    u = ubuf[slot].reshape(steps * bt, width)
    ub = u.astype(bf16)
    nblk, cb, sb = bre_ref.shape
    for j in range(nblk):
        uj = ub[:, j * cb:(j + 1) * cb]
        xr_s[:, j * sb:(j + 1) * sb] = jnp.dot(uj, bre_ref[j], preferred_element_type=f32)
        xi_s[:, j * sb:(j + 1) * sb] = jnp.dot(uj, bim_ref[j], preferred_element_type=f32)

    for c in range(nstate // lane_chunk):
        sl = slice(c * lane_chunk, (c + 1) * lane_chunk)
        arb = jnp.broadcast_to(ar_ref[:, sl], (bt, lane_chunk))
        aib = jnp.broadcast_to(ai_ref[:, sl], (bt, lane_chunk))

        def step(t, carry, sl=sl, arb=arb, aib=aib):
            hr, hi = carry
            r0 = pl.multiple_of(t * bt, bt)
            nhr = arb * hr - aib * hi + xr_s[pl.ds(r0, bt), sl]
            nhi = arb * hi + aib * hr + xi_s[pl.ds(r0, bt), sl]
            xr_s[pl.ds(r0, bt), sl] = nhr
            xi_s[pl.ds(r0, bt), sl] = nhi
            return nhr, nhi

        hr, hi = lax.fori_loop(0, steps, step, (hr_ref[:, sl], hi_ref[:, sl]), unroll=2)
        hr_ref[:, sl] = hr
        hi_ref[:, sl] = hi

    y = jnp.concatenate(
        [jnp.dot(xr_s[:, j * sb:(j + 1) * sb].astype(bf16), cre_ref[j], preferred_element_type=f32)
         + jnp.dot(xi_s[:, j * sb:(j + 1) * sb].astype(bf16), cim_ref[j], preferred_element_type=f32)
         for j in range(nblk)], axis=-1) + d_ref[...] * u
    z = jax.nn.gelu(y)
    out = z * jax.nn.sigmoid(_dot(z, gw_ref[...]) + gb_ref[...])
    ybuf[slot] = out.reshape(steps, bt, width)
    for cp in chunk_copies(i, slot, False):
        cp.start()

    @pl.when(i == nchunks - 1)
    def _():
        for cp in chunk_copies(i, slot, False):
            cp.wait()

        @pl.when(i >= 1)
        def _():
            for cp in chunk_copies(i - 1, 1 - slot, False):
                cp.wait()


def _s5(u3, h0r, h0i, ar, ai, bre, bim, cre_m, cim_m, d, gw, gb):
    b, t, width = u3.shape
    nstate = ar.shape[-1]
    bt = 8
    steps = min(S5_CHUNK, t)
    assert b % bt == 0 and t % steps == 0
    kern = functools.partial(_s5_kernel, steps=steps, bt=bt, lane_chunk=512)
    st_spec = pl.BlockSpec((bt, nstate), lambda j, i: (j, 0))
    y, hr, hi = pl.pallas_call(
        kern,
        grid=(b // bt, t // steps),
        in_specs=[pl.BlockSpec(memory_space=pl.ANY), st_spec, st_spec,
                  _full((1, nstate)), _full((1, nstate)), _full(bre.shape), _full(bim.shape),
                  _full(cre_m.shape), _full(cim_m.shape), _full((1, width)), _full((width, width)),
                  _full((1, width))],
        out_specs=[pl.BlockSpec(memory_space=pl.ANY), st_spec, st_spec],
        out_shape=[jax.ShapeDtypeStruct((b, t, width), f32), jax.ShapeDtypeStruct((b, nstate), f32),
                   jax.ShapeDtypeStruct((b, nstate), f32)],
        scratch_shapes=[pltpu.VMEM((steps * bt, nstate), f32), pltpu.VMEM((steps * bt, nstate), f32),
                        pltpu.VMEM((2, steps, bt, width), f32), pltpu.SemaphoreType.DMA((2,)),
                        pltpu.VMEM((2, steps, bt, width), f32), pltpu.SemaphoreType.DMA((2,))],
        compiler_params=_params("arbitrary", "arbitrary"),
        name="s5_mixer",
    )(u3, h0r, h0i, ar, ai, bre, bim, cre_m, cim_m, d, gw, gb)
    return y, hr, hi


def _s5_discretize(lam_re, lam_im, log_dt, b_re, b_im, c_re, c_im):
    g, p = lam_re.shape
    dt = jnp.exp(log_dt.astype(f32))[:, None]
    lr = lam_re.astype(f32)
    li = lam_im.astype(f32)
    mag = jnp.exp(lr * dt)
    ar = mag * jnp.cos(li * dt)
    ai = mag * jnp.sin(li * dt)
    den = lr * lr + li * li
    cr = ((ar - 1.0) * lr + ai * li) / den
    ci = (ai * lr - (ar - 1.0) * li) / den
    br = b_re.astype(f32)
    bi = b_im.astype(f32)
    bbar_r = cr[..., None] * br - ci[..., None] * bi
    bbar_i = cr[..., None] * bi + ci[..., None] * br
    c = br.shape[-1]
    gb = S5_BLOCK_CHANNELS // c
    assert g % gb == 0
    eye = jnp.eye(gb, dtype=f32)
    nblk = g // gb
    bd_in = lambda m: jnp.einsum('ngpc,gh->ngchp', m.reshape(nblk, gb, p, c), eye).reshape(
        nblk, gb * c, gb * p).astype(bf16)
    bd_out = lambda m: jnp.einsum('ngcp,gh->ngphc', m.reshape(nblk, gb, c, p), eye).reshape(
        nblk, gb * p, gb * c).astype(bf16)
    return (ar.reshape(1, g * p), ai.reshape(1, g * p), bd_in(bbar_r), bd_in(bbar_i),
            bd_out(c_re.astype(f32)), bd_out(-c_im.astype(f32)))


def _rwkv_kernel(p_ref, prev_ref, s0_ref, mu_ref, w0_ref, w1_ref, w2_ref, a0_ref, a1_ref, a2_ref, g1_ref, g2_ref,
                 kk_ref, ka_ref, rk_ref, lnw_ref, lnb_ref, y_ref, sout_ref, prev_s, st_s, *, steps, nsq):
    i = pl.program_id(1)
    L = steps
    L2 = 2 * L
    nb = p_ref.shape[0]
    w = p_ref.shape[-1] // 4
    npair = w // HEAD_PAIR

    row2 = lax.broadcasted_iota(jnp.int32, (L2, HEAD_PAIR), 0)
    lane2 = lax.broadcasted_iota(jnp.int32, (L2, HEAD_PAIR), 1)
    own = (row2 < L) == (lane2 < RWKV_HEAD)
    ri = lax.broadcasted_iota(jnp.int32, (L2, L2), 0)
    ci = lax.broadcasted_iota(jnp.int32, (L2, L2), 1)
    same = (ri < L) == (ci < L)
    strict = same & (ci < ri)
    incl = same & (ci <= ri)
    eye = (ri == ci).astype(f32)
    sr = lax.broadcasted_iota(jnp.int32, (HEAD_PAIR, HEAD_PAIR), 0)
    sc = lax.broadcasted_iota(jnp.int32, (HEAD_PAIR, HEAD_PAIR), 1)
    sdiag = (sr < RWKV_HEAD) == (sc < RWKV_HEAD)

    @pl.when(i == 0)
    def _():
        prev_s[...] = prev_ref[...]
        for bb in range(nb):
            for pr in range(npair):
                sp = s0_ref[bb, pr * HEAD_PAIR:(pr + 1) * HEAD_PAIR, :]
                st_s[bb * npair + pr] = jnp.where(sdiag, jnp.concatenate([sp, sp], axis=1), 0.0)

    trow = lax.broadcasted_iota(jnp.int32, (L, 1), 0)
    ps, deltas = [], []
    for bb in range(nb):
        pb = p_ref[bb]
        shifted = jnp.where(trow == 0, prev_s[bb], pltpu.roll(pb, 1, axis=0))
        prev_s[bb] = pb[L - 1:L, :]
        ps.append(pb)
        deltas.append(shifted - pb)
    p = jnp.concatenate(ps, axis=0) if nb > 1 else ps[0]
    delta = jnp.concatenate(deltas, axis=0) if nb > 1 else deltas[0]
    mu = mu_ref[...]
    pz = p[:, 3 * w:]
    dz = delta[:, 3 * w:]
    r = p[:, :w] + delta[:, :w] * mu[0:1]
    k = p[:, w:2 * w] + delta[:, w:2 * w] * mu[1:2]
    v = p[:, 2 * w:3 * w] + delta[:, 2 * w:3 * w] * mu[2:3]
    zw = pz + dz * mu[3:4]
    za = pz + dz * mu[4:5]
    zg = pz + dz * mu[5:6]
    logw = -DECAY_SCALE * jax.nn.sigmoid(w0_ref[...] + _dot(jnp.tanh(_dot(zw, w1_ref[...])), w2_ref[...]))
    a = jax.nn.sigmoid(a0_ref[...] + _dot(_dot(za, a1_ref[...]), a2_ref[...]))
    g = _dot(jax.nn.sigmoid(_dot(zg, g1_ref[...])), g2_ref[...])
    kkraw = k * kk_ref[...]
    kmod = k * (1.0 + (a - 1.0) * ka_ref[...])
    rk = r * kmod * rk_ref[...]

    tr = lax.broadcasted_iota(jnp.int32, (nb * L, nb * L), 0)
    tc = lax.broadcasted_iota(jnp.int32, (nb * L, nb * L), 1)
    tri = ((tc <= tr) & (tc >= (tr // L) * L)).astype(bf16)
    hi = logw.astype(bf16)
    rem = logw - hi.astype(f32)
    mid = rem.astype(bf16)
    lo = (rem - mid.astype(f32)).astype(bf16)
    cum = (jnp.dot(tri, hi, preferred_element_type=f32) + jnp.dot(tri, mid, preferred_element_type=f32)
           + jnp.dot(tri, lo, preferred_element_type=f32))

    stack = lambda x: jnp.concatenate([x, x], axis=0)
    chains = [(bb, pr) for bb in range(nb) for pr in range(npair)]
    nch = range(len(chains))
    rs = [slice(bb * L, (bb + 1) * L) for bb, _ in chains]
    sls = [slice(pr * HEAD_PAIR, (pr + 1) * HEAD_PAIR) for _, pr in chains]
    cut = lambda x, c: stack(x[rs[c], sls[c]])
    own_of = lambda x, c: jnp.where(own, cut(x, c), 0.0)

    wt, rm, km, bm, kap, vmf, vm, gram = [], [], [], [], [], [], [], []
    for c in nch:
        cum_c = cut(cum, c)
        wt_c = jnp.exp(cum_c)
        winv = jnp.exp(-cum_c)
        wprev = jnp.exp(cum_c - cut(logw, c))
        kkr = own_of(kkraw, c)
        kkn = kkr * lax.rsqrt(jnp.maximum(jnp.sum(kkr * kkr, axis=-1, keepdims=True), 1e-24))
        wt.append(wt_c)
        rm.append((own_of(r, c) * wt_c).astype(bf16))
        km.append((own_of(kmod, c) * winv).astype(bf16))
        bm.append((kkn * cut(a, c) * winv).astype(bf16))
        kap.append((kkn * wprev).astype(bf16))
        vmf.append(own_of(v, c))
        vm.append(vmf[c].astype(bf16))
        gram.append(_dg(jnp.concatenate([kap[c], rm[c]], axis=0), jnp.concatenate([bm[c], km[c]], axis=0), _NT))
    n_mat = [jnp.where(strict, x[:L2, :L2], 0.0) for x in gram]
    p_mat = [jnp.where(strict, x[:L2, L2:], 0.0) for x in gram]
    g_mat = [jnp.where(incl, x[L2:, :L2], 0.0) for x in gram]
    q_mat = [jnp.where(incl, x[L2:, L2:], 0.0) for x in gram]

    neg = [-x for x in n_mat]
    inv = [eye + x for x in neg]
    if nsq > 0:
        pw = [_dot(x, x) for x in neg]
        for _ in range(nsq - 1):
            both = [_dot(pw[c], jnp.concatenate([inv[c], pw[c]], axis=1)) for c in nch]
            inv = [inv[c] + both[c][:, :L2] for c in nch]
            pw = [x[:, L2:] for x in both]
        inv = [inv[c] + _dot(pw[c], inv[c]) for c in nch]

    s_old = [st_s[c] for c in nch]
    rhs = [_dg(kap[c], s_old[c], _NT) + _dot(p_mat[c], vm[c]) for c in nch]
    ym0 = [_dg(rm[c], s_old[c], _NT) for c in nch]
    um = [(-_dot(inv[c], rhs[c])).astype(bf16) for c in nch]
    vu = [jnp.concatenate([vm[c], um[c]], axis=0) for c in nch]
    ym = [ym0[c] + _dot(jnp.concatenate([q_mat[c], g_mat[c]], axis=1), vu[c]) for c in nch]
    for c in nch:
        st_s[c] = (s_old[c] + _dg(vu[c], jnp.concatenate([km[c], bm[c]], axis=0), _TN)) * wt[c][L - 1:L, :]

    for c, (bb, pr) in enumerate(chains):
        mean = jnp.sum(ym[c], axis=-1, keepdims=True) * (1.0 / RWKV_HEAD)
        yc = jnp.where(own, ym[c] - mean, 0.0)
        var = jnp.sum(yc * yc, axis=-1, keepdims=True) * (1.0 / RWKV_HEAD)
        yn = yc * lax.rsqrt(var + GN_EPS)
        bonus = jnp.sum(own_of(rk, c), axis=-1, keepdims=True) * vmf[c]
        mix = (yn[:L] + yn[L:]) * lnw_ref[:, sls[c]] + lnb_ref[:, sls[c]] + (bonus[:L] + bonus[L:])
        y_ref[bb, :, sls[c]] = mix * g[rs[c], sls[c]]

    @pl.when(i == pl.num_programs(1) - 1)
    def _():
        for c, (bb, pr) in enumerate(chains):
            s_new = st_s[c]
            sout_ref[bb, pr * HEAD_PAIR:(pr + 1) * HEAD_PAIR, :] = s_new[:, :RWKV_HEAD] + s_new[:, RWKV_HEAD:]


def _rwkv(prk, prev, s0, t, b, mu, w0, w1, w2, a0, a1, a2, g1, g2, k_k, k_a, r_k, ln_w, ln_b):
    w = prk.shape[-1] // 4
    steps = min(WKV_CHUNK, t)
    nb = min(WKV_BATCH * max(1, WKV_CHUNK // (4 * steps)), b)
    assert t % steps == 0 and w % HEAD_PAIR == 0 and b % nb == 0
    nsq = max((steps - 1).bit_length() - 1, 0)
    row = lambda x: x.reshape(1, -1).astype(f32)
    wb = lambda x: x.astype(bf16)
    weights = [mu.astype(f32), row(w0), wb(w1), wb(w2), row(a0), wb(a1), wb(a2), wb(g1), wb(g2),
               row(k_k), row(k_a), row(r_k), row(ln_w), row(ln_b)]
    state_spec = pl.BlockSpec((nb, w, RWKV_HEAD), lambda j, i: (j, 0, 0))
    y, s_fin = pl.pallas_call(
        functools.partial(_rwkv_kernel, steps=steps, nsq=nsq),
        grid=(b // nb, t // steps),
        in_specs=[pl.BlockSpec((nb, steps, 4 * w), lambda j, i: (j, i, 0)),
                  pl.BlockSpec((nb, 1, 4 * w), lambda j, i: (j, 0, 0)), state_spec]
                 + [_full(x.shape) for x in weights],
        out_specs=[pl.BlockSpec((nb, steps, w), lambda j, i: (j, i, 0)), state_spec],
        out_shape=[jax.ShapeDtypeStruct((b, t, w), f32), jax.ShapeDtypeStruct((b, w, RWKV_HEAD), f32)],
        scratch_shapes=[pltpu.VMEM((nb, 1, 4 * w), f32),
                        pltpu.VMEM((nb * (w // HEAD_PAIR), HEAD_PAIR, HEAD_PAIR), f32)],
        compiler_params=_params("arbitrary", "arbitrary"),
        name="rwkv7_mixer",
    )(prk.reshape(b, t, 4 * w), prev.reshape(b, 1, 4 * w).astype(f32), s0.reshape(b, w, RWKV_HEAD).astype(f32),
      *weights)
    return y.reshape(b * t, w), s_fin.reshape(b, w // RWKV_HEAD, RWKV_HEAD, RWKV_HEAD)


def _outproj_kernel(x_ref, s5_ref, rw_ref, wo_ref, g_ref, wq_ref, x1_ref, q_ref, *, s5w):
    x1 = x_ref[...] + _dot(s5_ref[...], wo_ref[:s5w, :]) + _dot(rw_ref[...], wo_ref[s5w:, :])
    x1_ref[...] = x1
    q_ref[...] = _dot(_rms(x1, g_ref[...]), wq_ref[...]).astype(bf16)


def _outproj(x, s5y, rwy, w_out, g_ca, w_cq):
    rows, d = x.shape
    s5w = s5y.shape[1]
    rww = rwy.shape[1]
    tm = _tile(rows)
    rt = lambda c: pl.BlockSpec((tm, c), lambda i: (i, 0))
    return pl.pallas_call(
        functools.partial(_outproj_kernel, s5w=s5w),
        grid=(rows // tm,),
        in_specs=[rt(d), rt(s5w), rt(rww), _full(w_out.shape), _full((1, d)), _full(w_cq.shape)],
        out_specs=[rt(d), rt(d)],
        out_shape=[jax.ShapeDtypeStruct((rows, d), f32), jax.ShapeDtypeStruct((rows, d), bf16)],
        compiler_params=_params("parallel"),
        name="outproj_q",
    )(x, s5y, rwy, w_out, g_ca.reshape(1, d), w_cq)


def _kv_to_head_tiles(x):
    b, m, h, hd = x.shape
    return x.reshape(b, m, h, hd // LANES, LANES).transpose(0, 1, 3, 2, 4).reshape(b * m * (hd // LANES) * h, LANES)


def _kv_from_head_tiles(x, b, m, h, hd):
    return x.reshape(b, m, hd // LANES, h, LANES).transpose(0, 1, 3, 2, 4).reshape(b, m, h, hd)


def _memkv_kernel(m_ref, g_ref, w_ref, k_ref, v_ref):
    tm, d = m_ref.shape
    hd = d // MEM_HEADS
    parts = hd // LANES
    per_token = parts * MEM_HEADS
    nb = _rms(m_ref[...], g_ref[...]).astype(bf16)
    for ref, cols in ((k_ref, slice(0, d)), (v_ref, slice(d, 2 * d))):
        kv = jnp.dot(nb, w_ref[:, cols], preferred_element_type=f32)
        for h in range(MEM_HEADS):
            for c in range(parts):
                lo = h * hd + c * LANES
                ref[pl.ds(c * MEM_HEADS + h, tm, stride=per_token), :] = kv[:, lo:lo + LANES]


def _memkv(mem, g, w):
    rows, d = mem.shape
    tm = _tile(rows)
    per_token = d // LANES
    out = pl.BlockSpec((tm * per_token, LANES), lambda i: (i, 0))
    return pl.pallas_call(
        _memkv_kernel,
        grid=(rows // tm,),
        in_specs=[pl.BlockSpec((tm, d), lambda i: (i, 0)), _full((1, d)), _full(w.shape)],
        out_specs=[out, out],
        out_shape=[jax.ShapeDtypeStruct((rows * per_token, LANES), f32)] * 2,
        compiler_params=_params("parallel"),
        name="memory_kv",
    )(mem, g.reshape(1, d), w)


def _attn_kernel(q_ref, x1_ref, k_ref, v_ref, wo_ref, x2_ref, *, nbb):
    d = q_ref.shape[-1]
    hd = d // MEM_HEADS
    parts = hd // LANES
    per_token = parts * MEM_HEADS
    n_mem = k_ref.shape[0] // (per_token * nbb)
    rows = q_ref.shape[0] // nbb
    scale = hd ** -0.5
    q = q_ref[...]

    def head(ref, bb, h):
        first = bb * n_mem * per_token
        return jnp.concatenate([ref[pl.ds(first + c * MEM_HEADS + h, n_mem, stride=per_token), :].astype(bf16)
                                for c in range(parts)], axis=-1)

    heads = []
    for h in range(MEM_HEADS):
        s = [_dg(q[bb * rows:(bb + 1) * rows, h * hd:(h + 1) * hd], head(k_ref, bb, h), _NT) * scale
             for bb in range(nbb)]
        e = [jnp.exp(x - jnp.max(x, axis=-1, keepdims=True)) for x in s]
        prob = [x / jnp.sum(x, axis=-1, keepdims=True) for x in e]
        o = [_dot(prob[bb], head(v_ref, bb, h)) for bb in range(nbb)]
        heads.append(jnp.concatenate(o, axis=0) if nbb > 1 else o[0])
    o = jnp.concatenate(heads, axis=-1)
    x2_ref[...] = x1_ref[...] + _dot(o, wo_ref[...])


def _attn(q, x1, mk, mv, w_co, t, b):
    d = q.shape[-1]
    rows_per_b = mk.shape[0] // b
    tt = min(ROW_TILE, t)
    assert t % tt == 0
    nt = t // tt
    nbb = min(ATTN_SHORT_BATCH, b) if (nt == 1 and tt * ATTN_SHORT_BATCH <= ROW_TILE) else 1
    assert b % nbb == 0
    xt = pl.BlockSpec((nbb * tt, d), lambda j, i: (j * nt + i, 0))
    kv = pl.BlockSpec((nbb * rows_per_b, LANES), lambda j, i: (j, 0))
    return pl.pallas_call(
        functools.partial(_attn_kernel, nbb=nbb),
        grid=(b // nbb, nt),
        in_specs=[xt, xt, kv, kv, _full(w_co.shape)],
        out_specs=xt,
        out_shape=jax.ShapeDtypeStruct((b * t, d), f32),
        compiler_params=_params("parallel", "parallel"),
        name="mem_attention",
    )(q, x1, mk, mv, w_co)


def _two_group_specs(tm, cols, na_tiles):
    a = pl.BlockSpec((tm, cols), lambda i, *_: (jnp.minimum(i, na_tiles - 1), 0))
    b = pl.BlockSpec((tm, cols), lambda i, *_: (jnp.maximum(i - na_tiles, 0), 0))
    return a, b


def _split2(x):
    hi = x.astype(bf16)
    return hi, (x - hi.astype(f32)).astype(bf16)


def _token_chunk(j, rows, nchunk, first=0):
    return pl.ds(first * nchunk + j, rows, stride=nchunk)


def _to_token_tiles(ref, x, first=0):
    rows, d = x.shape
    nchunk = d // LANES
    for j in range(nchunk):
        ref[_token_chunk(j, rows, nchunk, first), :] = x[:, j * LANES:(j + 1) * LANES]


def _token_rows(t, nchunk):
    return pl.ds(pl.multiple_of(t * nchunk, nchunk), nchunk)


def _router_kernel(xa_ref, xb_ref, g_ref, rw_ref, rb_ref, ht_ref, e_ref, rank_ref, gate_ref, cnt_ref,
                   *, n_exp, na_tiles):
    i = pl.program_id(0)
    tm, d = xa_ref.shape

    @pl.when(i == 0)
    def _():
        cnt_ref[...] = jnp.zeros_like(cnt_ref)

    h = _rms(jnp.where(i < na_tiles, xa_ref[...], xb_ref[...]), g_ref[...])
    _to_token_tiles(ht_ref, h)

    hh, hl = _split2(h)
    wh, wl = _split2(rw_ref[...])
    logits = (jnp.dot(hh, wh, preferred_element_type=f32) + jnp.dot(hh, wl, preferred_element_type=f32)
              + jnp.dot(hl, wh, preferred_element_type=f32) + rb_ref[...])

    lane = lax.broadcasted_iota(jnp.int32, (tm, n_exp), 1)
    work = logits
    sels, tops = [], []
    for _ in range(TOP_K):
        m = jnp.max(work, axis=-1, keepdims=True)
        idx = jnp.min(jnp.where(work == m, lane, n_exp), axis=-1, keepdims=True)
        sel = lane == idx
        sels.append(sel)
        tops.append(m)
        work = jnp.where(sel, -jnp.inf, work)
    chosen = sels[0] | sels[1] | sels[2] | sels[3]

    onehot = jnp.where(chosen, 1.0, 0.0)
    rr = lax.broadcasted_iota(jnp.int32, (tm, tm), 0)
    cc = lax.broadcasted_iota(jnp.int32, (tm, tm), 1)
    before = (cc < rr).astype(bf16)
    rank = jnp.dot(before, onehot.astype(bf16), preferred_element_type=f32) + cnt_ref[...]
    cnt_ref[...] = cnt_ref[...] + jnp.sum(onehot, axis=0, keepdims=True)

    den = sum(jnp.exp(t - tops[0]) for t in tops)
    e_cols, r_cols, g_cols = [], [], []
    for sel, top in zip(sels, tops):
        e_cols.append(jnp.sum(jnp.where(sel, lane, 0), axis=-1, keepdims=True))
        r_cols.append(jnp.sum(jnp.where(sel, rank, 0.0), axis=-1, keepdims=True))
        g_cols.append(jnp.exp(top - tops[0]) / den)
    e_ref[...] = jnp.concatenate(e_cols, axis=-1)
    rank_ref[...] = jnp.concatenate(r_cols, axis=-1).astype(jnp.int32)
    gate_ref[...] = jnp.concatenate(g_cols, axis=-1)


def _router(xa, xb, g, rw, rb):
    d = xa.shape[1]
    n = xa.shape[0] + xb.shape[0]
    n_exp = rw.shape[1]
    tm = ROW_TILE
    assert xa.shape[0] % tm == 0 and xb.shape[0] % tm == 0
    na_tiles = xa.shape[0] // tm
    sa, sb = _two_group_specs(tm, d, na_tiles)
    rt = lambda c: pl.BlockSpec((tm, c), lambda i: (i, 0))
    return pl.pallas_call(
        functools.partial(_router_kernel, n_exp=n_exp, na_tiles=na_tiles),
        grid=(n // tm,),
        in_specs=[sa, sb, _full((1, d)), _full((d, n_exp)), _full((1, n_exp))],
        out_specs=[pl.BlockSpec((tm * (d // LANES), LANES), lambda i: (i, 0)), rt(TOP_K), rt(TOP_K), rt(TOP_K),
                   _full((1, n_exp))],
        out_shape=[jax.ShapeDtypeStruct((n * (d // LANES), LANES), f32), jax.ShapeDtypeStruct((n, TOP_K), jnp.int32),
                   jax.ShapeDtypeStruct((n, TOP_K), jnp.int32), jax.ShapeDtypeStruct((n, TOP_K), f32),
                   jax.ShapeDtypeStruct((1, n_exp), f32)],
        compiler_params=_params("arbitrary"),
        name="moe_router",
    )(xa, xb, g.reshape(1, d), rw.astype(f32), rb.reshape(1, n_exp).astype(f32))


def _invert_kernel(dest_ref, zs_ref, ze_ref, tok_ref, *, n):
    def for_each_unmapped_row(fn):
        def segment(seg, c):
            return lax.fori_loop(zs_ref[seg], ze_ref[seg], fn, c)
        lax.fori_loop(0, zs_ref.shape[0], segment, 0)

    fill = lambda r, c: _set(tok_ref, r, 0, c)
    for_each_unmapped_row(fill)
    for kk in range(TOP_K):
        def place(t, c, kk=kk):
            return _set(tok_ref, dest_ref[kk * n + t], t, c)
        lax.fori_loop(0, n, place, 0, unroll=8)


def _set(ref, idx, val, carry):
    ref[idx] = val
    return carry


def _invert(dest_km, zero_start, zero_end, n, n_rows):
    grid_spec = pltpu.PrefetchScalarGridSpec(
        num_scalar_prefetch=3, grid=(1,), in_specs=[],
        out_specs=pl.BlockSpec(memory_space=pltpu.SMEM))
    return pl.pallas_call(
        functools.partial(_invert_kernel, n=n),
        grid_spec=grid_spec,
        out_shape=jax.ShapeDtypeStruct((n_rows,), jnp.int32),
        compiler_params=_params("arbitrary"),
        name="moe_invert",
    )(dest_km, zero_start, zero_end)


def _expert_kernel(be_ref, nv_ref, nxt_ref, tok_ref, ht_hbm, wgu_hbm, bgu_ref, wd_hbm, bd_ref, y_ref,
                   xbuf, xsem, wgu_f, wd_f, wsem, wgu_b, wd_b, *, layer, blk):
    i = pl.program_id(0)
    slot = i % 2
    nvalid = nv_ref[0]
    dff = wd_b.shape[0]
    nchunk = wgu_b.shape[0] // LANES
    last_block = pl.num_programs(0) - 1

    def gather(block, into):
        for r in range(blk):
            tok = tok_ref[block * blk + r]
            pltpu.make_async_copy(ht_hbm.at[_token_rows(tok, nchunk)], xbuf.at[into, _token_rows(r, nchunk)],
                                  xsem.at[into]).start(priority=0)

    def weight_copies(e):
        return (pltpu.make_async_copy(wgu_hbm.at[layer, e], wgu_f, wsem.at[0]),
                pltpu.make_async_copy(wd_hbm.at[layer, e], wd_f, wsem.at[1]))

    @pl.when(i == 0)
    def _():
        gather(0, 0)
        for cp in weight_copies(be_ref[0]):
            cp.start(priority=1)

    @pl.when((i < nvalid) & ((i == 0) | (be_ref[i] != be_ref[jnp.maximum(i - 1, 0)])))
    def _():
        for cp in weight_copies(be_ref[i]):
            cp.wait()
        wgu_b[...] = wgu_f[...].astype(bf16)
        wd_b[...] = wd_f[...].astype(bf16)

        @pl.when(nxt_ref[i] >= 0)
        def _():
            for cp in weight_copies(nxt_ref[i]):
                cp.start(priority=1)

    @pl.when(i < nvalid)
    def _():
        pltpu.make_async_copy(xbuf.at[slot], xbuf.at[slot], xsem.at[slot]).wait()
        gather(jnp.minimum(i + 1, last_block), 1 - slot)
        x = jnp.concatenate([xbuf[slot, _token_chunk(j, blk, nchunk), :].astype(bf16) for j in range(nchunk)],
                            axis=-1)
        hgu = jnp.dot(x, wgu_b[...], preferred_element_type=f32) + bgu_ref[0, 0]
        hg = jnp.minimum(hgu[:, :dff], SWIGLU_LIMIT)
        hu = jnp.clip(hgu[:, dff:], -SWIGLU_LIMIT, SWIGLU_LIMIT)
        act = (hu + 1.0) * hg * jax.nn.sigmoid(SWIGLU_ALPHA * hg)
        _to_token_tiles(y_ref, _dot(act, wd_b[...]) + bd_ref[0, 0])

    @pl.when(i >= nvalid)
    def _():
        y_ref[...] = jnp.zeros_like(y_ref)

    @pl.when(i == nvalid)
    def _():
        pltpu.make_async_copy(xbuf.at[slot], xbuf.at[slot], xsem.at[slot]).wait()


def _experts(ht, row_tok, block_e, nvalid, next_e, w_gu, b_gu, w_down, b_down, l, blk):
    depth, n_exp, d, dff2 = w_gu.shape
    dff = dff2 // 2
    nchunk = d // LANES
    n_rows = row_tok.shape[0]
    bspec = lambda cols: pl.BlockSpec((1, 1, 1, cols), lambda i, be, *_: (l, be[i], 0, 0))
    grid_spec = pltpu.PrefetchScalarGridSpec(
        num_scalar_prefetch=4,
        grid=(n_rows // blk,),
        in_specs=[pl.BlockSpec(memory_space=pl.ANY), pl.BlockSpec(memory_space=pl.ANY), bspec(dff2),
                  pl.BlockSpec(memory_space=pl.ANY), bspec(d)],
        out_specs=pl.BlockSpec((blk * nchunk, LANES), lambda i, *_: (i, 0)),
        scratch_shapes=[pltpu.VMEM((2, blk * nchunk, LANES), f32), pltpu.SemaphoreType.DMA((2,)),
                        pltpu.VMEM((d, dff2), f32), pltpu.VMEM((dff, d), f32), pltpu.SemaphoreType.DMA((2,)),
                        pltpu.VMEM((d, dff2), bf16), pltpu.VMEM((dff, d), bf16)],
    )
    return pl.pallas_call(
        functools.partial(_expert_kernel, layer=l, blk=blk),
        grid_spec=grid_spec,
        out_shape=jax.ShapeDtypeStruct((n_rows * nchunk, LANES), f32),
        compiler_params=_params("arbitrary"),
        name="moe_experts",
    )(block_e, nvalid, next_e, row_tok, ht, w_gu, b_gu.reshape(depth, n_exp, 1, dff2), w_down,
      b_down.reshape(depth, n_exp, 1, d))


def _combine_kernel(dest_ref, gate_ref, xa_ref, xb_ref, yb_hbm, g_ref, oa_ref, ob_ref, ybuf, acc_s, sem,
                    *, tm, n, na_tiles, final):
    i = pl.program_id(0)
    slot = i % 2
    nchunk = ybuf.shape[2] // tm
    last_tile = pl.num_programs(0) - 1

    def copy_row(tile, into, kk, t, prio):
        row = dest_ref[kk * n + tile * tm + t]
        pltpu.make_async_copy(yb_hbm.at[_token_rows(row, nchunk)], ybuf.at[into, kk, _token_rows(t, nchunk)],
                              sem.at[into]).start(priority=prio)

    @pl.when(i == 0)
    def _():
        for kk in range(TOP_K):
            def issue(pair, c, kk=kk):
                for prio in range(DMA_PRIORITIES):
                    copy_row(0, 0, kk, pair * DMA_PRIORITIES + prio, prio)
                return c
            lax.fori_loop(0, tm // DMA_PRIORITIES, issue, 0, unroll=4)

    pltpu.make_async_copy(ybuf.at[slot], ybuf.at[slot], sem.at[slot]).wait()
    nxt = jnp.minimum(i + 1, last_tile)
    for kk in range(TOP_K):
        for t in range(tm):
            copy_row(nxt, 1 - slot, kk, t, t % DMA_PRIORITIES)
    for t in range(tm):
        first = (i * tm + t) * TOP_K
        acc = gate_ref[first] * ybuf[slot, 0, t * nchunk:(t + 1) * nchunk, :]
        for kk in range(1, TOP_K):
            acc = acc + gate_ref[first + kk] * ybuf[slot, kk, t * nchunk:(t + 1) * nchunk, :]
        acc_s[t * nchunk:(t + 1) * nchunk, :] = acc
    moe = jnp.concatenate([acc_s[_token_chunk(j, tm, nchunk), :] for j in range(nchunk)], axis=-1)
    x3 = jnp.where(i < na_tiles, xa_ref[...], xb_ref[...]) + moe
    out = _rms(x3, g_ref[...]) if final else x3

    @pl.when(i < na_tiles)
    def _():
        oa_ref[...] = out

    @pl.when(i >= na_tiles)
    def _():
        ob_ref[...] = out

    @pl.when(i == last_tile)
    def _():
        pltpu.make_async_copy(ybuf.at[1 - slot], ybuf.at[1 - slot], sem.at[1 - slot]).wait()


def _combine(dest_km, xa, xb, gate, yb, g_final, final):
    d = xa.shape[1]
    n = xa.shape[0] + xb.shape[0]
    tm = COMBINE_TILE
    assert xa.shape[0] % tm == 0 and xb.shape[0] % tm == 0
    na_tiles = xa.shape[0] // tm
    sa, sb = _two_group_specs(tm, d, na_tiles)
    nchunk = d // LANES
    grid_spec = pltpu.PrefetchScalarGridSpec(
        num_scalar_prefetch=2,
        grid=(n // tm,),
        in_specs=[sa, sb, pl.BlockSpec(memory_space=pl.ANY), pl.BlockSpec((1, d), lambda i, *_: (0, 0))],
        out_specs=[sa, sb],
        scratch_shapes=[pltpu.VMEM((2, TOP_K, tm * nchunk, LANES), f32), pltpu.VMEM((tm * nchunk, LANES), f32),
                        pltpu.SemaphoreType.DMA((2,))],
    )
    return pl.pallas_call(
        functools.partial(_combine_kernel, tm=tm, n=n, na_tiles=na_tiles, final=final),
        grid_spec=grid_spec,
        out_shape=[jax.ShapeDtypeStruct(xa.shape, f32), jax.ShapeDtypeStruct(xb.shape, f32)],
        compiler_params=_params("arbitrary"),
        name="moe_combine",
    )(dest_km, gate.reshape(-1), xa, xb, yb, g_final.reshape(1, d))


def _moe(xa, xb, g_ffn, router_w, router_b, w_gu, b_gu, w_down, b_down, l, g_out, final):
    n = xa.shape[0] + xb.shape[0]
    n_exp = router_w.shape[1]
    blk = MOE_ROWS
    ht, top_e, top_rank, gate, counts = _router(xa, xb, g_ffn, router_w, router_b)

    counts = counts.reshape(n_exp).astype(jnp.int32)
    padded = (counts + blk - 1) // blk * blk
    pad_end = jnp.cumsum(padded)
    pad_start = pad_end - padded
    lookup = lambda table, idx: jnp.sum(jnp.where(idx[..., None] == jnp.arange(n_exp), table, 0), axis=-1)
    dest_km = (lookup(pad_start, top_e.T) + top_rank.T).reshape(-1)
    n_rows = -(-(n * TOP_K) // blk) * blk + n_exp * blk
    block_start = jnp.arange(n_rows // blk, dtype=jnp.int32) * blk
    block_e = jnp.minimum(jnp.sum(pad_end[None, :] <= block_start[:, None], axis=1), n_exp - 1).astype(jnp.int32)
    nvalid = (pad_end[-1:] // blk).astype(jnp.int32)
    zero_start = jnp.concatenate([pad_start + counts, pad_end[-1:]]).astype(jnp.int32)
    zero_end = jnp.concatenate([pad_end, jnp.full((1,), n_rows, jnp.int32)]).astype(jnp.int32)

    has_rows = padded > 0
    ids = jnp.arange(n_exp, dtype=jnp.int32)
    after = jnp.where(has_rows[None, :] & (ids[None, :] > ids[:, None]), ids[None, :], n_exp)
    next_of = jnp.min(after, axis=1)
    next_e = lookup(jnp.where(next_of < n_exp, next_of, -1), block_e).astype(jnp.int32)

    row_tok = _invert(dest_km, zero_start, zero_end, n, n_rows)
    yb = _experts(ht, row_tok, block_e, nvalid, next_e, w_gu, b_gu, w_down, b_down, l, blk)
    return _combine(dest_km, xa, xb, gate, yb, g_out, final)


def _mixers(x, t, b, h0r, h0i, prev, s0, P, l):
    s5w = P['s5_d'].shape[-1]
    u, prk = _inproj(x, P['norm_mix_g'][l], P['w_in'][l].astype(bf16), s5w)
    ar, ai, bre, bim, cre_m, cim_m = _s5_discretize(P['s5_lambda_re'][l], P['s5_lambda_im'][l], P['s5_log_dt'][l],
                                                    P['s5_b_re'][l], P['s5_b_im'][l], P['s5_c_re'][l], P['s5_c_im'][l])
    s5y, hr, hi = _s5(u.reshape(b, t, s5w), h0r, h0i, ar, ai, bre, bim, cre_m, cim_m,
                      P['s5_d'][l].reshape(1, s5w).astype(f32), P['s5_glu_w'][l].astype(bf16),
                      P['s5_glu_b'][l].reshape(1, s5w).astype(f32))
    s5y = s5y.reshape(b * t, s5w)
    rwy, s_fin = _rwkv(prk, prev, s0, t, b, P['rwkv_mu'][l], P['rwkv_w0'][l], P['rwkv_w1'][l], P['rwkv_w2'][l],
                       P['rwkv_a0'][l], P['rwkv_a1'][l], P['rwkv_a2'][l], P['rwkv_g1'][l], P['rwkv_g2'][l],
                       P['rwkv_k_k'][l], P['rwkv_k_a'][l], P['rwkv_r_k'][l], P['rwkv_ln_w'][l], P['rwkv_ln_b'][l])
    x1, q = _outproj(x, s5y, rwy, P['w_out'][l].astype(bf16), P['norm_ca_g'][l], P['w_cq'][l].astype(bf16))
    shift_last = prk.reshape(b, t, -1)[:, t - 1]
    return x1, q, hr, hi, shift_last, s_fin


def kernel(x_prompt, x_sample, mem_prompt, state_s5_re, state_s5_im, state_rwkv_shift, state_rwkv_wkv, cache_mem_k, cache_mem_v, norm_mix_g, w_in, s5_lambda_re, s5_lambda_im, s5_log_dt, s5_b_re, s5_b_im, s5_c_re, s5_c_im, s5_d, s5_glu_w, s5_glu_b, rwkv_mu, rwkv_w0, rwkv_w1, rwkv_w2, rwkv_a0, rwkv_a1, rwkv_a2, rwkv_g1, rwkv_g2, rwkv_k_k, rwkv_k_a, rwkv_r_k, rwkv_ln_w, rwkv_ln_b, w_out, norm_ca_g, norm_mem_g, w_cq, w_mem_kv, w_co, norm_ffn_g, router_w, router_b, ex_w_gu, ex_b_gu, ex_w_down, ex_b_down, norm_final_g):
    P = dict(norm_mix_g=norm_mix_g, w_in=w_in, s5_lambda_re=s5_lambda_re, s5_lambda_im=s5_lambda_im,
             s5_log_dt=s5_log_dt, s5_b_re=s5_b_re, s5_b_im=s5_b_im, s5_c_re=s5_c_re, s5_c_im=s5_c_im,
             s5_d=s5_d, s5_glu_w=s5_glu_w, s5_glu_b=s5_glu_b, rwkv_mu=rwkv_mu, rwkv_w0=rwkv_w0,
             rwkv_w1=rwkv_w1, rwkv_w2=rwkv_w2, rwkv_a0=rwkv_a0, rwkv_a1=rwkv_a1, rwkv_a2=rwkv_a2,
             rwkv_g1=rwkv_g1, rwkv_g2=rwkv_g2, rwkv_k_k=rwkv_k_k, rwkv_k_a=rwkv_k_a, rwkv_r_k=rwkv_r_k,
             rwkv_ln_w=rwkv_ln_w, rwkv_ln_b=rwkv_ln_b, w_out=w_out, norm_ca_g=norm_ca_g, w_cq=w_cq)
    depth = norm_mix_g.shape[0]
    bp, tp, d = x_prompt.shape
    bs, ts, _ = x_sample.shape
    n_mem = mem_prompt.shape[1]
    s5_groups, s5_state = s5_lambda_re.shape[1:]
    nstate = s5_groups * s5_state
    rw_in = state_rwkv_shift.shape[-1]
    heads = state_rwkv_wkv.shape[2]
    mh = cache_mem_k.shape[3]
    assert mh == MEM_HEADS

    xp = x_prompt.reshape(bp * tp, d)
    xs = x_sample.reshape(bs * ts, d)
    outs = {k: [] for k in ('p_re', 'p_im', 'p_sh', 'p_wkv', 'p_mk', 'p_mv', 's_re', 's_im', 's_sh', 's_wkv')}
    for l in range(depth):
        mk, mv = _memkv(mem_prompt.reshape(bp * n_mem, d), norm_mem_g[l], w_mem_kv[l].astype(bf16))
        zeros_state = jnp.zeros((bp, nstate), f32)
        x1p, qp, re_p, im_p, sh_p, wkv_p = _mixers(
            xp, tp, bp, zeros_state, zeros_state, jnp.zeros((bp, rw_in), f32),
            jnp.zeros((bp, heads, RWKV_HEAD, RWKV_HEAD), f32), P, l)
        x1s, qs, re_s, im_s, sh_s, wkv_s = _mixers(
            xs, ts, bs, state_s5_re[l].reshape(bs, nstate), state_s5_im[l].reshape(bs, nstate),
            state_rwkv_shift[l], state_rwkv_wkv[l], P, l)
        wco = w_co[l].astype(bf16)
        x2p = _attn(qp, x1p, mk, mv, wco, tp, bp)
        x2s = _attn(qs, x1s, _kv_to_head_tiles(cache_mem_k[l]), _kv_to_head_tiles(cache_mem_v[l]), wco, ts, bs)
        last = l == depth - 1
        xp, xs = _moe(x2p, x2s, norm_ffn_g[l], router_w[l], router_b[l], ex_w_gu, ex_b_gu, ex_w_down, ex_b_down,
                      l, norm_final_g, last)
        outs['p_re'].append(re_p.reshape(bp, s5_groups, s5_state))
        outs['p_im'].append(im_p.reshape(bp, s5_groups, s5_state))
        outs['p_sh'].append(sh_p)
        outs['p_wkv'].append(wkv_p)
        outs['p_mk'].append(_kv_from_head_tiles(mk, bp, n_mem, mh, d // mh))
        outs['p_mv'].append(_kv_from_head_tiles(mv, bp, n_mem, mh, d // mh))
        outs['s_re'].append(re_s.reshape(bs, s5_groups, s5_state))
        outs['s_im'].append(im_s.reshape(bs, s5_groups, s5_state))
        outs['s_sh'].append(sh_s)
        outs['s_wkv'].append(wkv_s)
    st = lambda k: jnp.stack(outs[k])
    return (xp.reshape(bp, tp, d), xs.reshape(bs, ts, d), st('p_re'), st('p_im'), st('p_sh'), st('p_wkv'),
            st('p_mk'), st('p_mv'), st('s_re'), st('s_im'), st('s_sh'), st('s_wkv'))
```

```python
import functools

import jax
import jax.numpy as jnp
from jax import lax
from jax.experimental import pallas as pl
from jax.experimental.pallas import tpu as pltpu

f32 = jnp.float32
bf16 = jnp.bfloat16

RWKV_HEAD = 64
HEAD_PAIR = 2 * RWKV_HEAD
DECAY_SCALE = 0.6065306597
MEM_HEADS = 4
TOP_K = 4
SWIGLU_LIMIT = 7.0
SWIGLU_ALPHA = 1.702
RMS_EPS = 1e-5
GN_EPS = 64e-5

V7X_VMEM_LIMIT = 56 * 1024 * 1024
ROW_TILE = 512
MOE_ROWS = 256
COMBINE_TILE = 256
WKV_CHUNK = 64
WKV_BATCH = 4
S5_CHUNK = 64
S5_BLOCK_CHANNELS = 128
LANES = 128
DMA_PRIORITIES = 2
ATTN_SHORT_BATCH = 4

_NT = (((1,), (1,)), ((), ()))
_TN = (((0,), (0,)), ((), ()))


def _params(*sem):
    return pltpu.CompilerParams(dimension_semantics=sem, vmem_limit_bytes=V7X_VMEM_LIMIT)


def _dot(a, b):
    return jnp.dot(a.astype(bf16), b.astype(bf16), preferred_element_type=f32)


def _dg(a, b, dims):
    return lax.dot_general(a.astype(bf16), b.astype(bf16), dims, preferred_element_type=f32)


def _rms(x, g):
    return x * lax.rsqrt(jnp.mean(x * x, axis=-1, keepdims=True) + RMS_EPS) * g


def _full(shape):
    n = len(shape)
    return pl.BlockSpec(shape, lambda *_: (0,) * n)


def _tile(rows, tile=ROW_TILE):
    t = min(tile, rows)
    assert rows % t == 0
    return t


def _inproj_kernel(x_ref, g_ref, w_ref, u_ref, p_ref, *, s5w):
    nb = _rms(x_ref[...], g_ref[...]).astype(bf16)
    u_ref[...] = jnp.dot(nb, w_ref[:, :s5w], preferred_element_type=f32)
    p_ref[...] = jnp.dot(nb, w_ref[:, s5w:], preferred_element_type=f32)


def _inproj(x, g, w, s5w):
    rows, d = x.shape
    inw = w.shape[1]
    tm = _tile(rows)
    return pl.pallas_call(
        functools.partial(_inproj_kernel, s5w=s5w),
        grid=(rows // tm,),
        in_specs=[pl.BlockSpec((tm, d), lambda i: (i, 0)), _full((1, d)), _full((d, inw))],
        out_specs=[pl.BlockSpec((tm, s5w), lambda i: (i, 0)), pl.BlockSpec((tm, inw - s5w), lambda i: (i, 0))],
        out_shape=[jax.ShapeDtypeStruct((rows, s5w), f32), jax.ShapeDtypeStruct((rows, inw - s5w), f32)],
        compiler_params=_params("parallel"),
        name="inproj",
    )(x, g.reshape(1, d), w)


def _s5_kernel(u_hbm, h0r_ref, h0i_ref, ar_ref, ai_ref, bre_ref, bim_ref, cre_ref, cim_ref, d_ref, gw_ref, gb_ref,
               y_hbm, hr_ref, hi_ref, xr_s, xi_s, ubuf, usem, ybuf, ysem, *, steps, bt, lane_chunk):
    batch_tile = pl.program_id(0)
    i = pl.program_id(1)
    nchunks = pl.num_programs(1)
    step = batch_tile * nchunks + i
    last_step = pl.num_programs(0) * nchunks - 1
    slot = step % 2
    width = ubuf.shape[-1]
    nstate = ar_ref.shape[-1]

    def chunk_copies(tile, chunk, s, load):
        t0 = pl.multiple_of(chunk * steps, steps)
        for b in range(bt):
            hbm = (u_hbm if load else y_hbm).at[tile * bt + b, pl.ds(t0, steps), :]
            if load:
                yield pltpu.make_async_copy(hbm, ubuf.at[s, :, b, :], usem.at[s])
            else:
                yield pltpu.make_async_copy(ybuf.at[s, :, b, :], hbm, ysem.at[s])

    @pl.when(i == 0)
    def _():
        hr_ref[...] = h0r_ref[...]
        hi_ref[...] = h0i_ref[...]

    @pl.when(step == 0)
    def _():
        for cp in chunk_copies(0, 0, 0, True):
            cp.start()

    @pl.when(step < last_step)
    def _():
        wraps = i + 1 == nchunks
        nxt_tile = jnp.where(wraps, batch_tile + 1, batch_tile)
        nxt_chunk = jnp.where(wraps, 0, i + 1)
        for cp in chunk_copies(nxt_tile, nxt_chunk, 1 - slot, True):
            cp.start()

    @pl.when(step >= 2)
    def _():
        for cp in chunk_copies(batch_tile, i, slot, False):
            cp.wait()

    for cp in chunk_copies(batch_tile, i, slot, True):
        cp.wait()
    u = ubuf[slot].reshape(steps * bt, width)
    ub = u.astype(bf16)
    nblk, cb, sb = bre_ref.shape
    for j in range(nblk):
        uj = ub[:, j * cb:(j + 1) * cb]
        xr_s[:, j * sb:(j + 1) * sb] = jnp.dot(uj, bre_ref[j], preferred_element_type=f32)
        xi_s[:, j * sb:(j + 1) * sb] = jnp.dot(uj, bim_ref[j], preferred_element_type=f32)

    for c in range(nstate // lane_chunk):
        sl = slice(c * lane_chunk, (c + 1) * lane_chunk)
        arb = jnp.broadcast_to(ar_ref[:, sl], (bt, lane_chunk))
        aib = jnp.broadcast_to(ai_ref[:, sl], (bt, lane_chunk))

        hr, hi = hr_ref[:, sl], hi_ref[:, sl]
        for t in range(steps):
            rows = slice(t * bt, (t + 1) * bt)
            hr, hi = (arb * hr - aib * hi + xr_s[rows, sl], arb * hi + aib * hr + xi_s[rows, sl])
            xr_s[rows, sl] = hr
            xi_s[rows, sl] = hi
        hr_ref[:, sl] = hr
        hi_ref[:, sl] = hi

    y = jnp.concatenate(
        [jnp.dot(xr_s[:, j * sb:(j + 1) * sb].astype(bf16), cre_ref[j], preferred_element_type=f32)
         + jnp.dot(xi_s[:, j * sb:(j + 1) * sb].astype(bf16), cim_ref[j], preferred_element_type=f32)
         for j in range(nblk)], axis=-1) + d_ref[...] * u
    z = jax.nn.gelu(y)
    out = z * jax.nn.sigmoid(_dot(z, gw_ref[...]) + gb_ref[...])
    ybuf[slot] = out.reshape(steps, bt, width)
    for cp in chunk_copies(batch_tile, i, slot, False):
        cp.start()

    @pl.when(step == last_step)
    def _():
        for cp in chunk_copies(batch_tile, i, slot, False):
            cp.wait()

        @pl.when(step >= 1)
        def _():
            for cp in chunk_copies(batch_tile, i, 1 - slot, False):
                cp.wait()


def _s5(u3, h0r, h0i, ar, ai, bre, bim, cre_m, cim_m, d, gw, gb):
    b, t, width = u3.shape
    nstate = ar.shape[-1]
    bt = 8
    steps = min(S5_CHUNK, t)
    assert b % bt == 0 and t % steps == 0
    kern = functools.partial(_s5_kernel, steps=steps, bt=bt, lane_chunk=512)
    st_spec = pl.BlockSpec((bt, nstate), lambda j, i: (j, 0))
    y, hr, hi = pl.pallas_call(
        kern,
        grid=(b // bt, t // steps),
        in_specs=[pl.BlockSpec(memory_space=pl.ANY), st_spec, st_spec,
                  _full((1, nstate)), _full((1, nstate)), _full(bre.shape), _full(bim.shape),
                  _full(cre_m.shape), _full(cim_m.shape), _full((1, width)), _full((width, width)),
                  _full((1, width))],
        out_specs=[pl.BlockSpec(memory_space=pl.ANY), st_spec, st_spec],
        out_shape=[jax.ShapeDtypeStruct((b, t, width), f32), jax.ShapeDtypeStruct((b, nstate), f32),
                   jax.ShapeDtypeStruct((b, nstate), f32)],
        scratch_shapes=[pltpu.VMEM((steps * bt, nstate), f32), pltpu.VMEM((steps * bt, nstate), f32),
                        pltpu.VMEM((2, steps, bt, width), f32), pltpu.SemaphoreType.DMA((2,)),
                        pltpu.VMEM((2, steps, bt, width), f32), pltpu.SemaphoreType.DMA((2,))],
        compiler_params=_params("arbitrary", "arbitrary"),
        name="s5_mixer",
    )(u3, h0r, h0i, ar, ai, bre, bim, cre_m, cim_m, d, gw, gb)
    return y, hr, hi


def _s5_discretize(lam_re, lam_im, log_dt, b_re, b_im, c_re, c_im):
    g, p = lam_re.shape
    dt = jnp.exp(log_dt.astype(f32))[:, None]
    lr = lam_re.astype(f32)
    li = lam_im.astype(f32)
    mag = jnp.exp(lr * dt)
    ar = mag * jnp.cos(li * dt)
    ai = mag * jnp.sin(li * dt)
    den = lr * lr + li * li
    cr = ((ar - 1.0) * lr + ai * li) / den
    ci = (ai * lr - (ar - 1.0) * li) / den
    br = b_re.astype(f32)
    bi = b_im.astype(f32)
    bbar_r = cr[..., None] * br - ci[..., None] * bi
    bbar_i = cr[..., None] * bi + ci[..., None] * br
    c = br.shape[-1]
    gb = S5_BLOCK_CHANNELS // c
    assert g % gb == 0
    eye = jnp.eye(gb, dtype=f32)
    nblk = g // gb
    bd_in = lambda m: jnp.einsum('ngpc,gh->ngchp', m.reshape(nblk, gb, p, c), eye).reshape(
        nblk, gb * c, gb * p).astype(bf16)
    bd_out = lambda m: jnp.einsum('ngcp,gh->ngphc', m.reshape(nblk, gb, c, p), eye).reshape(
        nblk, gb * p, gb * c).astype(bf16)
    return (ar.reshape(1, g * p), ai.reshape(1, g * p), bd_in(bbar_r), bd_in(bbar_i),
            bd_out(c_re.astype(f32)), bd_out(-c_im.astype(f32)))


def _rwkv_kernel(p_ref, prev_ref, s0_ref, mu_ref, w0_ref, w1_ref, w2_ref, a0_ref, a1_ref, a2_ref, g1_ref, g2_ref,
                 kk_ref, ka_ref, rk_ref, lnw_ref, lnb_ref, y_ref, sout_ref, prev_s, st_s, *, steps, nsq):
    i = pl.program_id(1)
    L = steps
    L2 = 2 * L
    nb = p_ref.shape[0]
    w = p_ref.shape[-1] // 4
    npair = w // HEAD_PAIR

    row2 = lax.broadcasted_iota(jnp.int32, (L2, HEAD_PAIR), 0)
    lane2 = lax.broadcasted_iota(jnp.int32, (L2, HEAD_PAIR), 1)
    own = (row2 < L) == (lane2 < RWKV_HEAD)
    ri = lax.broadcasted_iota(jnp.int32, (L2, L2), 0)
    ci = lax.broadcasted_iota(jnp.int32, (L2, L2), 1)
    same = (ri < L) == (ci < L)
    strict = same & (ci < ri)
    incl = same & (ci <= ri)
    eye = (ri == ci).astype(f32)
    sr = lax.broadcasted_iota(jnp.int32, (HEAD_PAIR, HEAD_PAIR), 0)
    sc = lax.broadcasted_iota(jnp.int32, (HEAD_PAIR, HEAD_PAIR), 1)
    sdiag = (sr < RWKV_HEAD) == (sc < RWKV_HEAD)

    @pl.when(i == 0)
    def _():
        prev_s[...] = prev_ref[...]
        for bb in range(nb):
            for pr in range(npair):
                sp = s0_ref[bb, pr * HEAD_PAIR:(pr + 1) * HEAD_PAIR, :]
                st_s[bb * npair + pr] = jnp.where(sdiag, jnp.concatenate([sp, sp], axis=1), 0.0)

    trow = lax.broadcasted_iota(jnp.int32, (L, 1), 0)
    ps, deltas = [], []
    for bb in range(nb):
        pb = p_ref[bb]
        shifted = jnp.where(trow == 0, prev_s[bb], pltpu.roll(pb, 1, axis=0))
        prev_s[bb] = pb[L - 1:L, :]
        ps.append(pb)
        deltas.append(shifted - pb)
    p = jnp.concatenate(ps, axis=0) if nb > 1 else ps[0]
    delta = jnp.concatenate(deltas, axis=0) if nb > 1 else deltas[0]
    mu = mu_ref[...]
    pz = p[:, 3 * w:]
    dz = delta[:, 3 * w:]
    r = p[:, :w] + delta[:, :w] * mu[0:1]
    k = p[:, w:2 * w] + delta[:, w:2 * w] * mu[1:2]
    v = p[:, 2 * w:3 * w] + delta[:, 2 * w:3 * w] * mu[2:3]
    zw = pz + dz * mu[3:4]
    za = pz + dz * mu[4:5]
    zg = pz + dz * mu[5:6]
    logw = -DECAY_SCALE * jax.nn.sigmoid(w0_ref[...] + _dot(jnp.tanh(_dot(zw, w1_ref[...])), w2_ref[...]))
    a = jax.nn.sigmoid(a0_ref[...] + _dot(_dot(za, a1_ref[...]), a2_ref[...]))
    g = _dot(jax.nn.sigmoid(_dot(zg, g1_ref[...])), g2_ref[...])
    kkraw = k * kk_ref[...]
    kmod = k * (1.0 + (a - 1.0) * ka_ref[...])
    rk = r * kmod * rk_ref[...]

    tr = lax.broadcasted_iota(jnp.int32, (nb * L, nb * L), 0)
    tc = lax.broadcasted_iota(jnp.int32, (nb * L, nb * L), 1)
    tri = ((tc <= tr) & (tc >= (tr // L) * L)).astype(bf16)
    hi = logw.astype(bf16)
    rem = logw - hi.astype(f32)
    mid = rem.astype(bf16)
    lo = (rem - mid.astype(f32)).astype(bf16)
    cum = (jnp.dot(tri, hi, preferred_element_type=f32) + jnp.dot(tri, mid, preferred_element_type=f32)
           + jnp.dot(tri, lo, preferred_element_type=f32))

    stack = lambda x: jnp.concatenate([x, x], axis=0)
    chains = [(bb, pr) for bb in range(nb) for pr in range(npair)]
    nch = range(len(chains))
    rs = [slice(bb * L, (bb + 1) * L) for bb, _ in chains]
    sls = [slice(pr * HEAD_PAIR, (pr + 1) * HEAD_PAIR) for _, pr in chains]
    cut = lambda x, c: stack(x[rs[c], sls[c]])
    own_of = lambda x, c: jnp.where(own, cut(x, c), 0.0)

    wt, rm, km, bm, kap, vmf, vm, gram = [], [], [], [], [], [], [], []
    for c in nch:
        cum_c = cut(cum, c)
        wt_c = jnp.exp(cum_c)
        winv = jnp.exp(-cum_c)
        wprev = jnp.exp(cum_c - cut(logw, c))
        kkr = own_of(kkraw, c)
        kkn = kkr * lax.rsqrt(jnp.maximum(jnp.sum(kkr * kkr, axis=-1, keepdims=True), 1e-24))
        wt.append(wt_c)
        rm.append((own_of(r, c) * wt_c).astype(bf16))
        km.append((own_of(kmod, c) * winv).astype(bf16))
        bm.append((kkn * cut(a, c) * winv).astype(bf16))
        kap.append((kkn * wprev).astype(bf16))
        vmf.append(own_of(v, c))
        vm.append(vmf[c].astype(bf16))
        gram.append(_dg(jnp.concatenate([kap[c], rm[c]], axis=0), jnp.concatenate([bm[c], km[c]], axis=0), _NT))
    n_mat = [jnp.where(strict, x[:L2, :L2], 0.0) for x in gram]
    p_mat = [jnp.where(strict, x[:L2, L2:], 0.0) for x in gram]
    g_mat = [jnp.where(incl, x[L2:, :L2], 0.0) for x in gram]
    q_mat = [jnp.where(incl, x[L2:, L2:], 0.0) for x in gram]

    neg = [-x for x in n_mat]
    inv = [eye + x for x in neg]
    if nsq > 0:
        pw = [_dot(x, x) for x in neg]
        for _ in range(nsq - 1):
            both = [_dot(pw[c], jnp.concatenate([inv[c], pw[c]], axis=1)) for c in nch]
            inv = [inv[c] + both[c][:, :L2] for c in nch]
            pw = [x[:, L2:] for x in both]
        inv = [inv[c] + _dot(pw[c], inv[c]) for c in nch]

    s_old = [st_s[c] for c in nch]
    rhs = [_dg(kap[c], s_old[c], _NT) + _dot(p_mat[c], vm[c]) for c in nch]
    ym0 = [_dg(rm[c], s_old[c], _NT) for c in nch]
    um = [(-_dot(inv[c], rhs[c])).astype(bf16) for c in nch]
    vu = [jnp.concatenate([vm[c], um[c]], axis=0) for c in nch]
    ym = [ym0[c] + _dot(jnp.concatenate([q_mat[c], g_mat[c]], axis=1), vu[c]) for c in nch]
    for c in nch:
        st_s[c] = (s_old[c] + _dg(vu[c], jnp.concatenate([km[c], bm[c]], axis=0), _TN)) * wt[c][L - 1:L, :]

    for c, (bb, pr) in enumerate(chains):
        mean = jnp.sum(ym[c], axis=-1, keepdims=True) * (1.0 / RWKV_HEAD)
        yc = jnp.where(own, ym[c] - mean, 0.0)
        var = jnp.sum(yc * yc, axis=-1, keepdims=True) * (1.0 / RWKV_HEAD)
        yn = yc * lax.rsqrt(var + GN_EPS)
        bonus = jnp.sum(own_of(rk, c), axis=-1, keepdims=True) * vmf[c]
        mix = (yn[:L] + yn[L:]) * lnw_ref[:, sls[c]] + lnb_ref[:, sls[c]] + (bonus[:L] + bonus[L:])
        y_ref[bb, :, sls[c]] = mix * g[rs[c], sls[c]]

    @pl.when(i == pl.num_programs(1) - 1)
    def _():
        for c, (bb, pr) in enumerate(chains):
            s_new = st_s[c]
            sout_ref[bb, pr * HEAD_PAIR:(pr + 1) * HEAD_PAIR, :] = s_new[:, :RWKV_HEAD] + s_new[:, RWKV_HEAD:]


def _rwkv(prk, prev, s0, t, b, mu, w0, w1, w2, a0, a1, a2, g1, g2, k_k, k_a, r_k, ln_w, ln_b):
    w = prk.shape[-1] // 4
    steps = min(WKV_CHUNK, t)
    nb = min(WKV_BATCH * max(1, WKV_CHUNK // (4 * steps)), b)
    assert t % steps == 0 and w % HEAD_PAIR == 0 and b % nb == 0
    nsq = max((steps - 1).bit_length() - 1, 0)
    row = lambda x: x.reshape(1, -1).astype(f32)
    wb = lambda x: x.astype(bf16)
    weights = [mu.astype(f32), row(w0), wb(w1), wb(w2), row(a0), wb(a1), wb(a2), wb(g1), wb(g2),
               row(k_k), row(k_a), row(r_k), row(ln_w), row(ln_b)]
    state_spec = pl.BlockSpec((nb, w, RWKV_HEAD), lambda j, i: (j, 0, 0))
    y, s_fin = pl.pallas_call(
        functools.partial(_rwkv_kernel, steps=steps, nsq=nsq),
        grid=(b // nb, t // steps),
        in_specs=[pl.BlockSpec((nb, steps, 4 * w), lambda j, i: (j, i, 0)),
                  pl.BlockSpec((nb, 1, 4 * w), lambda j, i: (j, 0, 0)), state_spec]
                 + [_full(x.shape) for x in weights],
        out_specs=[pl.BlockSpec((nb, steps, w), lambda j, i: (j, i, 0)), state_spec],
        out_shape=[jax.ShapeDtypeStruct((b, t, w), f32), jax.ShapeDtypeStruct((b, w, RWKV_HEAD), f32)],
        scratch_shapes=[pltpu.VMEM((nb, 1, 4 * w), f32),
                        pltpu.VMEM((nb * (w // HEAD_PAIR), HEAD_PAIR, HEAD_PAIR), f32)],
        compiler_params=_params("arbitrary", "arbitrary"),
        name="rwkv7_mixer",
    )(prk.reshape(b, t, 4 * w), prev.reshape(b, 1, 4 * w).astype(f32), s0.reshape(b, w, RWKV_HEAD).astype(f32),
      *weights)
    return y.reshape(b * t, w), s_fin.reshape(b, w // RWKV_HEAD, RWKV_HEAD, RWKV_HEAD)


def _outproj_kernel(x_ref, s5_ref, rw_ref, wo_ref, g_ref, wq_ref, x1_ref, q_ref, *, s5w):
    x1 = x_ref[...] + _dot(s5_ref[...], wo_ref[:s5w, :]) + _dot(rw_ref[...], wo_ref[s5w:, :])
    x1_ref[...] = x1
    q_ref[...] = _dot(_rms(x1, g_ref[...]), wq_ref[...]).astype(bf16)


def _outproj(x, s5y, rwy, w_out, g_ca, w_cq):
    rows, d = x.shape
    s5w = s5y.shape[1]
    rww = rwy.shape[1]
    tm = _tile(rows)
    rt = lambda c: pl.BlockSpec((tm, c), lambda i: (i, 0))
    return pl.pallas_call(
        functools.partial(_outproj_kernel, s5w=s5w),
        grid=(rows // tm,),
        in_specs=[rt(d), rt(s5w), rt(rww), _full(w_out.shape), _full((1, d)), _full(w_cq.shape)],
        out_specs=[rt(d), rt(d)],
        out_shape=[jax.ShapeDtypeStruct((rows, d), f32), jax.ShapeDtypeStruct((rows, d), bf16)],
        compiler_params=_params("parallel"),
        name="outproj_q",
    )(x, s5y, rwy, w_out, g_ca.reshape(1, d), w_cq)


def _kv_to_head_tiles(x):
    b, m, h, hd = x.shape
    return x.reshape(b, m, h, hd // LANES, LANES).transpose(0, 1, 3, 2, 4).reshape(b * m * (hd // LANES) * h, LANES)


def _kv_from_head_tiles(x, b, m, h, hd):
    return x.reshape(b, m, hd // LANES, h, LANES).transpose(0, 1, 3, 2, 4).reshape(b, m, h, hd)


def _memkv_kernel(m_ref, g_ref, w_ref, k_ref, v_ref):
    tm, d = m_ref.shape
    hd = d // MEM_HEADS
    parts = hd // LANES
    per_token = parts * MEM_HEADS
    nb = _rms(m_ref[...], g_ref[...]).astype(bf16)
    for ref, cols in ((k_ref, slice(0, d)), (v_ref, slice(d, 2 * d))):
        kv = jnp.dot(nb, w_ref[:, cols], preferred_element_type=f32)
        for h in range(MEM_HEADS):
            for c in range(parts):
                lo = h * hd + c * LANES
                ref[pl.ds(c * MEM_HEADS + h, tm, stride=per_token), :] = kv[:, lo:lo + LANES]


def _memkv(mem, g, w):
    rows, d = mem.shape
    tm = _tile(rows)
    per_token = d // LANES
    out = pl.BlockSpec((tm * per_token, LANES), lambda i: (i, 0))
    return pl.pallas_call(
        _memkv_kernel,
        grid=(rows // tm,),
        in_specs=[pl.BlockSpec((tm, d), lambda i: (i, 0)), _full((1, d)), _full(w.shape)],
        out_specs=[out, out],
        out_shape=[jax.ShapeDtypeStruct((rows * per_token, LANES), f32)] * 2,
        compiler_params=_params("parallel"),
        name="memory_kv",
    )(mem, g.reshape(1, d), w)


def _attn_kernel(q_ref, x1_ref, k_ref, v_ref, wo_ref, x2_ref, *, nbb):
    d = q_ref.shape[-1]
    hd = d // MEM_HEADS
    parts = hd // LANES
    per_token = parts * MEM_HEADS
    n_mem = k_ref.shape[0] // (per_token * nbb)
    rows = q_ref.shape[0] // nbb
    scale = hd ** -0.5
    q = q_ref[...]

    def head(ref, bb, h):
        first = bb * n_mem * per_token
        return jnp.concatenate([ref[pl.ds(first + c * MEM_HEADS + h, n_mem, stride=per_token), :].astype(bf16)
                                for c in range(parts)], axis=-1)

    heads = []
    for h in range(MEM_HEADS):
        s = [_dg(q[bb * rows:(bb + 1) * rows, h * hd:(h + 1) * hd], head(k_ref, bb, h), _NT) * scale
             for bb in range(nbb)]
        e = [jnp.exp(x - jnp.max(x, axis=-1, keepdims=True)) for x in s]
        prob = [x / jnp.sum(x, axis=-1, keepdims=True) for x in e]
        o = [_dot(prob[bb], head(v_ref, bb, h)) for bb in range(nbb)]
        heads.append(jnp.concatenate(o, axis=0) if nbb > 1 else o[0])
    o = jnp.concatenate(heads, axis=-1)
    x2_ref[...] = x1_ref[...] + _dot(o, wo_ref[...])


def _attn(q, x1, mk, mv, w_co, t, b):
    d = q.shape[-1]
    rows_per_b = mk.shape[0] // b
    tt = min(ROW_TILE, t)
    assert t % tt == 0
    nt = t // tt
    nbb = min(ATTN_SHORT_BATCH, b) if (nt == 1 and tt * ATTN_SHORT_BATCH <= ROW_TILE) else 1
    assert b % nbb == 0
    xt = pl.BlockSpec((nbb * tt, d), lambda j, i: (j * nt + i, 0))
    kv = pl.BlockSpec((nbb * rows_per_b, LANES), lambda j, i: (j, 0))
    return pl.pallas_call(
        functools.partial(_attn_kernel, nbb=nbb),
        grid=(b // nbb, nt),
        in_specs=[xt, xt, kv, kv, _full(w_co.shape)],
        out_specs=xt,
        out_shape=jax.ShapeDtypeStruct((b * t, d), f32),
        compiler_params=_params("parallel", "parallel"),
        name="mem_attention",
    )(q, x1, mk, mv, w_co)


def _two_group_specs(tm, cols, na_tiles):
    a = pl.BlockSpec((tm, cols), lambda i, *_: (jnp.minimum(i, na_tiles - 1), 0))
    b = pl.BlockSpec((tm, cols), lambda i, *_: (jnp.maximum(i - na_tiles, 0), 0))
    return a, b


def _split2(x):
    hi = x.astype(bf16)
    return hi, (x - hi.astype(f32)).astype(bf16)


def _token_chunk(j, rows, nchunk, first=0):
    return pl.ds(first * nchunk + j, rows, stride=nchunk)


def _to_token_tiles(ref, x, first=0):
    rows, d = x.shape
    nchunk = d // LANES
    for j in range(nchunk):
        ref[_token_chunk(j, rows, nchunk, first), :] = x[:, j * LANES:(j + 1) * LANES]


def _token_rows(t, nchunk):
    return pl.ds(pl.multiple_of(t * nchunk, nchunk), nchunk)


def _router_kernel(xa_ref, xb_ref, g_ref, rw_ref, rb_ref, ht_ref, e_ref, rank_ref, gate_ref, cnt_ref,
                   *, n_exp, na_tiles):
    i = pl.program_id(0)
    tm, d = xa_ref.shape

    @pl.when(i == 0)
    def _():
        cnt_ref[...] = jnp.zeros_like(cnt_ref)

    h = _rms(jnp.where(i < na_tiles, xa_ref[...], xb_ref[...]), g_ref[...])
    _to_token_tiles(ht_ref, h)

    hh, hl = _split2(h)
    wh, wl = _split2(rw_ref[...])
    logits = (jnp.dot(hh, wh, preferred_element_type=f32) + jnp.dot(hh, wl, preferred_element_type=f32)
              + jnp.dot(hl, wh, preferred_element_type=f32) + rb_ref[...])

    lane = lax.broadcasted_iota(jnp.int32, (tm, n_exp), 1)
    work = logits
    sels, tops = [], []
    for _ in range(TOP_K):
        m = jnp.max(work, axis=-1, keepdims=True)
        idx = jnp.min(jnp.where(work == m, lane, n_exp), axis=-1, keepdims=True)
        sel = lane == idx
        sels.append(sel)
        tops.append(m)
        work = jnp.where(sel, -jnp.inf, work)
    chosen = sels[0] | sels[1] | sels[2] | sels[3]

    onehot = jnp.where(chosen, 1.0, 0.0)
    rr = lax.broadcasted_iota(jnp.int32, (tm, tm), 0)
    cc = lax.broadcasted_iota(jnp.int32, (tm, tm), 1)
    before = (cc < rr).astype(bf16)
    rank = jnp.dot(before, onehot.astype(bf16), preferred_element_type=f32) + cnt_ref[...]
    cnt_ref[...] = cnt_ref[...] + jnp.sum(onehot, axis=0, keepdims=True)

    den = sum(jnp.exp(t - tops[0]) for t in tops)
    e_cols, r_cols, g_cols = [], [], []
    for sel, top in zip(sels, tops):
        e_cols.append(jnp.sum(jnp.where(sel, lane, 0), axis=-1, keepdims=True))
        r_cols.append(jnp.sum(jnp.where(sel, rank, 0.0), axis=-1, keepdims=True))
        g_cols.append(jnp.exp(top - tops[0]) / den)
    e_ref[...] = jnp.concatenate(e_cols, axis=-1)
    rank_ref[...] = jnp.concatenate(r_cols, axis=-1).astype(jnp.int32)
    gate_ref[...] = jnp.concatenate(g_cols, axis=-1)


def _router(xa, xb, g, rw, rb):
    d = xa.shape[1]
    n = xa.shape[0] + xb.shape[0]
    n_exp = rw.shape[1]
    tm = ROW_TILE
    assert xa.shape[0] % tm == 0 and xb.shape[0] % tm == 0
    na_tiles = xa.shape[0] // tm
    sa, sb = _two_group_specs(tm, d, na_tiles)
    rt = lambda c: pl.BlockSpec((tm, c), lambda i: (i, 0))
    return pl.pallas_call(
        functools.partial(_router_kernel, n_exp=n_exp, na_tiles=na_tiles),
        grid=(n // tm,),
        in_specs=[sa, sb, _full((1, d)), _full((d, n_exp)), _full((1, n_exp))],
        out_specs=[pl.BlockSpec((tm * (d // LANES), LANES), lambda i: (i, 0)), rt(TOP_K), rt(TOP_K), rt(TOP_K),
                   _full((1, n_exp))],
        out_shape=[jax.ShapeDtypeStruct((n * (d // LANES), LANES), f32), jax.ShapeDtypeStruct((n, TOP_K), jnp.int32),
                   jax.ShapeDtypeStruct((n, TOP_K), jnp.int32), jax.ShapeDtypeStruct((n, TOP_K), f32),
                   jax.ShapeDtypeStruct((1, n_exp), f32)],
        compiler_params=_params("arbitrary"),
        name="moe_router",
    )(xa, xb, g.reshape(1, d), rw.astype(f32), rb.reshape(1, n_exp).astype(f32))


def _invert_kernel(dest_ref, zs_ref, ze_ref, tok_ref, *, n):
    def for_each_unmapped_row(fn):
        def segment(seg, c):
            return lax.fori_loop(zs_ref[seg], ze_ref[seg], fn, c)
        lax.fori_loop(0, zs_ref.shape[0], segment, 0)

    fill = lambda r, c: _set(tok_ref, r, 0, c)
    for_each_unmapped_row(fill)
    for kk in range(TOP_K):
        def place(t, c, kk=kk):
            return _set(tok_ref, dest_ref[kk * n + t], t, c)
        lax.fori_loop(0, n, place, 0, unroll=8)


def _set(ref, idx, val, carry):
    ref[idx] = val
    return carry


def _invert(dest_km, zero_start, zero_end, n, n_rows):
    grid_spec = pltpu.PrefetchScalarGridSpec(
        num_scalar_prefetch=3, grid=(1,), in_specs=[],
        out_specs=pl.BlockSpec(memory_space=pltpu.SMEM))
    return pl.pallas_call(
        functools.partial(_invert_kernel, n=n),
        grid_spec=grid_spec,
        out_shape=jax.ShapeDtypeStruct((n_rows,), jnp.int32),
        compiler_params=_params("arbitrary"),
        name="moe_invert",
    )(dest_km, zero_start, zero_end)


def _expert_kernel(be_ref, nv_ref, nxt_ref, tok_ref, ht_hbm, wgu_hbm, bgu_ref, wd_hbm, bd_ref, y_ref,
                   xbuf, xsem, wgu_f, wd_f, wsem, wgu_b, wd_b, *, layer, blk):
    i = pl.program_id(0)
    slot = i % 2
    nvalid = nv_ref[0]
    dff = wd_b.shape[0]
    nchunk = wgu_b.shape[0] // LANES
    last_block = pl.num_programs(0) - 1

    def gather(block, into):
        for r in range(blk):
            tok = tok_ref[block * blk + r]
            pltpu.make_async_copy(ht_hbm.at[_token_rows(tok, nchunk)], xbuf.at[into, _token_rows(r, nchunk)],
                                  xsem.at[into]).start(priority=0)

    def weight_copies(e):
        return (pltpu.make_async_copy(wgu_hbm.at[layer, e], wgu_f, wsem.at[0]),
                pltpu.make_async_copy(wd_hbm.at[layer, e], wd_f, wsem.at[1]))

    @pl.when(i == 0)
    def _():
        gather(0, 0)
        for cp in weight_copies(be_ref[0]):
            cp.start(priority=1)

    @pl.when((i < nvalid) & ((i == 0) | (be_ref[i] != be_ref[jnp.maximum(i - 1, 0)])))
    def _():
        for cp in weight_copies(be_ref[i]):
            cp.wait()
        wgu_b[...] = wgu_f[...].astype(bf16)
        wd_b[...] = wd_f[...].astype(bf16)

        @pl.when(nxt_ref[i] >= 0)
        def _():
            for cp in weight_copies(nxt_ref[i]):
                cp.start(priority=1)

    @pl.when(i < nvalid)
    def _():
        pltpu.make_async_copy(xbuf.at[slot], xbuf.at[slot], xsem.at[slot]).wait()
        gather(jnp.minimum(i + 1, last_block), 1 - slot)
        x = jnp.concatenate([xbuf[slot, _token_chunk(j, blk, nchunk), :].astype(bf16) for j in range(nchunk)],
                            axis=-1)
        hgu = jnp.dot(x, wgu_b[...], preferred_element_type=f32) + bgu_ref[0, 0]
        hg = jnp.minimum(hgu[:, :dff], SWIGLU_LIMIT)
        hu = jnp.clip(hgu[:, dff:], -SWIGLU_LIMIT, SWIGLU_LIMIT)
        act = (hu + 1.0) * hg * jax.nn.sigmoid(SWIGLU_ALPHA * hg)
        _to_token_tiles(y_ref, _dot(act, wd_b[...]) + bd_ref[0, 0])

    @pl.when(i >= nvalid)
    def _():
        y_ref[...] = jnp.zeros_like(y_ref)

    @pl.when(i == nvalid)
    def _():
        pltpu.make_async_copy(xbuf.at[slot], xbuf.at[slot], xsem.at[slot]).wait()


def _experts(ht, row_tok, block_e, nvalid, next_e, w_gu, b_gu, w_down, b_down, l, blk):
    depth, n_exp, d, dff2 = w_gu.shape
    dff = dff2 // 2
    nchunk = d // LANES
    n_rows = row_tok.shape[0]
    bspec = lambda cols: pl.BlockSpec((1, 1, 1, cols), lambda i, be, *_: (l, be[i], 0, 0))
    grid_spec = pltpu.PrefetchScalarGridSpec(
        num_scalar_prefetch=4,
        grid=(n_rows // blk,),
        in_specs=[pl.BlockSpec(memory_space=pl.ANY), pl.BlockSpec(memory_space=pl.ANY), bspec(dff2),
                  pl.BlockSpec(memory_space=pl.ANY), bspec(d)],
        out_specs=pl.BlockSpec((blk * nchunk, LANES), lambda i, *_: (i, 0)),
        scratch_shapes=[pltpu.VMEM((2, blk * nchunk, LANES), f32), pltpu.SemaphoreType.DMA((2,)),
                        pltpu.VMEM((d, dff2), f32), pltpu.VMEM((dff, d), f32), pltpu.SemaphoreType.DMA((2,)),
                        pltpu.VMEM((d, dff2), bf16), pltpu.VMEM((dff, d), bf16)],
    )
    return pl.pallas_call(
        functools.partial(_expert_kernel, layer=l, blk=blk),
        grid_spec=grid_spec,
        out_shape=jax.ShapeDtypeStruct((n_rows * nchunk, LANES), f32),
        compiler_params=_params("arbitrary"),
        name="moe_experts",
    )(block_e, nvalid, next_e, row_tok, ht, w_gu, b_gu.reshape(depth, n_exp, 1, dff2), w_down,
      b_down.reshape(depth, n_exp, 1, d))


def _combine_kernel(dest_ref, gate_ref, xa_ref, xb_ref, yb_hbm, g_ref, oa_ref, ob_ref, ybuf, acc_s, sem,
                    *, tm, n, na_tiles, final):
    i = pl.program_id(0)
    slot = i % 2
    nchunk = ybuf.shape[2] // tm
    last_tile = pl.num_programs(0) - 1

    def copy_row(tile, into, kk, t, prio):
        row = dest_ref[kk * n + tile * tm + t]
        pltpu.make_async_copy(yb_hbm.at[_token_rows(row, nchunk)], ybuf.at[into, kk, _token_rows(t, nchunk)],
                              sem.at[into]).start(priority=prio)

    @pl.when(i == 0)
    def _():
        for kk in range(TOP_K):
            def issue(pair, c, kk=kk):
                for prio in range(DMA_PRIORITIES):
                    copy_row(0, 0, kk, pair * DMA_PRIORITIES + prio, prio)
                return c
            lax.fori_loop(0, tm // DMA_PRIORITIES, issue, 0, unroll=4)

    pltpu.make_async_copy(ybuf.at[slot], ybuf.at[slot], sem.at[slot]).wait()
    nxt = jnp.minimum(i + 1, last_tile)
    for kk in range(TOP_K):
        for t in range(tm):
            copy_row(nxt, 1 - slot, kk, t, t % DMA_PRIORITIES)
    for t in range(tm):
        first = (i * tm + t) * TOP_K
        acc = gate_ref[first] * ybuf[slot, 0, t * nchunk:(t + 1) * nchunk, :]
        for kk in range(1, TOP_K):
            acc = acc + gate_ref[first + kk] * ybuf[slot, kk, t * nchunk:(t + 1) * nchunk, :]
        acc_s[t * nchunk:(t + 1) * nchunk, :] = acc
    moe = jnp.concatenate([acc_s[_token_chunk(j, tm, nchunk), :] for j in range(nchunk)], axis=-1)
    x3 = jnp.where(i < na_tiles, xa_ref[...], xb_ref[...]) + moe
    out = _rms(x3, g_ref[...]) if final else x3

    @pl.when(i < na_tiles)
    def _():
        oa_ref[...] = out

    @pl.when(i >= na_tiles)
    def _():
        ob_ref[...] = out

    @pl.when(i == last_tile)
    def _():
        pltpu.make_async_copy(ybuf.at[1 - slot], ybuf.at[1 - slot], sem.at[1 - slot]).wait()


def _combine(dest_km, xa, xb, gate, yb, g_final, final):
    d = xa.shape[1]
    n = xa.shape[0] + xb.shape[0]
    tm = COMBINE_TILE
    assert xa.shape[0] % tm == 0 and xb.shape[0] % tm == 0
    na_tiles = xa.shape[0] // tm
    sa, sb = _two_group_specs(tm, d, na_tiles)
    nchunk = d // LANES
    grid_spec = pltpu.PrefetchScalarGridSpec(
        num_scalar_prefetch=2,
        grid=(n // tm,),
        in_specs=[sa, sb, pl.BlockSpec(memory_space=pl.ANY), pl.BlockSpec((1, d), lambda i, *_: (0, 0))],
        out_specs=[sa, sb],
        scratch_shapes=[pltpu.VMEM((2, TOP_K, tm * nchunk, LANES), f32), pltpu.VMEM((tm * nchunk, LANES), f32),
                        pltpu.SemaphoreType.DMA((2,))],
    )
    return pl.pallas_call(
        functools.partial(_combine_kernel, tm=tm, n=n, na_tiles=na_tiles, final=final),
        grid_spec=grid_spec,
        out_shape=[jax.ShapeDtypeStruct(xa.shape, f32), jax.ShapeDtypeStruct(xb.shape, f32)],
        compiler_params=_params("arbitrary"),
        name="moe_combine",
    )(dest_km, gate.reshape(-1), xa, xb, yb, g_final.reshape(1, d))


def _moe(xa, xb, g_ffn, router_w, router_b, w_gu, b_gu, w_down, b_down, l, g_out, final):
    n = xa.shape[0] + xb.shape[0]
    n_exp = router_w.shape[1]
    blk = MOE_ROWS
    ht, top_e, top_rank, gate, counts = _router(xa, xb, g_ffn, router_w, router_b)

    counts = counts.reshape(n_exp).astype(jnp.int32)
    padded = (counts + blk - 1) // blk * blk
    pad_end = jnp.cumsum(padded)
    pad_start = pad_end - padded
    lookup = lambda table, idx: jnp.sum(jnp.where(idx[..., None] == jnp.arange(n_exp), table, 0), axis=-1)
    dest_km = (lookup(pad_start, top_e.T) + top_rank.T).reshape(-1)
    n_rows = -(-(n * TOP_K) // blk) * blk + n_exp * blk
    block_start = jnp.arange(n_rows // blk, dtype=jnp.int32) * blk
    block_e = jnp.minimum(jnp.sum(pad_end[None, :] <= block_start[:, None], axis=1), n_exp - 1).astype(jnp.int32)
    nvalid = (pad_end[-1:] // blk).astype(jnp.int32)
    zero_start = jnp.concatenate([pad_start + counts, pad_end[-1:]]).astype(jnp.int32)
    zero_end = jnp.concatenate([pad_end, jnp.full((1,), n_rows, jnp.int32)]).astype(jnp.int32)

    has_rows = padded > 0
    ids = jnp.arange(n_exp, dtype=jnp.int32)
    after = jnp.where(has_rows[None, :] & (ids[None, :] > ids[:, None]), ids[None, :], n_exp)
    next_of = jnp.min(after, axis=1)
    next_e = lookup(jnp.where(next_of < n_exp, next_of, -1), block_e).astype(jnp.int32)

    row_tok = _invert(dest_km, zero_start, zero_end, n, n_rows)
    yb = _experts(ht, row_tok, block_e, nvalid, next_e, w_gu, b_gu, w_down, b_down, l, blk)
    return _combine(dest_km, xa, xb, gate, yb, g_out, final)


def _mixers(x, t, b, h0r, h0i, prev, s0, P, l):
    s5w = P['s5_d'].shape[-1]
    u, prk = _inproj(x, P['norm_mix_g'][l], P['w_in'][l].astype(bf16), s5w)
    ar, ai, bre, bim, cre_m, cim_m = _s5_discretize(P['s5_lambda_re'][l], P['s5_lambda_im'][l], P['s5_log_dt'][l],
                                                    P['s5_b_re'][l], P['s5_b_im'][l], P['s5_c_re'][l], P['s5_c_im'][l])
    s5y, hr, hi = _s5(u.reshape(b, t, s5w), h0r, h0i, ar, ai, bre, bim, cre_m, cim_m,
                      P['s5_d'][l].reshape(1, s5w).astype(f32), P['s5_glu_w'][l].astype(bf16),
                      P['s5_glu_b'][l].reshape(1, s5w).astype(f32))
    s5y = s5y.reshape(b * t, s5w)
    rwy, s_fin = _rwkv(prk, prev, s0, t, b, P['rwkv_mu'][l], P['rwkv_w0'][l], P['rwkv_w1'][l], P['rwkv_w2'][l],
                       P['rwkv_a0'][l], P['rwkv_a1'][l], P['rwkv_a2'][l], P['rwkv_g1'][l], P['rwkv_g2'][l],
                       P['rwkv_k_k'][l], P['rwkv_k_a'][l], P['rwkv_r_k'][l], P['rwkv_ln_w'][l], P['rwkv_ln_b'][l])
    x1, q = _outproj(x, s5y, rwy, P['w_out'][l].astype(bf16), P['norm_ca_g'][l], P['w_cq'][l].astype(bf16))
    shift_last = prk.reshape(b, t, -1)[:, t - 1]
    return x1, q, hr, hi, shift_last, s_fin


def kernel(x_prompt, x_sample, mem_prompt, state_s5_re, state_s5_im, state_rwkv_shift, state_rwkv_wkv, cache_mem_k, cache_mem_v, norm_mix_g, w_in, s5_lambda_re, s5_lambda_im, s5_log_dt, s5_b_re, s5_b_im, s5_c_re, s5_c_im, s5_d, s5_glu_w, s5_glu_b, rwkv_mu, rwkv_w0, rwkv_w1, rwkv_w2, rwkv_a0, rwkv_a1, rwkv_a2, rwkv_g1, rwkv_g2, rwkv_k_k, rwkv_k_a, rwkv_r_k, rwkv_ln_w, rwkv_ln_b, w_out, norm_ca_g, norm_mem_g, w_cq, w_mem_kv, w_co, norm_ffn_g, router_w, router_b, ex_w_gu, ex_b_gu, ex_w_down, ex_b_down, norm_final_g):
    P = dict(norm_mix_g=norm_mix_g, w_in=w_in, s5_lambda_re=s5_lambda_re, s5_lambda_im=s5_lambda_im,
             s5_log_dt=s5_log_dt, s5_b_re=s5_b_re, s5_b_im=s5_b_im, s5_c_re=s5_c_re, s5_c_im=s5_c_im,
             s5_d=s5_d, s5_glu_w=s5_glu_w, s5_glu_b=s5_glu_b, rwkv_mu=rwkv_mu, rwkv_w0=rwkv_w0,
             rwkv_w1=rwkv_w1, rwkv_w2=rwkv_w2, rwkv_a0=rwkv_a0, rwkv_a1=rwkv_a1, rwkv_a2=rwkv_a2,
             rwkv_g1=rwkv_g1, rwkv_g2=rwkv_g2, rwkv_k_k=rwkv_k_k, rwkv_k_a=rwkv_k_a, rwkv_r_k=rwkv_r_k,
             rwkv_ln_w=rwkv_ln_w, rwkv_ln_b=rwkv_ln_b, w_out=w_out, norm_ca_g=norm_ca_g, w_cq=w_cq)
    depth = norm_mix_g.shape[0]
    bp, tp, d = x_prompt.shape
    bs, ts, _ = x_sample.shape
    n_mem = mem_prompt.shape[1]
    s5_groups, s5_state = s5_lambda_re.shape[1:]
    nstate = s5_groups * s5_state
    rw_in = state_rwkv_shift.shape[-1]
    heads = state_rwkv_wkv.shape[2]
    mh = cache_mem_k.shape[3]
    assert mh == MEM_HEADS

    xp = x_prompt.reshape(bp * tp, d)
    xs = x_sample.reshape(bs * ts, d)
    outs = {k: [] for k in ('p_re', 'p_im', 'p_sh', 'p_wkv', 'p_mk', 'p_mv', 's_re', 's_im', 's_sh', 's_wkv')}
    for l in range(depth):
        mk, mv = _memkv(mem_prompt.reshape(bp * n_mem, d), norm_mem_g[l], w_mem_kv[l].astype(bf16))
        zeros_state = jnp.zeros((bp, nstate), f32)
        x1p, qp, re_p, im_p, sh_p, wkv_p = _mixers(
            xp, tp, bp, zeros_state, zeros_state, jnp.zeros((bp, rw_in), f32),
            jnp.zeros((bp, heads, RWKV_HEAD, RWKV_HEAD), f32), P, l)
        x1s, qs, re_s, im_s, sh_s, wkv_s = _mixers(
            xs, ts, bs, state_s5_re[l].reshape(bs, nstate), state_s5_im[l].reshape(bs, nstate),
            state_rwkv_shift[l], state_rwkv_wkv[l], P, l)
        wco = w_co[l].astype(bf16)
        x2p = _attn(qp, x1p, mk, mv, wco, tp, bp)
        x2s = _attn(qs, x1s, _kv_to_head_tiles(cache_mem_k[l]), _kv_to_head_tiles(cache_mem_v[l]), wco, ts, bs)
        last = l == depth - 1
        xp, xs = _moe(x2p, x2s, norm_ffn_g[l], router_w[l], router_b[l], ex_w_gu, ex_b_gu, ex_w_down, ex_b_down,
                      l, norm_final_g, last)
        outs['p_re'].append(re_p.reshape(bp, s5_groups, s5_state))
        outs['p_im'].append(im_p.reshape(bp, s5_groups, s5_state))
        outs['p_sh'].append(sh_p)
        outs['p_wkv'].append(wkv_p)
        outs['p_mk'].append(_kv_from_head_tiles(mk, bp, n_mem, mh, d // mh))
        outs['p_mv'].append(_kv_from_head_tiles(mv, bp, n_mem, mh, d // mh))
        outs['s_re'].append(re_s.reshape(bs, s5_groups, s5_state))
        outs['s_im'].append(im_s.reshape(bs, s5_groups, s5_state))
        outs['s_sh'].append(sh_s)
        outs['s_wkv'].append(wkv_s)
    st = lambda k: jnp.stack(outs[k])
    return (xp.reshape(bp, tp, d), xs.reshape(bs, ts, d), st('p_re'), st('p_im'), st('p_sh'), st('p_wkv'),
            st('p_mk'), st('p_mv'), st('s_re'), st('s_im'), st('s_sh'), st('s_wkv'))
```

```python
import functools

import jax
import jax.numpy as jnp
from jax import lax
from jax.experimental import pallas as pl
from jax.experimental.pallas import tpu as pltpu

f32 = jnp.float32
bf16 = jnp.bfloat16

RWKV_HEAD = 64
HEAD_PAIR = 2 * RWKV_HEAD
DECAY_SCALE = 0.6065306597
MEM_HEADS = 4
TOP_K = 4
SWIGLU_LIMIT = 7.0
SWIGLU_ALPHA = 1.702
RMS_EPS = 1e-5
GN_EPS = 64e-5

V7X_VMEM_LIMIT = 56 * 1024 * 1024
ROW_TILE = 512
MOE_ROWS = 256
COMBINE_TILE = 256
WKV_CHUNK = 64
WKV_BATCH = 4
S5_CHUNK = 64
S5_BLOCK_CHANNELS = 128
LANES = 128
DMA_PRIORITIES = 2
ATTN_SHORT_BATCH = 4

_NT = (((1,), (1,)), ((), ()))
_TN = (((0,), (0,)), ((), ()))


def _params(*sem):
    return pltpu.CompilerParams(dimension_semantics=sem, vmem_limit_bytes=V7X_VMEM_LIMIT)


def _dot(a, b):
    return jnp.dot(a.astype(bf16), b.astype(bf16), preferred_element_type=f32)


def _dg(a, b, dims):
    return lax.dot_general(a.astype(bf16), b.astype(bf16), dims, preferred_element_type=f32)


def _rms(x, g):
    return x * lax.rsqrt(jnp.mean(x * x, axis=-1, keepdims=True) + RMS_EPS) * g


def _full(shape):
    n = len(shape)
    return pl.BlockSpec(shape, lambda *_: (0,) * n)


def _tile(rows, tile=ROW_TILE):
    t = min(tile, rows)
    assert rows % t == 0
    return t


def _inproj_kernel(x_ref, g_ref, w_ref, u_ref, p_ref, *, s5w):
    nb = _rms(x_ref[...], g_ref[...]).astype(bf16)
    u_ref[...] = jnp.dot(nb, w_ref[:, :s5w], preferred_element_type=f32)
    p_ref[...] = jnp.dot(nb, w_ref[:, s5w:], preferred_element_type=f32)


def _inproj(x, g, w, s5w):
    rows, d = x.shape
    inw = w.shape[1]
    tm = _tile(rows)
    return pl.pallas_call(
        functools.partial(_inproj_kernel, s5w=s5w),
        grid=(rows // tm,),
        in_specs=[pl.BlockSpec((tm, d), lambda i: (i, 0)), _full((1, d)), _full((d, inw))],
        out_specs=[pl.BlockSpec((tm, s5w), lambda i: (i, 0)), pl.BlockSpec((tm, inw - s5w), lambda i: (i, 0))],
        out_shape=[jax.ShapeDtypeStruct((rows, s5w), f32), jax.ShapeDtypeStruct((rows, inw - s5w), f32)],
        compiler_params=_params("parallel"),
        name="inproj",
    )(x, g.reshape(1, d), w)


def _s5_kernel(u_hbm, h0r_ref, h0i_ref, ar_ref, ai_ref, bre_ref, bim_ref, cre_ref, cim_ref, d_ref, gw_ref, gb_ref,
               y_hbm, hr_ref, hi_ref, xr_s, xi_s, ubuf, usem, ybuf, ysem, *, steps, bt, lane_chunk):
    batch_tile = pl.program_id(0)
    i = pl.program_id(1)
    nchunks = pl.num_programs(1)
    step = batch_tile * nchunks + i
    last_step = pl.num_programs(0) * nchunks - 1
    slot = step % 2
    width = ubuf.shape[-1]
    nstate = ar_ref.shape[-1]

    def chunk_copies(tile, chunk, s, load):
        t0 = pl.multiple_of(chunk * steps, steps)
        for b in range(bt):
            hbm = (u_hbm if load else y_hbm).at[tile * bt + b, pl.ds(t0, steps), :]
            if load:
                yield pltpu.make_async_copy(hbm, ubuf.at[s, :, b, :], usem.at[s])
            else:
                yield pltpu.make_async_copy(ybuf.at[s, :, b, :], hbm, ysem.at[s])

    @pl.when(i == 0)
    def _():
        hr_ref[...] = h0r_ref[...]
        hi_ref[...] = h0i_ref[...]

    @pl.when(step == 0)
    def _():
        for cp in chunk_copies(0, 0, 0, True):
            cp.start()

    @pl.when(step < last_step)
    def _():
        wraps = i + 1 == nchunks
        nxt_tile = jnp.where(wraps, batch_tile + 1, batch_tile)
        nxt_chunk = jnp.where(wraps, 0, i + 1)
        for cp in chunk_copies(nxt_tile, nxt_chunk, 1 - slot, True):
            cp.start()

    @pl.when(step >= 2)
    def _():
        for cp in chunk_copies(batch_tile, i, slot, False):
            cp.wait()

    for cp in chunk_copies(batch_tile, i, slot, True):
        cp.wait()
    u = ubuf[slot].reshape(steps * bt, width)
    ub = u.astype(bf16)
    nblk, cb, sb = bre_ref.shape
    for j in range(nblk):
        uj = ub[:, j * cb:(j + 1) * cb]
        xr_s[:, j * sb:(j + 1) * sb] = jnp.dot(uj, bre_ref[j], preferred_element_type=f32)
        xi_s[:, j * sb:(j + 1) * sb] = jnp.dot(uj, bim_ref[j], preferred_element_type=f32)

    for c in range(nstate // lane_chunk):
        sl = slice(c * lane_chunk, (c + 1) * lane_chunk)
        arb = jnp.broadcast_to(ar_ref[:, sl], (bt, lane_chunk))
        aib = jnp.broadcast_to(ai_ref[:, sl], (bt, lane_chunk))

        hr, hi = hr_ref[:, sl], hi_ref[:, sl]
        for t in range(steps):
            rows = slice(t * bt, (t + 1) * bt)
            hr, hi = (arb * hr - aib * hi + xr_s[rows, sl], arb * hi + aib * hr + xi_s[rows, sl])
            xr_s[rows, sl] = hr
            xi_s[rows, sl] = hi
        hr_ref[:, sl] = hr
        hi_ref[:, sl] = hi

    y = jnp.concatenate(
        [jnp.dot(xr_s[:, j * sb:(j + 1) * sb].astype(bf16), cre_ref[j], preferred_element_type=f32)
         + jnp.dot(xi_s[:, j * sb:(j + 1) * sb].astype(bf16), cim_ref[j], preferred_element_type=f32)
         for j in range(nblk)], axis=-1) + d_ref[...] * u
    z = jax.nn.gelu(y)
    out = z * jax.nn.sigmoid(_dot(z, gw_ref[...]) + gb_ref[...])
    ybuf[slot] = out.reshape(steps, bt, width)
    for cp in chunk_copies(batch_tile, i, slot, False):
        cp.start()

    @pl.when(step == last_step)
    def _():
        for cp in chunk_copies(batch_tile, i, slot, False):
            cp.wait()

        @pl.when(step >= 1)
        def _():
            for cp in chunk_copies(batch_tile, i, 1 - slot, False):
                cp.wait()


def _s5(u3, h0r, h0i, ar, ai, bre, bim, cre_m, cim_m, d, gw, gb):
    b, t, width = u3.shape
    nstate = ar.shape[-1]
    bt = 8
    steps = min(S5_CHUNK, t)
    assert b % bt == 0 and t % steps == 0
    kern = functools.partial(_s5_kernel, steps=steps, bt=bt, lane_chunk=512)
    st_spec = pl.BlockSpec((bt, nstate), lambda j, i: (j, 0))
    y, hr, hi = pl.pallas_call(
        kern,
        grid=(b // bt, t // steps),
        in_specs=[pl.BlockSpec(memory_space=pl.ANY), st_spec, st_spec,
                  _full((1, nstate)), _full((1, nstate)), _full(bre.shape), _full(bim.shape),
                  _full(cre_m.shape), _full(cim_m.shape), _full((1, width)), _full((width, width)),
                  _full((1, width))],
        out_specs=[pl.BlockSpec(memory_space=pl.ANY), st_spec, st_spec],
        out_shape=[jax.ShapeDtypeStruct((b, t, width), f32), jax.ShapeDtypeStruct((b, nstate), f32),
                   jax.ShapeDtypeStruct((b, nstate), f32)],
        scratch_shapes=[pltpu.VMEM((steps * bt, nstate), f32), pltpu.VMEM((steps * bt, nstate), f32),
                        pltpu.VMEM((2, steps, bt, width), f32), pltpu.SemaphoreType.DMA((2,)),
                        pltpu.VMEM((2, steps, bt, width), f32), pltpu.SemaphoreType.DMA((2,))],
        compiler_params=_params("arbitrary", "arbitrary"),
        name="s5_mixer",
    )(u3, h0r, h0i, ar, ai, bre, bim, cre_m, cim_m, d, gw, gb)
    return y, hr, hi


def _s5_discretize(lam_re, lam_im, log_dt, b_re, b_im, c_re, c_im):
    g, p = lam_re.shape
    dt = jnp.exp(log_dt.astype(f32))[:, None]
    lr = lam_re.astype(f32)
    li = lam_im.astype(f32)
    mag = jnp.exp(lr * dt)
    ar = mag * jnp.cos(li * dt)
    ai = mag * jnp.sin(li * dt)
    den = lr * lr + li * li
    cr = ((ar - 1.0) * lr + ai * li) / den
    ci = (ai * lr - (ar - 1.0) * li) / den
    br = b_re.astype(f32)
    bi = b_im.astype(f32)
    bbar_r = cr[..., None] * br - ci[..., None] * bi
    bbar_i = cr[..., None] * bi + ci[..., None] * br
    c = br.shape[-1]
    gb = S5_BLOCK_CHANNELS // c
    assert g % gb == 0
    eye = jnp.eye(gb, dtype=f32)
    nblk = g // gb
    bd_in = lambda m: jnp.einsum('ngpc,gh->ngchp', m.reshape(nblk, gb, p, c), eye).reshape(
        nblk, gb * c, gb * p).astype(bf16)
    bd_out = lambda m: jnp.einsum('ngcp,gh->ngphc', m.reshape(nblk, gb, c, p), eye).reshape(
        nblk, gb * p, gb * c).astype(bf16)
    return (ar.reshape(1, g * p), ai.reshape(1, g * p), bd_in(bbar_r), bd_in(bbar_i),
            bd_out(c_re.astype(f32)), bd_out(-c_im.astype(f32)))


def _rwkv_kernel(p_ref, prev_ref, s0_ref, mu_ref, w0_ref, w1_ref, w2_ref, a0_ref, a1_ref, a2_ref, g1_ref, g2_ref,
                 kk_ref, ka_ref, rk_ref, lnw_ref, lnb_ref, y_ref, sout_ref, prev_s, st_s, *, steps, nsq):
    i = pl.program_id(1)
    L = steps
    L2 = 2 * L
    nb = p_ref.shape[0]
    w = p_ref.shape[-1] // 4
    npair = w // HEAD_PAIR

    row2 = lax.broadcasted_iota(jnp.int32, (L2, HEAD_PAIR), 0)
    lane2 = lax.broadcasted_iota(jnp.int32, (L2, HEAD_PAIR), 1)
    own = (row2 < L) == (lane2 < RWKV_HEAD)
    ri = lax.broadcasted_iota(jnp.int32, (L2, L2), 0)
    ci = lax.broadcasted_iota(jnp.int32, (L2, L2), 1)
    same = (ri < L) == (ci < L)
    strict = same & (ci < ri)
    incl = same & (ci <= ri)
    eye = (ri == ci).astype(f32)
    sr = lax.broadcasted_iota(jnp.int32, (HEAD_PAIR, HEAD_PAIR), 0)
    sc = lax.broadcasted_iota(jnp.int32, (HEAD_PAIR, HEAD_PAIR), 1)
    sdiag = (sr < RWKV_HEAD) == (sc < RWKV_HEAD)

    @pl.when(i == 0)
    def _():
        prev_s[...] = prev_ref[...]
        for bb in range(nb):
            for pr in range(npair):
                sp = s0_ref[bb, pr * HEAD_PAIR:(pr + 1) * HEAD_PAIR, :]
                st_s[bb * npair + pr] = jnp.where(sdiag, jnp.concatenate([sp, sp], axis=1), 0.0)

    trow = lax.broadcasted_iota(jnp.int32, (L, 1), 0)
    ps, deltas = [], []
    for bb in range(nb):
        pb = p_ref[bb]
        shifted = jnp.where(trow == 0, prev_s[bb], pltpu.roll(pb, 1, axis=0))
        prev_s[bb] = pb[L - 1:L, :]
        ps.append(pb)
        deltas.append(shifted - pb)
    p = jnp.concatenate(ps, axis=0) if nb > 1 else ps[0]
    delta = jnp.concatenate(deltas, axis=0) if nb > 1 else deltas[0]
    mu = mu_ref[...]
    pz = p[:, 3 * w:]
    dz = delta[:, 3 * w:]
    r = p[:, :w] + delta[:, :w] * mu[0:1]
    k = p[:, w:2 * w] + delta[:, w:2 * w] * mu[1:2]
    v = p[:, 2 * w:3 * w] + delta[:, 2 * w:3 * w] * mu[2:3]
    zw = pz + dz * mu[3:4]
    za = pz + dz * mu[4:5]
    zg = pz + dz * mu[5:6]
    logw = -DECAY_SCALE * jax.nn.sigmoid(w0_ref[...] + _dot(jnp.tanh(_dot(zw, w1_ref[...])), w2_ref[...]))
    a = jax.nn.sigmoid(a0_ref[...] + _dot(_dot(za, a1_ref[...]), a2_ref[...]))
    g = _dot(jax.nn.sigmoid(_dot(zg, g1_ref[...])), g2_ref[...])
    kkraw = k * kk_ref[...]
    kmod = k * (1.0 + (a - 1.0) * ka_ref[...])
    rk = r * kmod * rk_ref[...]

    tr = lax.broadcasted_iota(jnp.int32, (nb * L, nb * L), 0)
    tc = lax.broadcasted_iota(jnp.int32, (nb * L, nb * L), 1)
    tri = ((tc <= tr) & (tc >= (tr // L) * L)).astype(bf16)
    hi = logw.astype(bf16)
    rem = logw - hi.astype(f32)
    mid = rem.astype(bf16)
    lo = (rem - mid.astype(f32)).astype(bf16)
    cum = (jnp.dot(tri, hi, preferred_element_type=f32) + jnp.dot(tri, mid, preferred_element_type=f32)
           + jnp.dot(tri, lo, preferred_element_type=f32))

    stack = lambda x: jnp.concatenate([x, x], axis=0)
    chains = [(bb, pr) for bb in range(nb) for pr in range(npair)]
    nch = range(len(chains))
    rs = [slice(bb * L, (bb + 1) * L) for bb, _ in chains]
    sls = [slice(pr * HEAD_PAIR, (pr + 1) * HEAD_PAIR) for _, pr in chains]
    cut = lambda x, c: stack(x[rs[c], sls[c]])
    own_of = lambda x, c: jnp.where(own, cut(x, c), 0.0)

    wt, rm, km, bm, kap, vmf, vm, gram = [], [], [], [], [], [], [], []
    for c in nch:
        cum_c = cut(cum, c)
        wt_c = jnp.exp(cum_c)
        winv = jnp.exp(-cum_c)
        wprev = jnp.exp(cum_c - cut(logw, c))
        kkr = own_of(kkraw, c)
        kkn = kkr * lax.rsqrt(jnp.maximum(jnp.sum(kkr * kkr, axis=-1, keepdims=True), 1e-24))
        wt.append(wt_c)
        rm.append((own_of(r, c) * wt_c).astype(bf16))
        km.append((own_of(kmod, c) * winv).astype(bf16))
        bm.append((kkn * cut(a, c) * winv).astype(bf16))
        kap.append((kkn * wprev).astype(bf16))
        vmf.append(own_of(v, c))
        vm.append(vmf[c].astype(bf16))
        gram.append(_dg(jnp.concatenate([kap[c], rm[c]], axis=0), jnp.concatenate([bm[c], km[c]], axis=0), _NT))
    n_mat = [jnp.where(strict, x[:L2, :L2], 0.0) for x in gram]
    p_mat = [jnp.where(strict, x[:L2, L2:], 0.0) for x in gram]
    g_mat = [jnp.where(incl, x[L2:, :L2], 0.0) for x in gram]
    q_mat = [jnp.where(incl, x[L2:, L2:], 0.0) for x in gram]

    neg = [-x for x in n_mat]
    inv = [eye + x for x in neg]
    if nsq > 0:
        pw = [_dot(x, x) for x in neg]
        for _ in range(nsq - 1):
            both = [_dot(pw[c], jnp.concatenate([inv[c], pw[c]], axis=1)) for c in nch]
            inv = [inv[c] + both[c][:, :L2] for c in nch]
            pw = [x[:, L2:] for x in both]
        inv = [inv[c] + _dot(pw[c], inv[c]) for c in nch]

    s_old = [st_s[c] for c in nch]
    rhs = [_dg(kap[c], s_old[c], _NT) + _dot(p_mat[c], vm[c]) for c in nch]
    ym0 = [_dg(rm[c], s_old[c], _NT) for c in nch]
    um = [(-_dot(inv[c], rhs[c])).astype(bf16) for c in nch]
    vu = [jnp.concatenate([vm[c], um[c]], axis=0) for c in nch]
    ym = [ym0[c] + _dot(jnp.concatenate([q_mat[c], g_mat[c]], axis=1), vu[c]) for c in nch]
    for c in nch:
        st_s[c] = (s_old[c] + _dg(vu[c], jnp.concatenate([km[c], bm[c]], axis=0), _TN)) * wt[c][L - 1:L, :]

    for c, (bb, pr) in enumerate(chains):
        mean = jnp.sum(ym[c], axis=-1, keepdims=True) * (1.0 / RWKV_HEAD)
        yc = jnp.where(own, ym[c] - mean, 0.0)
        var = jnp.sum(yc * yc, axis=-1, keepdims=True) * (1.0 / RWKV_HEAD)
        yn = yc * lax.rsqrt(var + GN_EPS)
        bonus = jnp.sum(own_of(rk, c), axis=-1, keepdims=True) * vmf[c]
        mix = (yn[:L] + yn[L:]) * lnw_ref[:, sls[c]] + lnb_ref[:, sls[c]] + (bonus[:L] + bonus[L:])
        y_ref[bb, :, sls[c]] = mix * g[rs[c], sls[c]]

    @pl.when(i == pl.num_programs(1) - 1)
    def _():
        for c, (bb, pr) in enumerate(chains):
            s_new = st_s[c]
            sout_ref[bb, pr * HEAD_PAIR:(pr + 1) * HEAD_PAIR, :] = s_new[:, :RWKV_HEAD] + s_new[:, RWKV_HEAD:]


def _rwkv(prk, prev, s0, t, b, mu, w0, w1, w2, a0, a1, a2, g1, g2, k_k, k_a, r_k, ln_w, ln_b):
    w = prk.shape[-1] // 4
    steps = min(WKV_CHUNK, t)
    nb = min(WKV_BATCH * max(1, WKV_CHUNK // (4 * steps)), b)
    assert t % steps == 0 and w % HEAD_PAIR == 0 and b % nb == 0
    nsq = max((steps - 1).bit_length() - 1, 0)
    row = lambda x: x.reshape(1, -1).astype(f32)
    wb = lambda x: x.astype(bf16)
    weights = [mu.astype(f32), row(w0), wb(w1), wb(w2), row(a0), wb(a1), wb(a2), wb(g1), wb(g2),
               row(k_k), row(k_a), row(r_k), row(ln_w), row(ln_b)]
    state_spec = pl.BlockSpec((nb, w, RWKV_HEAD), lambda j, i: (j, 0, 0))
    y, s_fin = pl.pallas_call(
        functools.partial(_rwkv_kernel, steps=steps, nsq=nsq),
        grid=(b // nb, t // steps),
        in_specs=[pl.BlockSpec((nb, steps, 4 * w), lambda j, i: (j, i, 0)),
                  pl.BlockSpec((nb, 1, 4 * w), lambda j, i: (j, 0, 0)), state_spec]
                 + [_full(x.shape) for x in weights],
        out_specs=[pl.BlockSpec((nb, steps, w), lambda j, i: (j, i, 0)), state_spec],
        out_shape=[jax.ShapeDtypeStruct((b, t, w), f32), jax.ShapeDtypeStruct((b, w, RWKV_HEAD), f32)],
        scratch_shapes=[pltpu.VMEM((nb, 1, 4 * w), f32),
                        pltpu.VMEM((nb * (w // HEAD_PAIR), HEAD_PAIR, HEAD_PAIR), f32)],
        compiler_params=_params("arbitrary", "arbitrary"),
        name="rwkv7_mixer",
    )(prk.reshape(b, t, 4 * w), prev.reshape(b, 1, 4 * w).astype(f32), s0.reshape(b, w, RWKV_HEAD).astype(f32),
      *weights)
    return y.reshape(b * t, w), s_fin.reshape(b, w // RWKV_HEAD, RWKV_HEAD, RWKV_HEAD)


def _outproj_kernel(x_ref, s5_ref, rw_ref, wo_ref, g_ref, wq_ref, x1_ref, q_ref, *, s5w):
    x1 = x_ref[...] + _dot(s5_ref[...], wo_ref[:s5w, :]) + _dot(rw_ref[...], wo_ref[s5w:, :])
    x1_ref[...] = x1
    q_ref[...] = _dot(_rms(x1, g_ref[...]), wq_ref[...]).astype(bf16)


def _outproj(x, s5y, rwy, w_out, g_ca, w_cq):
    rows, d = x.shape
    s5w = s5y.shape[1]
    rww = rwy.shape[1]
    tm = _tile(rows)
    rt = lambda c: pl.BlockSpec((tm, c), lambda i: (i, 0))
    return pl.pallas_call(
        functools.partial(_outproj_kernel, s5w=s5w),
        grid=(rows // tm,),
        in_specs=[rt(d), rt(s5w), rt(rww), _full(w_out.shape), _full((1, d)), _full(w_cq.shape)],
        out_specs=[rt(d), rt(d)],
        out_shape=[jax.ShapeDtypeStruct((rows, d), f32), jax.ShapeDtypeStruct((rows, d), bf16)],
        compiler_params=_params("parallel"),
        name="outproj_q",
    )(x, s5y, rwy, w_out, g_ca.reshape(1, d), w_cq)


def _kv_to_head_tiles(x):
    b, m, h, hd = x.shape
    return x.reshape(b, m, h, hd // LANES, LANES).transpose(0, 1, 3, 2, 4).reshape(b * m * (hd // LANES) * h, LANES)


def _kv_from_head_tiles(x, b, m, h, hd):
    return x.reshape(b, m, hd // LANES, h, LANES).transpose(0, 1, 3, 2, 4).reshape(b, m, h, hd)


def _memkv_kernel(m_ref, g_ref, w_ref, k_ref, v_ref):
    tm, d = m_ref.shape
    hd = d // MEM_HEADS
    parts = hd // LANES
    per_token = parts * MEM_HEADS
    nb = _rms(m_ref[...], g_ref[...]).astype(bf16)
    for ref, cols in ((k_ref, slice(0, d)), (v_ref, slice(d, 2 * d))):
        kv = jnp.dot(nb, w_ref[:, cols], preferred_element_type=f32)
        for h in range(MEM_HEADS):
            for c in range(parts):
                lo = h * hd + c * LANES
                ref[pl.ds(c * MEM_HEADS + h, tm, stride=per_token), :] = kv[:, lo:lo + LANES]


def _memkv(mem, g, w):
    rows, d = mem.shape
    tm = _tile(rows)
    per_token = d // LANES
    out = pl.BlockSpec((tm * per_token, LANES), lambda i: (i, 0))
    return pl.pallas_call(
        _memkv_kernel,
        grid=(rows // tm,),
        in_specs=[pl.BlockSpec((tm, d), lambda i: (i, 0)), _full((1, d)), _full(w.shape)],
        out_specs=[out, out],
        out_shape=[jax.ShapeDtypeStruct((rows * per_token, LANES), f32)] * 2,
        compiler_params=_params("parallel"),
        name="memory_kv",
    )(mem, g.reshape(1, d), w)


def _attn_kernel(q_ref, x1_ref, k_ref, v_ref, wo_ref, x2_ref, *, nbb):
    d = q_ref.shape[-1]
    hd = d // MEM_HEADS
    parts = hd // LANES
    per_token = parts * MEM_HEADS
    n_mem = k_ref.shape[0] // (per_token * nbb)
    rows = q_ref.shape[0] // nbb
    scale = hd ** -0.5
    q = q_ref[...]

    def head(ref, bb, h):
        first = bb * n_mem * per_token
        return jnp.concatenate([ref[pl.ds(first + c * MEM_HEADS + h, n_mem, stride=per_token), :].astype(bf16)
                                for c in range(parts)], axis=-1)

    heads = []
    for h in range(MEM_HEADS):
        s = [_dg(q[bb * rows:(bb + 1) * rows, h * hd:(h + 1) * hd], head(k_ref, bb, h), _NT) * scale
             for bb in range(nbb)]
        e = [jnp.exp(x - jnp.max(x, axis=-1, keepdims=True)) for x in s]
        prob = [x / jnp.sum(x, axis=-1, keepdims=True) for x in e]
        o = [_dot(prob[bb], head(v_ref, bb, h)) for bb in range(nbb)]
        heads.append(jnp.concatenate(o, axis=0) if nbb > 1 else o[0])
    o = jnp.concatenate(heads, axis=-1)
    x2_ref[...] = x1_ref[...] + _dot(o, wo_ref[...])


def _attn(q, x1, mk, mv, w_co, t, b):
    d = q.shape[-1]
    rows_per_b = mk.shape[0] // b
    tt = min(ROW_TILE, t)
    assert t % tt == 0
    nt = t // tt
    nbb = min(ATTN_SHORT_BATCH, b) if (nt == 1 and tt * ATTN_SHORT_BATCH <= ROW_TILE) else 1
    assert b % nbb == 0
    xt = pl.BlockSpec((nbb * tt, d), lambda j, i: (j * nt + i, 0))
    kv = pl.BlockSpec((nbb * rows_per_b, LANES), lambda j, i: (j, 0))
    return pl.pallas_call(
        functools.partial(_attn_kernel, nbb=nbb),
        grid=(b // nbb, nt),
        in_specs=[xt, xt, kv, kv, _full(w_co.shape)],
        out_specs=xt,
        out_shape=jax.ShapeDtypeStruct((b * t, d), f32),
        compiler_params=_params("parallel", "parallel"),
        name="mem_attention",
    )(q, x1, mk, mv, w_co)


def _two_group_specs(tm, cols, na_tiles):
    a = pl.BlockSpec((tm, cols), lambda i, *_: (jnp.minimum(i, na_tiles - 1), 0))
    b = pl.BlockSpec((tm, cols), lambda i, *_: (jnp.maximum(i - na_tiles, 0), 0))
    return a, b


def _split2(x):
    hi = x.astype(bf16)
    return hi, (x - hi.astype(f32)).astype(bf16)


def _token_chunk(j, rows, nchunk, first=0):
    return pl.ds(first * nchunk + j, rows, stride=nchunk)


def _to_token_tiles(ref, x, first=0):
    rows, d = x.shape
    nchunk = d // LANES
    for j in range(nchunk):
        ref[_token_chunk(j, rows, nchunk, first), :] = x[:, j * LANES:(j + 1) * LANES]


def _token_rows(t, nchunk):
    return pl.ds(pl.multiple_of(t * nchunk, nchunk), nchunk)


def _router_kernel(xa_ref, xb_ref, g_ref, rwt_ref, rb_ref, ht_ref, e_ref, rank_ref, gate_ref, cnt_ref,
                   *, n_exp, na_tiles):
    i = pl.program_id(0)
    tm, d = xa_ref.shape

    @pl.when(i == 0)
    def _():
        cnt_ref[...] = jnp.zeros_like(cnt_ref)

    h = _rms(jnp.where(i < na_tiles, xa_ref[...], xb_ref[...]), g_ref[...])
    _to_token_tiles(ht_ref, h)

    hh, hl = _split2(h)
    wh, wl = _split2(rwt_ref[...])
    nt = lambda a, b: lax.dot_general(a, b, _NT, preferred_element_type=f32)
    logits = nt(wh, hh) + nt(wl, hh) + nt(wh, hl) + rb_ref[...]

    expert = lax.broadcasted_iota(jnp.int32, (n_exp, tm), 0)
    work = logits
    sels, tops = [], []
    for _ in range(TOP_K):
        m = jnp.max(work, axis=0, keepdims=True)
        idx = jnp.min(jnp.where(work == m, expert, n_exp), axis=0, keepdims=True)
        sel = expert == idx
        sels.append(sel)
        tops.append(m)
        work = jnp.where(sel, -jnp.inf, work)
    chosen = sels[0] | sels[1] | sels[2] | sels[3]

    onehot = jnp.where(chosen, 1.0, 0.0)
    rr = lax.broadcasted_iota(jnp.int32, (tm, tm), 0)
    cc = lax.broadcasted_iota(jnp.int32, (tm, tm), 1)
    earlier = (rr < cc).astype(bf16)
    rank = jnp.dot(onehot.astype(bf16), earlier, preferred_element_type=f32) + cnt_ref[...]
    cnt_ref[...] = cnt_ref[...] + jnp.sum(onehot, axis=1, keepdims=True)

    den = sum(jnp.exp(t - tops[0]) for t in tops)
    e_rows, r_rows, g_rows = [], [], []
    for sel, top in zip(sels, tops):
        e_rows.append(jnp.sum(jnp.where(sel, expert, 0), axis=0, keepdims=True))
        r_rows.append(jnp.sum(jnp.where(sel, rank, 0.0), axis=0, keepdims=True))
        g_rows.append(jnp.exp(top - tops[0]) / den)
    e_ref[...] = jnp.concatenate(e_rows, axis=0)
    rank_ref[...] = jnp.concatenate(r_rows, axis=0).astype(jnp.int32)
    gate_ref[...] = jnp.concatenate(g_rows, axis=0)


def _router(xa, xb, g, rw, rb):
    d = xa.shape[1]
    n = xa.shape[0] + xb.shape[0]
    n_exp = rw.shape[1]
    tm = ROW_TILE
    assert xa.shape[0] % tm == 0 and xb.shape[0] % tm == 0
    na_tiles = xa.shape[0] // tm
    sa, sb = _two_group_specs(tm, d, na_tiles)
    per_k = pl.BlockSpec((TOP_K, tm), lambda i: (0, i))
    return pl.pallas_call(
        functools.partial(_router_kernel, n_exp=n_exp, na_tiles=na_tiles),
        grid=(n // tm,),
        in_specs=[sa, sb, _full((1, d)), _full((n_exp, d)), _full((n_exp, 1))],
        out_specs=[pl.BlockSpec((tm * (d // LANES), LANES), lambda i: (i, 0)), per_k, per_k, per_k,
                   _full((n_exp, 1))],
        out_shape=[jax.ShapeDtypeStruct((n * (d // LANES), LANES), f32), jax.ShapeDtypeStruct((TOP_K, n), jnp.int32),
                   jax.ShapeDtypeStruct((TOP_K, n), jnp.int32), jax.ShapeDtypeStruct((TOP_K, n), f32),
                   jax.ShapeDtypeStruct((n_exp, 1), f32)],
        compiler_params=_params("arbitrary"),
        name="moe_router",
    )(xa, xb, g.reshape(1, d), rw.astype(f32).T, rb.reshape(n_exp, 1).astype(f32))


def _invert_kernel(dest_ref, zs_ref, ze_ref, tok_ref, *, n):
    def for_each_unmapped_row(fn):
        def segment(seg, c):
            return lax.fori_loop(zs_ref[seg], ze_ref[seg], fn, c)
        lax.fori_loop(0, zs_ref.shape[0], segment, 0)

    fill = lambda r, c: _set(tok_ref, r, 0, c)
    for_each_unmapped_row(fill)
    for kk in range(TOP_K):
        def place(t, c, kk=kk):
            return _set(tok_ref, dest_ref[kk * n + t], t, c)
        lax.fori_loop(0, n, place, 0, unroll=8)


def _set(ref, idx, val, carry):
    ref[idx] = val
    return carry


def _invert(dest_km, zero_start, zero_end, n, n_rows):
    grid_spec = pltpu.PrefetchScalarGridSpec(
        num_scalar_prefetch=3, grid=(1,), in_specs=[],
        out_specs=pl.BlockSpec(memory_space=pltpu.SMEM))
    return pl.pallas_call(
        functools.partial(_invert_kernel, n=n),
        grid_spec=grid_spec,
        out_shape=jax.ShapeDtypeStruct((n_rows,), jnp.int32),
        compiler_params=_params("arbitrary"),
        name="moe_invert",
    )(dest_km, zero_start, zero_end)


def _expert_kernel(be_ref, nv_ref, nxt_ref, tok_ref, ht_hbm, wgu_hbm, bgu_ref, wd_hbm, bd_ref, y_ref,
                   xbuf, xsem, wgu_f, wd_f, wsem, wgu_b, wd_b, *, layer, blk):
    i = pl.program_id(0)
    slot = i % 2
    nvalid = nv_ref[0]
    dff = wd_b.shape[0]
    nchunk = wgu_b.shape[0] // LANES
    last_block = pl.num_programs(0) - 1

    def gather(block, into):
        for r in range(blk):
            tok = tok_ref[block * blk + r]
            pltpu.make_async_copy(ht_hbm.at[_token_rows(tok, nchunk)], xbuf.at[into, _token_rows(r, nchunk)],
                                  xsem.at[into]).start(priority=0)

    def weight_copies(e):
        return (pltpu.make_async_copy(wgu_hbm.at[layer, e], wgu_f, wsem.at[0]),
                pltpu.make_async_copy(wd_hbm.at[layer, e], wd_f, wsem.at[1]))

    @pl.when(i == 0)
    def _():
        gather(0, 0)
        for cp in weight_copies(be_ref[0]):
            cp.start(priority=1)

    @pl.when((i < nvalid) & ((i == 0) | (be_ref[i] != be_ref[jnp.maximum(i - 1, 0)])))
    def _():
        for cp in weight_copies(be_ref[i]):
            cp.wait()
        wgu_b[...] = wgu_f[...].astype(bf16)
        wd_b[...] = wd_f[...].astype(bf16)

        @pl.when(nxt_ref[i] >= 0)
        def _():
            for cp in weight_copies(nxt_ref[i]):
                cp.start(priority=1)

    @pl.when(i < nvalid)
    def _():
        pltpu.make_async_copy(xbuf.at[slot], xbuf.at[slot], xsem.at[slot]).wait()
        gather(jnp.minimum(i + 1, last_block), 1 - slot)
        x = jnp.concatenate([xbuf[slot, _token_chunk(j, blk, nchunk), :].astype(bf16) for j in range(nchunk)],
                            axis=-1)
        hgu = jnp.dot(x, wgu_b[...], preferred_element_type=f32) + bgu_ref[0, 0]
        hg = jnp.minimum(hgu[:, :dff], SWIGLU_LIMIT)
        hu = jnp.clip(hgu[:, dff:], -SWIGLU_LIMIT, SWIGLU_LIMIT)
        act = (hu + 1.0) * hg * jax.nn.sigmoid(SWIGLU_ALPHA * hg)
        _to_token_tiles(y_ref, _dot(act, wd_b[...]) + bd_ref[0, 0])

    @pl.when(i >= nvalid)
    def _():
        y_ref[...] = jnp.zeros_like(y_ref)

    @pl.when(i == nvalid)
    def _():
        pltpu.make_async_copy(xbuf.at[slot], xbuf.at[slot], xsem.at[slot]).wait()


def _experts(ht, row_tok, block_e, nvalid, next_e, w_gu, b_gu, w_down, b_down, l, blk):
    depth, n_exp, d, dff2 = w_gu.shape
    dff = dff2 // 2
    nchunk = d // LANES
    n_rows = row_tok.shape[0]
    bspec = lambda cols: pl.BlockSpec((1, 1, 1, cols), lambda i, be, *_: (l, be[i], 0, 0))
    grid_spec = pltpu.PrefetchScalarGridSpec(
        num_scalar_prefetch=4,
        grid=(n_rows // blk,),
        in_specs=[pl.BlockSpec(memory_space=pl.ANY), pl.BlockSpec(memory_space=pl.ANY), bspec(dff2),
                  pl.BlockSpec(memory_space=pl.ANY), bspec(d)],
        out_specs=pl.BlockSpec((blk * nchunk, LANES), lambda i, *_: (i, 0)),
        scratch_shapes=[pltpu.VMEM((2, blk * nchunk, LANES), f32), pltpu.SemaphoreType.DMA((2,)),
                        pltpu.VMEM((d, dff2), f32), pltpu.VMEM((dff, d), f32), pltpu.SemaphoreType.DMA((2,)),
                        pltpu.VMEM((d, dff2), bf16), pltpu.VMEM((dff, d), bf16)],
    )
    return pl.pallas_call(
        functools.partial(_expert_kernel, layer=l, blk=blk),
        grid_spec=grid_spec,
        out_shape=jax.ShapeDtypeStruct((n_rows * nchunk, LANES), f32),
        compiler_params=_params("arbitrary"),
        name="moe_experts",
    )(block_e, nvalid, next_e, row_tok, ht, w_gu, b_gu.reshape(depth, n_exp, 1, dff2), w_down,
      b_down.reshape(depth, n_exp, 1, d))


def _combine_kernel(dest_ref, gate_ref, xa_ref, xb_ref, yb_hbm, g_ref, oa_ref, ob_ref, ybuf, acc_s, sem,
                    *, tm, n, na_tiles, final):
    i = pl.program_id(0)
    slot = i % 2
    nchunk = ybuf.shape[2] // tm
    last_tile = pl.num_programs(0) - 1

    def copy_row(tile, into, kk, t, prio):
        row = dest_ref[kk * n + tile * tm + t]
        pltpu.make_async_copy(yb_hbm.at[_token_rows(row, nchunk)], ybuf.at[into, kk, _token_rows(t, nchunk)],
                              sem.at[into]).start(priority=prio)

    @pl.when(i == 0)
    def _():
        for kk in range(TOP_K):
            def issue(pair, c, kk=kk):
                for prio in range(DMA_PRIORITIES):
                    copy_row(0, 0, kk, pair * DMA_PRIORITIES + prio, prio)
                return c
            lax.fori_loop(0, tm // DMA_PRIORITIES, issue, 0, unroll=4)

    pltpu.make_async_copy(ybuf.at[slot], ybuf.at[slot], sem.at[slot]).wait()
    nxt = jnp.minimum(i + 1, last_tile)
    for kk in range(TOP_K):
        for t in range(tm):
            copy_row(nxt, 1 - slot, kk, t, t % DMA_PRIORITIES)
    for t in range(tm):
        tok = i * tm + t
        acc = gate_ref[tok] * ybuf[slot, 0, t * nchunk:(t + 1) * nchunk, :]
        for kk in range(1, TOP_K):
            acc = acc + gate_ref[kk * n + tok] * ybuf[slot, kk, t * nchunk:(t + 1) * nchunk, :]
        acc_s[t * nchunk:(t + 1) * nchunk, :] = acc
    moe = jnp.concatenate([acc_s[_token_chunk(j, tm, nchunk), :] for j in range(nchunk)], axis=-1)
    x3 = jnp.where(i < na_tiles, xa_ref[...], xb_ref[...]) + moe
    out = _rms(x3, g_ref[...]) if final else x3

    @pl.when(i < na_tiles)
    def _():
        oa_ref[...] = out

    @pl.when(i >= na_tiles)
    def _():
        ob_ref[...] = out

    @pl.when(i == last_tile)
    def _():
        pltpu.make_async_copy(ybuf.at[1 - slot], ybuf.at[1 - slot], sem.at[1 - slot]).wait()


def _combine(dest_km, xa, xb, gate, yb, g_final, final):
    d = xa.shape[1]
    n = xa.shape[0] + xb.shape[0]
    tm = COMBINE_TILE
    assert xa.shape[0] % tm == 0 and xb.shape[0] % tm == 0
    na_tiles = xa.shape[0] // tm
    sa, sb = _two_group_specs(tm, d, na_tiles)
    nchunk = d // LANES
    grid_spec = pltpu.PrefetchScalarGridSpec(
        num_scalar_prefetch=2,
        grid=(n // tm,),
        in_specs=[sa, sb, pl.BlockSpec(memory_space=pl.ANY), pl.BlockSpec((1, d), lambda i, *_: (0, 0))],
        out_specs=[sa, sb],
        scratch_shapes=[pltpu.VMEM((2, TOP_K, tm * nchunk, LANES), f32), pltpu.VMEM((tm * nchunk, LANES), f32),
                        pltpu.SemaphoreType.DMA((2,))],
    )
    return pl.pallas_call(
        functools.partial(_combine_kernel, tm=tm, n=n, na_tiles=na_tiles, final=final),
        grid_spec=grid_spec,
        out_shape=[jax.ShapeDtypeStruct(xa.shape, f32), jax.ShapeDtypeStruct(xb.shape, f32)],
        compiler_params=_params("arbitrary"),
        name="moe_combine",
    )(dest_km, gate.reshape(-1), xa, xb, yb, g_final.reshape(1, d))


def _moe(xa, xb, g_ffn, router_w, router_b, w_gu, b_gu, w_down, b_down, l, g_out, final):
    n = xa.shape[0] + xb.shape[0]
    n_exp = router_w.shape[1]
    blk = MOE_ROWS
    ht, top_e, top_rank, gate, counts = _router(xa, xb, g_ffn, router_w, router_b)

    counts = counts.reshape(n_exp).astype(jnp.int32)
    padded = (counts + blk - 1) // blk * blk
    pad_end = jnp.cumsum(padded)
    pad_start = pad_end - padded
    lookup = lambda table, idx: jnp.sum(jnp.where(idx[..., None] == jnp.arange(n_exp), table, 0), axis=-1)
    dest_km = (lookup(pad_start, top_e) + top_rank).reshape(-1)
    n_rows = -(-(n * TOP_K) // blk) * blk + n_exp * blk
    block_start = jnp.arange(n_rows // blk, dtype=jnp.int32) * blk
    block_e = jnp.minimum(jnp.sum(pad_end[None, :] <= block_start[:, None], axis=1), n_exp - 1).astype(jnp.int32)
    nvalid = (pad_end[-1:] // blk).astype(jnp.int32)
    zero_start = jnp.concatenate([pad_start + counts, pad_end[-1:]]).astype(jnp.int32)
    zero_end = jnp.concatenate([pad_end, jnp.full((1,), n_rows, jnp.int32)]).astype(jnp.int32)

    has_rows = padded > 0
    ids = jnp.arange(n_exp, dtype=jnp.int32)
    after = jnp.where(has_rows[None, :] & (ids[None, :] > ids[:, None]), ids[None, :], n_exp)
    next_of = jnp.min(after, axis=1)
    next_e = lookup(jnp.where(next_of < n_exp, next_of, -1), block_e).astype(jnp.int32)

    row_tok = _invert(dest_km, zero_start, zero_end, n, n_rows)
    yb = _experts(ht, row_tok, block_e, nvalid, next_e, w_gu, b_gu, w_down, b_down, l, blk)
    return _combine(dest_km, xa, xb, gate, yb, g_out, final)


def _mixers(x, t, b, h0r, h0i, prev, s0, P, l):
    s5w = P['s5_d'].shape[-1]
    u, prk = _inproj(x, P['norm_mix_g'][l], P['w_in'][l].astype(bf16), s5w)
    ar, ai, bre, bim, cre_m, cim_m = _s5_discretize(P['s5_lambda_re'][l], P['s5_lambda_im'][l], P['s5_log_dt'][l],
                                                    P['s5_b_re'][l], P['s5_b_im'][l], P['s5_c_re'][l], P['s5_c_im'][l])
    s5y, hr, hi = _s5(u.reshape(b, t, s5w), h0r, h0i, ar, ai, bre, bim, cre_m, cim_m,
                      P['s5_d'][l].reshape(1, s5w).astype(f32), P['s5_glu_w'][l].astype(bf16),
                      P['s5_glu_b'][l].reshape(1, s5w).astype(f32))
    s5y = s5y.reshape(b * t, s5w)
    rwy, s_fin = _rwkv(prk, prev, s0, t, b, P['rwkv_mu'][l], P['rwkv_w0'][l], P['rwkv_w1'][l], P['rwkv_w2'][l],
                       P['rwkv_a0'][l], P['rwkv_a1'][l], P['rwkv_a2'][l], P['rwkv_g1'][l], P['rwkv_g2'][l],
                       P['rwkv_k_k'][l], P['rwkv_k_a'][l], P['rwkv_r_k'][l], P['rwkv_ln_w'][l], P['rwkv_ln_b'][l])
    x1, q = _outproj(x, s5y, rwy, P['w_out'][l].astype(bf16), P['norm_ca_g'][l], P['w_cq'][l].astype(bf16))
    shift_last = prk.reshape(b, t, -1)[:, t - 1]
    return x1, q, hr, hi, shift_last, s_fin


def kernel(x_prompt, x_sample, mem_prompt, state_s5_re, state_s5_im, state_rwkv_shift, state_rwkv_wkv, cache_mem_k, cache_mem_v, norm_mix_g, w_in, s5_lambda_re, s5_lambda_im, s5_log_dt, s5_b_re, s5_b_im, s5_c_re, s5_c_im, s5_d, s5_glu_w, s5_glu_b, rwkv_mu, rwkv_w0, rwkv_w1, rwkv_w2, rwkv_a0, rwkv_a1, rwkv_a2, rwkv_g1, rwkv_g2, rwkv_k_k, rwkv_k_a, rwkv_r_k, rwkv_ln_w, rwkv_ln_b, w_out, norm_ca_g, norm_mem_g, w_cq, w_mem_kv, w_co, norm_ffn_g, router_w, router_b, ex_w_gu, ex_b_gu, ex_w_down, ex_b_down, norm_final_g):
    P = dict(norm_mix_g=norm_mix_g, w_in=w_in, s5_lambda_re=s5_lambda_re, s5_lambda_im=s5_lambda_im,
             s5_log_dt=s5_log_dt, s5_b_re=s5_b_re, s5_b_im=s5_b_im, s5_c_re=s5_c_re, s5_c_im=s5_c_im,
             s5_d=s5_d, s5_glu_w=s5_glu_w, s5_glu_b=s5_glu_b, rwkv_mu=rwkv_mu, rwkv_w0=rwkv_w0,
             rwkv_w1=rwkv_w1, rwkv_w2=rwkv_w2, rwkv_a0=rwkv_a0, rwkv_a1=rwkv_a1, rwkv_a2=rwkv_a2,
             rwkv_g1=rwkv_g1, rwkv_g2=rwkv_g2, rwkv_k_k=rwkv_k_k, rwkv_k_a=rwkv_k_a, rwkv_r_k=rwkv_r_k,
             rwkv_ln_w=rwkv_ln_w, rwkv_ln_b=rwkv_ln_b, w_out=w_out, norm_ca_g=norm_ca_g, w_cq=w_cq)
    depth = norm_mix_g.shape[0]
    bp, tp, d = x_prompt.shape
    bs, ts, _ = x_sample.shape
    n_mem = mem_prompt.shape[1]
    s5_groups, s5_state = s5_lambda_re.shape[1:]
    nstate = s5_groups * s5_state
    rw_in = state_rwkv_shift.shape[-1]
    heads = state_rwkv_wkv.shape[2]
    mh = cache_mem_k.shape[3]
    assert mh == MEM_HEADS

    xp = x_prompt.reshape(bp * tp, d)
    xs = x_sample.reshape(bs * ts, d)
    outs = {k: [] for k in ('p_re', 'p_im', 'p_sh', 'p_wkv', 'p_mk', 'p_mv', 's_re', 's_im', 's_sh', 's_wkv')}
    for l in range(depth):
        mk, mv = _memkv(mem_prompt.reshape(bp * n_mem, d), norm_mem_g[l], w_mem_kv[l].astype(bf16))
        zeros_state = jnp.zeros((bp, nstate), f32)
        x1p, qp, re_p, im_p, sh_p, wkv_p = _mixers(
            xp, tp, bp, zeros_state, zeros_state, jnp.zeros((bp, rw_in), f32),
            jnp.zeros((bp, heads, RWKV_HEAD, RWKV_HEAD), f32), P, l)
        x1s, qs, re_s, im_s, sh_s, wkv_s = _mixers(
            xs, ts, bs, state_s5_re[l].reshape(bs, nstate), state_s5_im[l].reshape(bs, nstate),
            state_rwkv_shift[l], state_rwkv_wkv[l], P, l)
        wco = w_co[l].astype(bf16)
        x2p = _attn(qp, x1p, mk, mv, wco, tp, bp)
        x2s = _attn(qs, x1s, _kv_to_head_tiles(cache_mem_k[l]), _kv_to_head_tiles(cache_mem_v[l]), wco, ts, bs)
        last = l == depth - 1
        xp, xs = _moe(x2p, x2s, norm_ffn_g[l], router_w[l], router_b[l], ex_w_gu, ex_b_gu, ex_w_down, ex_b_down,
                      l, norm_final_g, last)
        outs['p_re'].append(re_p.reshape(bp, s5_groups, s5_state))
        outs['p_im'].append(im_p.reshape(bp, s5_groups, s5_state))
        outs['p_sh'].append(sh_p)
        outs['p_wkv'].append(wkv_p)
        outs['p_mk'].append(_kv_from_head_tiles(mk, bp, n_mem, mh, d // mh))
        outs['p_mv'].append(_kv_from_head_tiles(mv, bp, n_mem, mh, d // mh))
        outs['s_re'].append(re_s.reshape(bs, s5_groups, s5_state))
        outs['s_im'].append(im_s.reshape(bs, s5_groups, s5_state))
        outs['s_sh'].append(sh_s)
        outs['s_wkv'].append(wkv_s)
    st = lambda k: jnp.stack(outs[k])
    return (xp.reshape(bp, tp, d), xs.reshape(bs, ts, d), st('p_re'), st('p_im'), st('p_sh'), st('p_wkv'),
            st('p_mk'), st('p_mv'), st('s_re'), st('s_im'), st('s_sh'), st('s_wkv'))
```

```python
import functools

import jax
import jax.numpy as jnp
from jax import lax
from jax.experimental import pallas as pl
from jax.experimental.pallas import tpu as pltpu

f32 = jnp.float32
bf16 = jnp.bfloat16

RWKV_HEAD = 64
HEAD_PAIR = 2 * RWKV_HEAD
DECAY_SCALE = 0.6065306597
MEM_HEADS = 4
TOP_K = 4
SWIGLU_LIMIT = 7.0
SWIGLU_ALPHA = 1.702
RMS_EPS = 1e-5
GN_EPS = 64e-5

V7X_VMEM_LIMIT = 56 * 1024 * 1024
ROW_TILE = 512
MOE_ROWS = 256
COMBINE_TILE = 256
WKV_CHUNK = 64
WKV_BATCH = 8
S5_CHUNK = 64
S5_BLOCK_CHANNELS = 128
LANES = 128
DMA_PRIORITIES = 2
ATTN_SHORT_BATCH = 4

_NT = (((1,), (1,)), ((), ()))
_TN = (((0,), (0,)), ((), ()))


def _params(*sem):
    return pltpu.CompilerParams(dimension_semantics=sem, vmem_limit_bytes=V7X_VMEM_LIMIT)


def _dot(a, b):
    return jnp.dot(a.astype(bf16), b.astype(bf16), preferred_element_type=f32)


def _dg(a, b, dims):
    return lax.dot_general(a.astype(bf16), b.astype(bf16), dims, preferred_element_type=f32)


def _rms(x, g):
    return x * lax.rsqrt(jnp.mean(x * x, axis=-1, keepdims=True) + RMS_EPS) * g


def _full(shape):
    n = len(shape)
    return pl.BlockSpec(shape, lambda *_: (0,) * n)


def _tile(rows, tile=ROW_TILE):
    t = min(tile, rows)
    assert rows % t == 0
    return t


def _inproj_kernel(x_ref, g_ref, w_ref, u_ref, p_ref, *, s5w):
    nb = _rms(x_ref[...], g_ref[...]).astype(bf16)
    u_ref[...] = jnp.dot(nb, w_ref[:, :s5w], preferred_element_type=f32)
    p_ref[...] = jnp.dot(nb, w_ref[:, s5w:], preferred_element_type=f32)


def _inproj(x, g, w, s5w):
    rows, d = x.shape
    inw = w.shape[1]
    tm = _tile(rows)
    return pl.pallas_call(
        functools.partial(_inproj_kernel, s5w=s5w),
        grid=(rows // tm,),
        in_specs=[pl.BlockSpec((tm, d), lambda i: (i, 0)), _full((1, d)), _full((d, inw))],
        out_specs=[pl.BlockSpec((tm, s5w), lambda i: (i, 0)), pl.BlockSpec((tm, inw - s5w), lambda i: (i, 0))],
        out_shape=[jax.ShapeDtypeStruct((rows, s5w), f32), jax.ShapeDtypeStruct((rows, inw - s5w), f32)],
        compiler_params=_params("parallel"),
        name="inproj",
    )(x, g.reshape(1, d), w)


def _s5_kernel(u_hbm, h0r_ref, h0i_ref, ar_ref, ai_ref, bre_ref, bim_ref, cre_ref, cim_ref, d_ref, gw_ref, gb_ref,
               y_hbm, hr_ref, hi_ref, xr_s, xi_s, ubuf, usem, ybuf, ysem, *, steps, bt, lane_chunk):
    batch_tile = pl.program_id(0)
    i = pl.program_id(1)
    nchunks = pl.num_programs(1)
    step = batch_tile * nchunks + i
    last_step = pl.num_programs(0) * nchunks - 1
    slot = step % 2
    width = ubuf.shape[-1]
    nstate = ar_ref.shape[-1]

    def chunk_copies(tile, chunk, s, load):
        t0 = pl.multiple_of(chunk * steps, steps)
        for b in range(bt):
            hbm = (u_hbm if load else y_hbm).at[tile * bt + b, pl.ds(t0, steps), :]
            if load:
                yield pltpu.make_async_copy(hbm, ubuf.at[s, :, b, :], usem.at[s])
            else:
                yield pltpu.make_async_copy(ybuf.at[s, :, b, :], hbm, ysem.at[s])

    @pl.when(i == 0)
    def _():
        hr_ref[...] = h0r_ref[...]
        hi_ref[...] = h0i_ref[...]

    @pl.when(step == 0)
    def _():
        for cp in chunk_copies(0, 0, 0, True):
            cp.start()

    @pl.when(step < last_step)
    def _():
        wraps = i + 1 == nchunks
        nxt_tile = jnp.where(wraps, batch_tile + 1, batch_tile)
        nxt_chunk = jnp.where(wraps, 0, i + 1)
        for cp in chunk_copies(nxt_tile, nxt_chunk, 1 - slot, True):
            cp.start()

    @pl.when(step >= 2)
    def _():
        for cp in chunk_copies(batch_tile, i, slot, False):
            cp.wait()

    for cp in chunk_copies(batch_tile, i, slot, True):
        cp.wait()
    u = ubuf[slot].reshape(steps * bt, width)
    ub = u.astype(bf16)
    nblk, cb, sb = bre_ref.shape
    for j in range(nblk):
        uj = ub[:, j * cb:(j + 1) * cb]
        xr_s[:, j * sb:(j + 1) * sb] = jnp.dot(uj, bre_ref[j], preferred_element_type=f32)
        xi_s[:, j * sb:(j + 1) * sb] = jnp.dot(uj, bim_ref[j], preferred_element_type=f32)

    for c in range(nstate // lane_chunk):
        sl = slice(c * lane_chunk, (c + 1) * lane_chunk)
        arb = jnp.broadcast_to(ar_ref[:, sl], (bt, lane_chunk))
        aib = jnp.broadcast_to(ai_ref[:, sl], (bt, lane_chunk))

        hr, hi = hr_ref[:, sl], hi_ref[:, sl]
        for t in range(steps):
            rows = slice(t * bt, (t + 1) * bt)
            hr, hi = (arb * hr - aib * hi + xr_s[rows, sl], arb * hi + aib * hr + xi_s[rows, sl])
            xr_s[rows, sl] = hr
            xi_s[rows, sl] = hi
        hr_ref[:, sl] = hr
        hi_ref[:, sl] = hi

    y = jnp.concatenate(
        [jnp.dot(xr_s[:, j * sb:(j + 1) * sb].astype(bf16), cre_ref[j], preferred_element_type=f32)
         + jnp.dot(xi_s[:, j * sb:(j + 1) * sb].astype(bf16), cim_ref[j], preferred_element_type=f32)
         for j in range(nblk)], axis=-1) + d_ref[...] * u
    z = jax.nn.gelu(y)
    out = z * jax.nn.sigmoid(_dot(z, gw_ref[...]) + gb_ref[...])
    ybuf[slot] = out.reshape(steps, bt, width)
    for cp in chunk_copies(batch_tile, i, slot, False):
        cp.start()

    @pl.when(step == last_step)
    def _():
        for cp in chunk_copies(batch_tile, i, slot, False):
            cp.wait()

        @pl.when(step >= 1)
        def _():
            for cp in chunk_copies(batch_tile, i, 1 - slot, False):
                cp.wait()


def _s5(u3, h0r, h0i, ar, ai, bre, bim, cre_m, cim_m, d, gw, gb):
    b, t, width = u3.shape
    nstate = ar.shape[-1]
    bt = 8
    steps = min(S5_CHUNK, t)
    assert b % bt == 0 and t % steps == 0
    kern = functools.partial(_s5_kernel, steps=steps, bt=bt, lane_chunk=512)
    st_spec = pl.BlockSpec((bt, nstate), lambda j, i: (j, 0))
    y, hr, hi = pl.pallas_call(
        kern,
        grid=(b // bt, t // steps),
        in_specs=[pl.BlockSpec(memory_space=pl.ANY), st_spec, st_spec,
                  _full((1, nstate)), _full((1, nstate)), _full(bre.shape), _full(bim.shape),
                  _full(cre_m.shape), _full(cim_m.shape), _full((1, width)), _full((width, width)),
                  _full((1, width))],
        out_specs=[pl.BlockSpec(memory_space=pl.ANY), st_spec, st_spec],
        out_shape=[jax.ShapeDtypeStruct((b, t, width), f32), jax.ShapeDtypeStruct((b, nstate), f32),
                   jax.ShapeDtypeStruct((b, nstate), f32)],
        scratch_shapes=[pltpu.VMEM((steps * bt, nstate), f32), pltpu.VMEM((steps * bt, nstate), f32),
                        pltpu.VMEM((2, steps, bt, width), f32), pltpu.SemaphoreType.DMA((2,)),
                        pltpu.VMEM((2, steps, bt, width), f32), pltpu.SemaphoreType.DMA((2,))],
        compiler_params=_params("arbitrary", "arbitrary"),
        name="s5_mixer",
    )(u3, h0r, h0i, ar, ai, bre, bim, cre_m, cim_m, d, gw, gb)
    return y, hr, hi


def _s5_discretize(lam_re, lam_im, log_dt, b_re, b_im, c_re, c_im):
    g, p = lam_re.shape
    dt = jnp.exp(log_dt.astype(f32))[:, None]
    lr = lam_re.astype(f32)
    li = lam_im.astype(f32)
    mag = jnp.exp(lr * dt)
    ar = mag * jnp.cos(li * dt)
    ai = mag * jnp.sin(li * dt)
    den = lr * lr + li * li
    cr = ((ar - 1.0) * lr + ai * li) / den
    ci = (ai * lr - (ar - 1.0) * li) / den
    br = b_re.astype(f32)
    bi = b_im.astype(f32)
    bbar_r = cr[..., None] * br - ci[..., None] * bi
    bbar_i = cr[..., None] * bi + ci[..., None] * br
    c = br.shape[-1]
    gb = S5_BLOCK_CHANNELS // c
    assert g % gb == 0
    eye = jnp.eye(gb, dtype=f32)
    nblk = g // gb
    bd_in = lambda m: jnp.einsum('ngpc,gh->ngchp', m.reshape(nblk, gb, p, c), eye).reshape(
        nblk, gb * c, gb * p).astype(bf16)
    bd_out = lambda m: jnp.einsum('ngcp,gh->ngphc', m.reshape(nblk, gb, c, p), eye).reshape(
        nblk, gb * p, gb * c).astype(bf16)
    return (ar.reshape(1, g * p), ai.reshape(1, g * p), bd_in(bbar_r), bd_in(bbar_i),
            bd_out(c_re.astype(f32)), bd_out(-c_im.astype(f32)))


def _rwkv_kernel(p_ref, prev_ref, s0_ref, mu_ref, w0_ref, w1_ref, w2_ref, a0_ref, a1_ref, a2_ref, g1_ref, g2_ref,
                 kk_ref, ka_ref, rk_ref, lnw_ref, lnb_ref, y_ref, sout_ref, prev_s, st_s, *, steps, nsq):
    i = pl.program_id(1)
    L = steps
    L2 = 2 * L
    nb = p_ref.shape[0]
    w = p_ref.shape[-1] // 4
    npair = w // HEAD_PAIR

    row2 = lax.broadcasted_iota(jnp.int32, (L2, HEAD_PAIR), 0)
    lane2 = lax.broadcasted_iota(jnp.int32, (L2, HEAD_PAIR), 1)
    own = (row2 < L) == (lane2 < RWKV_HEAD)
    ri = lax.broadcasted_iota(jnp.int32, (L2, L2), 0)
    ci = lax.broadcasted_iota(jnp.int32, (L2, L2), 1)
    same = (ri < L) == (ci < L)
    strict = same & (ci < ri)
    incl = same & (ci <= ri)
    eye = (ri == ci).astype(f32)
    sr = lax.broadcasted_iota(jnp.int32, (HEAD_PAIR, HEAD_PAIR), 0)
    sc = lax.broadcasted_iota(jnp.int32, (HEAD_PAIR, HEAD_PAIR), 1)
    sdiag = (sr < RWKV_HEAD) == (sc < RWKV_HEAD)

    @pl.when(i == 0)
    def _():
        prev_s[...] = prev_ref[...]
        for bb in range(nb):
            for pr in range(npair):
                sp = s0_ref[bb, pr * HEAD_PAIR:(pr + 1) * HEAD_PAIR, :]
                st_s[bb * npair + pr] = jnp.where(sdiag, jnp.concatenate([sp, sp], axis=1), 0.0)

    trow = lax.broadcasted_iota(jnp.int32, (L, 1), 0)
    ps, deltas = [], []
    for bb in range(nb):
        pb = p_ref[bb]
        shifted = jnp.where(trow == 0, prev_s[bb], pltpu.roll(pb, 1, axis=0))
        prev_s[bb] = pb[L - 1:L, :]
        ps.append(pb)
        deltas.append(shifted - pb)
    p = jnp.concatenate(ps, axis=0) if nb > 1 else ps[0]
    delta = jnp.concatenate(deltas, axis=0) if nb > 1 else deltas[0]
    mu = mu_ref[...]
    pz = p[:, 3 * w:]
    dz = delta[:, 3 * w:]
    r = p[:, :w] + delta[:, :w] * mu[0:1]
    k = p[:, w:2 * w] + delta[:, w:2 * w] * mu[1:2]
    v = p[:, 2 * w:3 * w] + delta[:, 2 * w:3 * w] * mu[2:3]
    zw = pz + dz * mu[3:4]
    za = pz + dz * mu[4:5]
    zg = pz + dz * mu[5:6]
    logw = -DECAY_SCALE * jax.nn.sigmoid(w0_ref[...] + _dot(jnp.tanh(_dot(zw, w1_ref[...])), w2_ref[...]))
    a = jax.nn.sigmoid(a0_ref[...] + _dot(_dot(za, a1_ref[...]), a2_ref[...]))
    g = _dot(jax.nn.sigmoid(_dot(zg, g1_ref[...])), g2_ref[...])
    kkraw = k * kk_ref[...]
    kmod = k * (1.0 + (a - 1.0) * ka_ref[...])
    rk = r * kmod * rk_ref[...]

    tr = lax.broadcasted_iota(jnp.int32, (nb * L, nb * L), 0)
    tc = lax.broadcasted_iota(jnp.int32, (nb * L, nb * L), 1)
    tri = ((tc <= tr) & (tc >= (tr // L) * L)).astype(bf16)
    hi = logw.astype(bf16)
    rem = logw - hi.astype(f32)
    mid = rem.astype(bf16)
    lo = (rem - mid.astype(f32)).astype(bf16)
    cum = (jnp.dot(tri, hi, preferred_element_type=f32) + jnp.dot(tri, mid, preferred_element_type=f32)
           + jnp.dot(tri, lo, preferred_element_type=f32))

    stack = lambda x: jnp.concatenate([x, x], axis=0)
    chains = [(bb, pr) for bb in range(nb) for pr in range(npair)]
    nch = range(len(chains))
    rs = [slice(bb * L, (bb + 1) * L) for bb, _ in chains]
    sls = [slice(pr * HEAD_PAIR, (pr + 1) * HEAD_PAIR) for _, pr in chains]
    cut = lambda x, c: stack(x[rs[c], sls[c]])
    own_of = lambda x, c: jnp.where(own, cut(x, c), 0.0)

    wt, rm, km, bm, kap, vmf, vm, gram = [], [], [], [], [], [], [], []
    for c in nch:
        cum_c = cut(cum, c)
        wt_c = jnp.exp(cum_c)
        winv = jnp.exp(-cum_c)
        wprev = jnp.exp(cum_c - cut(logw, c))
        kkr = own_of(kkraw, c)
        kkn = kkr * lax.rsqrt(jnp.maximum(jnp.sum(kkr * kkr, axis=-1, keepdims=True), 1e-24))
        wt.append(wt_c)
        rm.append((own_of(r, c) * wt_c).astype(bf16))
        km.append((own_of(kmod, c) * winv).astype(bf16))
        bm.append((kkn * cut(a, c) * winv).astype(bf16))
        kap.append((kkn * wprev).astype(bf16))
        vmf.append(own_of(v, c))
        vm.append(vmf[c].astype(bf16))
        gram.append(_dg(jnp.concatenate([kap[c], rm[c]], axis=0), jnp.concatenate([bm[c], km[c]], axis=0), _NT))
    n_mat = [jnp.where(strict, x[:L2, :L2], 0.0) for x in gram]
    p_mat = [jnp.where(strict, x[:L2, L2:], 0.0) for x in gram]
    g_mat = [jnp.where(incl, x[L2:, :L2], 0.0) for x in gram]
    q_mat = [jnp.where(incl, x[L2:, L2:], 0.0) for x in gram]

    neg = [-x for x in n_mat]
    inv = [eye + x for x in neg]
    if nsq > 0:
        pw = [_dot(x, x) for x in neg]
        for _ in range(nsq - 1):
            both = [_dot(pw[c], jnp.concatenate([inv[c], pw[c]], axis=1)) for c in nch]
            inv = [inv[c] + both[c][:, :L2] for c in nch]
            pw = [x[:, L2:] for x in both]
        inv = [inv[c] + _dot(pw[c], inv[c]) for c in nch]

    s_old = [st_s[c] for c in nch]
    rhs = [_dg(kap[c], s_old[c], _NT) + _dot(p_mat[c], vm[c]) for c in nch]
    ym0 = [_dg(rm[c], s_old[c], _NT) for c in nch]
    um = [(-_dot(inv[c], rhs[c])).astype(bf16) for c in nch]
    vu = [jnp.concatenate([vm[c], um[c]], axis=0) for c in nch]
    ym = [ym0[c] + _dot(jnp.concatenate([q_mat[c], g_mat[c]], axis=1), vu[c]) for c in nch]
    for c in nch:
        st_s[c] = (s_old[c] + _dg(vu[c], jnp.concatenate([km[c], bm[c]], axis=0), _TN)) * wt[c][L - 1:L, :]

    for c, (bb, pr) in enumerate(chains):
        mean = jnp.sum(ym[c], axis=-1, keepdims=True) * (1.0 / RWKV_HEAD)
        yc = jnp.where(own, ym[c] - mean, 0.0)
        var = jnp.sum(yc * yc, axis=-1, keepdims=True) * (1.0 / RWKV_HEAD)
        yn = yc * lax.rsqrt(var + GN_EPS)
        bonus = jnp.sum(own_of(rk, c), axis=-1, keepdims=True) * vmf[c]
        mix = (yn[:L] + yn[L:]) * lnw_ref[:, sls[c]] + lnb_ref[:, sls[c]] + (bonus[:L] + bonus[L:])
        y_ref[bb, :, sls[c]] = mix * g[rs[c], sls[c]]

    @pl.when(i == pl.num_programs(1) - 1)
    def _():
        for c, (bb, pr) in enumerate(chains):
            s_new = st_s[c]
            sout_ref[bb, pr * HEAD_PAIR:(pr + 1) * HEAD_PAIR, :] = s_new[:, :RWKV_HEAD] + s_new[:, RWKV_HEAD:]


def _rwkv(prk, prev, s0, t, b, mu, w0, w1, w2, a0, a1, a2, g1, g2, k_k, k_a, r_k, ln_w, ln_b):
    w = prk.shape[-1] // 4
    steps = min(WKV_CHUNK, t)
    nb = min(WKV_BATCH * max(1, WKV_CHUNK // (4 * steps)), b)
    assert t % steps == 0 and w % HEAD_PAIR == 0 and b % nb == 0
    nsq = max((steps - 1).bit_length() - 1, 0)
    row = lambda x: x.reshape(1, -1).astype(f32)
    wb = lambda x: x.astype(bf16)
    weights = [mu.astype(f32), row(w0), wb(w1), wb(w2), row(a0), wb(a1), wb(a2), wb(g1), wb(g2),
               row(k_k), row(k_a), row(r_k), row(ln_w), row(ln_b)]
    state_spec = pl.BlockSpec((nb, w, RWKV_HEAD), lambda j, i: (j, 0, 0))
    y, s_fin = pl.pallas_call(
        functools.partial(_rwkv_kernel, steps=steps, nsq=nsq),
        grid=(b // nb, t // steps),
        in_specs=[pl.BlockSpec((nb, steps, 4 * w), lambda j, i: (j, i, 0)),
                  pl.BlockSpec((nb, 1, 4 * w), lambda j, i: (j, 0, 0)), state_spec]
                 + [_full(x.shape) for x in weights],
        out_specs=[pl.BlockSpec((nb, steps, w), lambda j, i: (j, i, 0)), state_spec],
        out_shape=[jax.ShapeDtypeStruct((b, t, w), f32), jax.ShapeDtypeStruct((b, w, RWKV_HEAD), f32)],
        scratch_shapes=[pltpu.VMEM((nb, 1, 4 * w), f32),
                        pltpu.VMEM((nb * (w // HEAD_PAIR), HEAD_PAIR, HEAD_PAIR), f32)],
        compiler_params=_params("arbitrary", "arbitrary"),
        name="rwkv7_mixer",
    )(prk.reshape(b, t, 4 * w), prev.reshape(b, 1, 4 * w).astype(f32), s0.reshape(b, w, RWKV_HEAD).astype(f32),
      *weights)
    return y.reshape(b * t, w), s_fin.reshape(b, w // RWKV_HEAD, RWKV_HEAD, RWKV_HEAD)


def _outproj_kernel(x_ref, s5_ref, rw_ref, wo_ref, g_ref, wq_ref, x1_ref, q_ref, *, s5w):
    x1 = x_ref[...] + _dot(s5_ref[...], wo_ref[:s5w, :]) + _dot(rw_ref[...], wo_ref[s5w:, :])
    x1_ref[...] = x1
    q_ref[...] = _dot(_rms(x1, g_ref[...]), wq_ref[...]).astype(bf16)


def _outproj(x, s5y, rwy, w_out, g_ca, w_cq):
    rows, d = x.shape
    s5w = s5y.shape[1]
    rww = rwy.shape[1]
    tm = _tile(rows)
    rt = lambda c: pl.BlockSpec((tm, c), lambda i: (i, 0))
    return pl.pallas_call(
        functools.partial(_outproj_kernel, s5w=s5w),
        grid=(rows // tm,),
        in_specs=[rt(d), rt(s5w), rt(rww), _full(w_out.shape), _full((1, d)), _full(w_cq.shape)],
        out_specs=[rt(d), rt(d)],
        out_shape=[jax.ShapeDtypeStruct((rows, d), f32), jax.ShapeDtypeStruct((rows, d), bf16)],
        compiler_params=_params("parallel"),
        name="outproj_q",
    )(x, s5y, rwy, w_out, g_ca.reshape(1, d), w_cq)


def _kv_to_head_tiles(x):
    b, m, h, hd = x.shape
    return x.reshape(b, m, h, hd // LANES, LANES).transpose(0, 1, 3, 2, 4).reshape(b * m * (hd // LANES) * h, LANES)


def _kv_from_head_tiles(x, b, m, h, hd):
    return x.reshape(b, m, hd // LANES, h, LANES).transpose(0, 1, 3, 2, 4).reshape(b, m, h, hd)


def _memkv_kernel(m_ref, g_ref, w_ref, k_ref, v_ref):
    tm, d = m_ref.shape
    hd = d // MEM_HEADS
    parts = hd // LANES
    per_token = parts * MEM_HEADS
    nb = _rms(m_ref[...], g_ref[...]).astype(bf16)
    for ref, cols in ((k_ref, slice(0, d)), (v_ref, slice(d, 2 * d))):
        kv = jnp.dot(nb, w_ref[:, cols], preferred_element_type=f32)
        for h in range(MEM_HEADS):
            for c in range(parts):
                lo = h * hd + c * LANES
                ref[pl.ds(c * MEM_HEADS + h, tm, stride=per_token), :] = kv[:, lo:lo + LANES]


def _memkv(mem, g, w):
    rows, d = mem.shape
    tm = _tile(rows)
    per_token = d // LANES
    out = pl.BlockSpec((tm * per_token, LANES), lambda i: (i, 0))
    return pl.pallas_call(
        _memkv_kernel,
        grid=(rows // tm,),
        in_specs=[pl.BlockSpec((tm, d), lambda i: (i, 0)), _full((1, d)), _full(w.shape)],
        out_specs=[out, out],
        out_shape=[jax.ShapeDtypeStruct((rows * per_token, LANES), f32)] * 2,
        compiler_params=_params("parallel"),
        name="memory_kv",
    )(mem, g.reshape(1, d), w)


def _attn_kernel(q_ref, x1_ref, k_ref, v_ref, wo_ref, x2_ref, *, nbb):
    d = q_ref.shape[-1]
    hd = d // MEM_HEADS
    parts = hd // LANES
    per_token = parts * MEM_HEADS
    n_mem = k_ref.shape[0] // (per_token * nbb)
    rows = q_ref.shape[0] // nbb
    scale = hd ** -0.5
    q = q_ref[...]

    def head(ref, bb, h):
        first = bb * n_mem * per_token
        return jnp.concatenate([ref[pl.ds(first + c * MEM_HEADS + h, n_mem, stride=per_token), :].astype(bf16)
                                for c in range(parts)], axis=-1)

    heads = []
    for h in range(MEM_HEADS):
        s = [_dg(q[bb * rows:(bb + 1) * rows, h * hd:(h + 1) * hd], head(k_ref, bb, h), _NT) * scale
             for bb in range(nbb)]
        e = [jnp.exp(x - jnp.max(x, axis=-1, keepdims=True)) for x in s]
        prob = [x / jnp.sum(x, axis=-1, keepdims=True) for x in e]
        o = [_dot(prob[bb], head(v_ref, bb, h)) for bb in range(nbb)]
        heads.append(jnp.concatenate(o, axis=0) if nbb > 1 else o[0])
    o = jnp.concatenate(heads, axis=-1)
    x2_ref[...] = x1_ref[...] + _dot(o, wo_ref[...])


def _attn(q, x1, mk, mv, w_co, t, b):
    d = q.shape[-1]
    rows_per_b = mk.shape[0] // b
    tt = min(ROW_TILE, t)
    assert t % tt == 0
    nt = t // tt
    nbb = min(ATTN_SHORT_BATCH, b) if (nt == 1 and tt * ATTN_SHORT_BATCH <= ROW_TILE) else 1
    assert b % nbb == 0
    xt = pl.BlockSpec((nbb * tt, d), lambda j, i: (j * nt + i, 0))
    kv = pl.BlockSpec((nbb * rows_per_b, LANES), lambda j, i: (j, 0))
    return pl.pallas_call(
        functools.partial(_attn_kernel, nbb=nbb),
        grid=(b // nbb, nt),
        in_specs=[xt, xt, kv, kv, _full(w_co.shape)],
        out_specs=xt,
        out_shape=jax.ShapeDtypeStruct((b * t, d), f32),
        compiler_params=_params("parallel", "parallel"),
        name="mem_attention",
    )(q, x1, mk, mv, w_co)


def _two_group_specs(tm, cols, na_tiles):
    a = pl.BlockSpec((tm, cols), lambda i, *_: (jnp.minimum(i, na_tiles - 1), 0))
    b = pl.BlockSpec((tm, cols), lambda i, *_: (jnp.maximum(i - na_tiles, 0), 0))
    return a, b


def _split2(x):
    hi = x.astype(bf16)
    return hi, (x - hi.astype(f32)).astype(bf16)


def _token_chunk(j, rows, nchunk, first=0):
    return pl.ds(first * nchunk + j, rows, stride=nchunk)


def _to_token_tiles(ref, x, first=0):
    rows, d = x.shape
    nchunk = d // LANES
    for j in range(nchunk):
        ref[_token_chunk(j, rows, nchunk, first), :] = x[:, j * LANES:(j + 1) * LANES]


def _token_rows(t, nchunk):
    return pl.ds(pl.multiple_of(t * nchunk, nchunk), nchunk)


def _router_kernel(xa_ref, xb_ref, g_ref, rwt_ref, rb_ref, ht_ref, e_ref, rank_ref, gate_ref, cnt_ref,
                   *, n_exp, na_tiles):
    i = pl.program_id(0)
    tm, d = xa_ref.shape

    @pl.when(i == 0)
    def _():
        cnt_ref[...] = jnp.zeros_like(cnt_ref)

    h = _rms(jnp.where(i < na_tiles, xa_ref[...], xb_ref[...]), g_ref[...])
    _to_token_tiles(ht_ref, h)

    hh, hl = _split2(h)
    wh, wl = _split2(rwt_ref[...])
    nt = lambda a, b: lax.dot_general(a, b, _NT, preferred_element_type=f32)
    logits = nt(wh, hh) + nt(wl, hh) + nt(wh, hl) + rb_ref[...]

    expert = lax.broadcasted_iota(jnp.int32, (n_exp, tm), 0)
    work = logits
    sels, tops = [], []
    for _ in range(TOP_K):
        m = jnp.max(work, axis=0, keepdims=True)
        idx = jnp.min(jnp.where(work == m, expert, n_exp), axis=0, keepdims=True)
        sel = expert == idx
        sels.append(sel)
        tops.append(m)
        work = jnp.where(sel, -jnp.inf, work)
    chosen = sels[0] | sels[1] | sels[2] | sels[3]

    onehot = jnp.where(chosen, 1.0, 0.0)
    rr = lax.broadcasted_iota(jnp.int32, (tm, tm), 0)
    cc = lax.broadcasted_iota(jnp.int32, (tm, tm), 1)
    earlier = (rr < cc).astype(bf16)
    rank = jnp.dot(onehot.astype(bf16), earlier, preferred_element_type=f32) + cnt_ref[...]
    cnt_ref[...] = cnt_ref[...] + jnp.sum(onehot, axis=1, keepdims=True)

    den = sum(jnp.exp(t - tops[0]) for t in tops)
    e_rows, r_rows, g_rows = [], [], []
    for sel, top in zip(sels, tops):
        e_rows.append(jnp.sum(jnp.where(sel, expert, 0), axis=0, keepdims=True))
        r_rows.append(jnp.sum(jnp.where(sel, rank, 0.0), axis=0, keepdims=True))
        g_rows.append(jnp.exp(top - tops[0]) / den)
    e_ref[...] = jnp.concatenate(e_rows, axis=0)
    rank_ref[...] = jnp.concatenate(r_rows, axis=0).astype(jnp.int32)
    gate_ref[...] = jnp.concatenate(g_rows, axis=0)


def _router(xa, xb, g, rw, rb):
    d = xa.shape[1]
    n = xa.shape[0] + xb.shape[0]
    n_exp = rw.shape[1]
    tm = ROW_TILE
    assert xa.shape[0] % tm == 0 and xb.shape[0] % tm == 0
    na_tiles = xa.shape[0] // tm
    sa, sb = _two_group_specs(tm, d, na_tiles)
    per_k = pl.BlockSpec((TOP_K, tm), lambda i: (0, i))
    return pl.pallas_call(
        functools.partial(_router_kernel, n_exp=n_exp, na_tiles=na_tiles),
        grid=(n // tm,),
        in_specs=[sa, sb, _full((1, d)), _full((n_exp, d)), _full((n_exp, 1))],
        out_specs=[pl.BlockSpec((tm * (d // LANES), LANES), lambda i: (i, 0)), per_k, per_k, per_k,
                   _full((n_exp, 1))],
        out_shape=[jax.ShapeDtypeStruct((n * (d // LANES), LANES), f32), jax.ShapeDtypeStruct((TOP_K, n), jnp.int32),
                   jax.ShapeDtypeStruct((TOP_K, n), jnp.int32), jax.ShapeDtypeStruct((TOP_K, n), f32),
                   jax.ShapeDtypeStruct((n_exp, 1), f32)],
        compiler_params=_params("arbitrary"),
        name="moe_router",
    )(xa, xb, g.reshape(1, d), rw.astype(f32).T, rb.reshape(n_exp, 1).astype(f32))


def _invert_kernel(dest_ref, zs_ref, ze_ref, tok_ref, *, n):
    def for_each_unmapped_row(fn):
        def segment(seg, c):
            return lax.fori_loop(zs_ref[seg], ze_ref[seg], fn, c)
        lax.fori_loop(0, zs_ref.shape[0], segment, 0)

    fill = lambda r, c: _set(tok_ref, r, 0, c)
    for_each_unmapped_row(fill)
    for kk in range(TOP_K):
        def place(t, c, kk=kk):
            return _set(tok_ref, dest_ref[kk * n + t], t, c)
        lax.fori_loop(0, n, place, 0, unroll=8)


def _set(ref, idx, val, carry):
    ref[idx] = val
    return carry


def _invert(dest_km, zero_start, zero_end, n, n_rows):
    grid_spec = pltpu.PrefetchScalarGridSpec(
        num_scalar_prefetch=3, grid=(1,), in_specs=[],
        out_specs=pl.BlockSpec(memory_space=pltpu.SMEM))
    return pl.pallas_call(
        functools.partial(_invert_kernel, n=n),
        grid_spec=grid_spec,
        out_shape=jax.ShapeDtypeStruct((n_rows,), jnp.int32),
        compiler_params=_params("arbitrary"),
        name="moe_invert",
    )(dest_km, zero_start, zero_end)


def _expert_kernel(be_ref, nv_ref, nxt_ref, tok_ref, ht_hbm, wgu_hbm, bgu_ref, wd_hbm, bd_ref, y_ref,
                   xbuf, xsem, wgu_f, wd_f, wsem, wgu_b, wd_b, *, layer, blk):
    i = pl.program_id(0)
    slot = i % 2
    nvalid = nv_ref[0]
    dff = wd_b.shape[0]
    nchunk = wgu_b.shape[0] // LANES
    last_block = pl.num_programs(0) - 1

    def gather(block, into):
        for r in range(blk):
            tok = tok_ref[block * blk + r]
            pltpu.make_async_copy(ht_hbm.at[_token_rows(tok, nchunk)], xbuf.at[into, _token_rows(r, nchunk)],
                                  xsem.at[into]).start(priority=0)

    def weight_copies(e):
        return (pltpu.make_async_copy(wgu_hbm.at[layer, e], wgu_f, wsem.at[0]),
                pltpu.make_async_copy(wd_hbm.at[layer, e], wd_f, wsem.at[1]))

    @pl.when(i == 0)
    def _():
        gather(0, 0)
        for cp in weight_copies(be_ref[0]):
            cp.start(priority=1)

    @pl.when((i < nvalid) & ((i == 0) | (be_ref[i] != be_ref[jnp.maximum(i - 1, 0)])))
    def _():
        for cp in weight_copies(be_ref[i]):
            cp.wait()
        wgu_b[...] = wgu_f[...].astype(bf16)
        wd_b[...] = wd_f[...].astype(bf16)

        @pl.when(nxt_ref[i] >= 0)
        def _():
            for cp in weight_copies(nxt_ref[i]):
                cp.start(priority=1)

    @pl.when(i < nvalid)
    def _():
        pltpu.make_async_copy(xbuf.at[slot], xbuf.at[slot], xsem.at[slot]).wait()
        gather(jnp.minimum(i + 1, last_block), 1 - slot)
        x = jnp.concatenate([xbuf[slot, _token_chunk(j, blk, nchunk), :].astype(bf16) for j in range(nchunk)],
                            axis=-1)
        hgu = jnp.dot(x, wgu_b[...], preferred_element_type=f32) + bgu_ref[0, 0]
        hg = jnp.minimum(hgu[:, :dff], SWIGLU_LIMIT)
        hu = jnp.clip(hgu[:, dff:], -SWIGLU_LIMIT, SWIGLU_LIMIT)
        act = (hu + 1.0) * hg * jax.nn.sigmoid(SWIGLU_ALPHA * hg)
        _to_token_tiles(y_ref, _dot(act, wd_b[...]) + bd_ref[0, 0])

    @pl.when(i >= nvalid)
    def _():
        y_ref[...] = jnp.zeros_like(y_ref)

    @pl.when(i == nvalid)
    def _():
        pltpu.make_async_copy(xbuf.at[slot], xbuf.at[slot], xsem.at[slot]).wait()


def _experts(ht, row_tok, block_e, nvalid, next_e, w_gu, b_gu, w_down, b_down, l, blk):
    depth, n_exp, d, dff2 = w_gu.shape
    dff = dff2 // 2
    nchunk = d // LANES
    n_rows = row_tok.shape[0]
    bspec = lambda cols: pl.BlockSpec((1, 1, 1, cols), lambda i, be, *_: (l, be[i], 0, 0))
    grid_spec = pltpu.PrefetchScalarGridSpec(
        num_scalar_prefetch=4,
        grid=(n_rows // blk,),
        in_specs=[pl.BlockSpec(memory_space=pl.ANY), pl.BlockSpec(memory_space=pl.ANY), bspec(dff2),
                  pl.BlockSpec(memory_space=pl.ANY), bspec(d)],
        out_specs=pl.BlockSpec((blk * nchunk, LANES), lambda i, *_: (i, 0)),
        scratch_shapes=[pltpu.VMEM((2, blk * nchunk, LANES), f32), pltpu.SemaphoreType.DMA((2,)),
                        pltpu.VMEM((d, dff2), f32), pltpu.VMEM((dff, d), f32), pltpu.SemaphoreType.DMA((2,)),
                        pltpu.VMEM((d, dff2), bf16), pltpu.VMEM((dff, d), bf16)],
    )
    return pl.pallas_call(
        functools.partial(_expert_kernel, layer=l, blk=blk),
        grid_spec=grid_spec,
        out_shape=jax.ShapeDtypeStruct((n_rows * nchunk, LANES), f32),
        compiler_params=_params("arbitrary"),
        name="moe_experts",
    )(block_e, nvalid, next_e, row_tok, ht, w_gu, b_gu.reshape(depth, n_exp, 1, dff2), w_down,
      b_down.reshape(depth, n_exp, 1, d))


def _combine_kernel(dest_ref, gate_ref, xa_ref, xb_ref, yb_hbm, g_ref, oa_ref, ob_ref, ybuf, acc_s, sem,
                    *, tm, n, na_tiles, final):
    i = pl.program_id(0)
    slot = i % 2
    nchunk = ybuf.shape[2] // tm
    last_tile = pl.num_programs(0) - 1

    def copy_row(tile, into, kk, t, prio):
        row = dest_ref[kk * n + tile * tm + t]
        pltpu.make_async_copy(yb_hbm.at[_token_rows(row, nchunk)], ybuf.at[into, kk, _token_rows(t, nchunk)],
                              sem.at[into]).start(priority=prio)

    @pl.when(i == 0)
    def _():
        for kk in range(TOP_K):
            def issue(pair, c, kk=kk):
                for prio in range(DMA_PRIORITIES):
                    copy_row(0, 0, kk, pair * DMA_PRIORITIES + prio, prio)
                return c
            lax.fori_loop(0, tm // DMA_PRIORITIES, issue, 0, unroll=4)

    pltpu.make_async_copy(ybuf.at[slot], ybuf.at[slot], sem.at[slot]).wait()
    nxt = jnp.minimum(i + 1, last_tile)
    for kk in range(TOP_K):
        for t in range(tm):
            copy_row(nxt, 1 - slot, kk, t, t % DMA_PRIORITIES)
    for t in range(tm):
        tok = i * tm + t
        acc = gate_ref[tok] * ybuf[slot, 0, t * nchunk:(t + 1) * nchunk, :]
        for kk in range(1, TOP_K):
            acc = acc + gate_ref[kk * n + tok] * ybuf[slot, kk, t * nchunk:(t + 1) * nchunk, :]
        acc_s[t * nchunk:(t + 1) * nchunk, :] = acc
    moe = jnp.concatenate([acc_s[_token_chunk(j, tm, nchunk), :] for j in range(nchunk)], axis=-1)
    x3 = jnp.where(i < na_tiles, xa_ref[...], xb_ref[...]) + moe
    out = _rms(x3, g_ref[...]) if final else x3

    @pl.when(i < na_tiles)
    def _():
        oa_ref[...] = out

    @pl.when(i >= na_tiles)
    def _():
        ob_ref[...] = out

    @pl.when(i == last_tile)
    def _():
        pltpu.make_async_copy(ybuf.at[1 - slot], ybuf.at[1 - slot], sem.at[1 - slot]).wait()


def _combine(dest_km, xa, xb, gate, yb, g_final, final):
    d = xa.shape[1]
    n = xa.shape[0] + xb.shape[0]
    tm = COMBINE_TILE
    assert xa.shape[0] % tm == 0 and xb.shape[0] % tm == 0
    na_tiles = xa.shape[0] // tm
    sa, sb = _two_group_specs(tm, d, na_tiles)
    nchunk = d // LANES
    grid_spec = pltpu.PrefetchScalarGridSpec(
        num_scalar_prefetch=2,
        grid=(n // tm,),
        in_specs=[sa, sb, pl.BlockSpec(memory_space=pl.ANY), pl.BlockSpec((1, d), lambda i, *_: (0, 0))],
        out_specs=[sa, sb],
        scratch_shapes=[pltpu.VMEM((2, TOP_K, tm * nchunk, LANES), f32), pltpu.VMEM((tm * nchunk, LANES), f32),
                        pltpu.SemaphoreType.DMA((2,))],
    )
    return pl.pallas_call(
        functools.partial(_combine_kernel, tm=tm, n=n, na_tiles=na_tiles, final=final),
        grid_spec=grid_spec,
        out_shape=[jax.ShapeDtypeStruct(xa.shape, f32), jax.ShapeDtypeStruct(xb.shape, f32)],
        compiler_params=_params("arbitrary"),
        name="moe_combine",
    )(dest_km, gate.reshape(-1), xa, xb, yb, g_final.reshape(1, d))


def _moe(xa, xb, g_ffn, router_w, router_b, w_gu, b_gu, w_down, b_down, l, g_out, final):
    n = xa.shape[0] + xb.shape[0]
    n_exp = router_w.shape[1]
    blk = MOE_ROWS
    ht, top_e, top_rank, gate, counts = _router(xa, xb, g_ffn, router_w, router_b)

    counts = counts.reshape(n_exp).astype(jnp.int32)
    padded = (counts + blk - 1) // blk * blk
    pad_end = jnp.cumsum(padded)
    pad_start = pad_end - padded
    lookup = lambda table, idx: jnp.sum(jnp.where(idx[..., None] == jnp.arange(n_exp), table, 0), axis=-1)
    dest_km = (lookup(pad_start, top_e) + top_rank).reshape(-1)
    n_rows = -(-(n * TOP_K) // blk) * blk + n_exp * blk
    block_start = jnp.arange(n_rows // blk, dtype=jnp.int32) * blk
    block_e = jnp.minimum(jnp.sum(pad_end[None, :] <= block_start[:, None], axis=1), n_exp - 1).astype(jnp.int32)
    nvalid = (pad_end[-1:] // blk).astype(jnp.int32)
    zero_start = jnp.concatenate([pad_start + counts, pad_end[-1:]]).astype(jnp.int32)
    zero_end = jnp.concatenate([pad_end, jnp.full((1,), n_rows, jnp.int32)]).astype(jnp.int32)

    has_rows = padded > 0
    ids = jnp.arange(n_exp, dtype=jnp.int32)
    after = jnp.where(has_rows[None, :] & (ids[None, :] > ids[:, None]), ids[None, :], n_exp)
    next_of = jnp.min(after, axis=1)
    next_e = lookup(jnp.where(next_of < n_exp, next_of, -1), block_e).astype(jnp.int32)

    row_tok = _invert(dest_km, zero_start, zero_end, n, n_rows)
    yb = _experts(ht, row_tok, block_e, nvalid, next_e, w_gu, b_gu, w_down, b_down, l, blk)
    return _combine(dest_km, xa, xb, gate, yb, g_out, final)


def _mixers(x, t, b, h0r, h0i, prev, s0, P, l):
    s5w = P['s5_d'].shape[-1]
    u, prk = _inproj(x, P['norm_mix_g'][l], P['w_in'][l].astype(bf16), s5w)
    ar, ai, bre, bim, cre_m, cim_m = _s5_discretize(P['s5_lambda_re'][l], P['s5_lambda_im'][l], P['s5_log_dt'][l],
                                                    P['s5_b_re'][l], P['s5_b_im'][l], P['s5_c_re'][l], P['s5_c_im'][l])
    s5y, hr, hi = _s5(u.reshape(b, t, s5w), h0r, h0i, ar, ai, bre, bim, cre_m, cim_m,
                      P['s5_d'][l].reshape(1, s5w).astype(f32), P['s5_glu_w'][l].astype(bf16),
                      P['s5_glu_b'][l].reshape(1, s5w).astype(f32))
    s5y = s5y.reshape(b * t, s5w)
    rwy, s_fin = _rwkv(prk, prev, s0, t, b, P['rwkv_mu'][l], P['rwkv_w0'][l], P['rwkv_w1'][l], P['rwkv_w2'][l],
                       P['rwkv_a0'][l], P['rwkv_a1'][l], P['rwkv_a2'][l], P['rwkv_g1'][l], P['rwkv_g2'][l],
                       P['rwkv_k_k'][l], P['rwkv_k_a'][l], P['rwkv_r_k'][l], P['rwkv_ln_w'][l], P['rwkv_ln_b'][l])
    x1, q = _outproj(x, s5y, rwy, P['w_out'][l].astype(bf16), P['norm_ca_g'][l], P['w_cq'][l].astype(bf16))
    shift_last = prk.reshape(b, t, -1)[:, t - 1]
    return x1, q, hr, hi, shift_last, s_fin


def kernel(x_prompt, x_sample, mem_prompt, state_s5_re, state_s5_im, state_rwkv_shift, state_rwkv_wkv, cache_mem_k, cache_mem_v, norm_mix_g, w_in, s5_lambda_re, s5_lambda_im, s5_log_dt, s5_b_re, s5_b_im, s5_c_re, s5_c_im, s5_d, s5_glu_w, s5_glu_b, rwkv_mu, rwkv_w0, rwkv_w1, rwkv_w2, rwkv_a0, rwkv_a1, rwkv_a2, rwkv_g1, rwkv_g2, rwkv_k_k, rwkv_k_a, rwkv_r_k, rwkv_ln_w, rwkv_ln_b, w_out, norm_ca_g, norm_mem_g, w_cq, w_mem_kv, w_co, norm_ffn_g, router_w, router_b, ex_w_gu, ex_b_gu, ex_w_down, ex_b_down, norm_final_g):
    P = dict(norm_mix_g=norm_mix_g, w_in=w_in, s5_lambda_re=s5_lambda_re, s5_lambda_im=s5_lambda_im,
             s5_log_dt=s5_log_dt, s5_b_re=s5_b_re, s5_b_im=s5_b_im, s5_c_re=s5_c_re, s5_c_im=s5_c_im,
             s5_d=s5_d, s5_glu_w=s5_glu_w, s5_glu_b=s5_glu_b, rwkv_mu=rwkv_mu, rwkv_w0=rwkv_w0,
             rwkv_w1=rwkv_w1, rwkv_w2=rwkv_w2, rwkv_a0=rwkv_a0, rwkv_a1=rwkv_a1, rwkv_a2=rwkv_a2,
             rwkv_g1=rwkv_g1, rwkv_g2=rwkv_g2, rwkv_k_k=rwkv_k_k, rwkv_k_a=rwkv_k_a, rwkv_r_k=rwkv_r_k,
             rwkv_ln_w=rwkv_ln_w, rwkv_ln_b=rwkv_ln_b, w_out=w_out, norm_ca_g=norm_ca_g, w_cq=w_cq)
    depth = norm_mix_g.shape[0]
    bp, tp, d = x_prompt.shape
    bs, ts, _ = x_sample.shape
    n_mem = mem_prompt.shape[1]
    s5_groups, s5_state = s5_lambda_re.shape[1:]
    nstate = s5_groups * s5_state
    rw_in = state_rwkv_shift.shape[-1]
    heads = state_rwkv_wkv.shape[2]
    mh = cache_mem_k.shape[3]
    assert mh == MEM_HEADS

    xp = x_prompt.reshape(bp * tp, d)
    xs = x_sample.reshape(bs * ts, d)
    outs = {k: [] for k in ('p_re', 'p_im', 'p_sh', 'p_wkv', 'p_mk', 'p_mv', 's_re', 's_im', 's_sh', 's_wkv')}
    for l in range(depth):
        mk, mv = _memkv(mem_prompt.reshape(bp * n_mem, d), norm_mem_g[l], w_mem_kv[l].astype(bf16))
        zeros_state = jnp.zeros((bp, nstate), f32)
        x1p, qp, re_p, im_p, sh_p, wkv_p = _mixers(
            xp, tp, bp, zeros_state, zeros_state, jnp.zeros((bp, rw_in), f32),
            jnp.zeros((bp, heads, RWKV_HEAD, RWKV_HEAD), f32), P, l)
        x1s, qs, re_s, im_s, sh_s, wkv_s = _mixers(
            xs, ts, bs, state_s5_re[l].reshape(bs, nstate), state_s5_im[l].reshape(bs, nstate),
            state_rwkv_shift[l], state_rwkv_wkv[l], P, l)
        wco = w_co[l].astype(bf16)
        x2p = _attn(qp, x1p, mk, mv, wco, tp, bp)
        x2s = _attn(qs, x1s, _kv_to_head_tiles(cache_mem_k[l]), _kv_to_head_tiles(cache_mem_v[l]), wco, ts, bs)
        last = l == depth - 1
        xp, xs = _moe(x2p, x2s, norm_ffn_g[l], router_w[l], router_b[l], ex_w_gu, ex_b_gu, ex_w_down, ex_b_down,
                      l, norm_final_g, last)
        outs['p_re'].append(re_p.reshape(bp, s5_groups, s5_state))
        outs['p_im'].append(im_p.reshape(bp, s5_groups, s5_state))
        outs['p_sh'].append(sh_p)
        outs['p_wkv'].append(wkv_p)
        outs['p_mk'].append(_kv_from_head_tiles(mk, bp, n_mem, mh, d // mh))
        outs['p_mv'].append(_kv_from_head_tiles(mv, bp, n_mem, mh, d // mh))
        outs['s_re'].append(re_s.reshape(bs, s5_groups, s5_state))
        outs['s_im'].append(im_s.reshape(bs, s5_groups, s5_state))
        outs['s_sh'].append(sh_s)
        outs['s_wkv'].append(wkv_s)
    st = lambda k: jnp.stack(outs[k])
    return (xp.reshape(bp, tp, d), xs.reshape(bs, ts, d), st('p_re'), st('p_im'), st('p_sh'), st('p_wkv'),
            st('p_mk'), st('p_mv'), st('s_re'), st('s_im'), st('s_sh'), st('s_wkv'))
```
